```python
import jax, jax.numpy as jnp
from jax import lax
import numpy as np

D_MODEL = 1024
BATCH = 16
SEQ = 2048
DEPTH = 1

HEAD_DIM = 64
A_GROUPS = 6
B_HEADS = 6
M_HEADS = 4
A_WIDTH = A_GROUPS * HEAD_DIM
B_WIDTH = B_HEADS * HEAD_DIM
M_WIDTH = M_HEADS * HEAD_DIM
D_MIX = A_WIDTH + B_WIDTH + M_WIDTH
IN_COLS = 2 * A_WIDTH + 3 * B_WIDTH + B_HEADS + M_WIDTH
CHUNK = 128
QBLOCK = 128
MEM_TOKENS = 256
D_FF = ((8 * D_MODEL // 3 + 255) // 256) * 256
EPS = 1e-6
NEG_INF = -1e30

kernel_name = "hybrid_gmlp_fox_memxattn_block"


def rms_norm(x, g):
    xf = x.astype(jnp.float32)
    y = xf * lax.rsqrt(jnp.mean(xf * xf, axis=-1, keepdims=True) + EPS)
    return (y * g.astype(jnp.float32)).astype(x.dtype)


def chunked_sgu(zu, zv, g_sgu, w_s, b_s):
    B, S, _ = zu.shape
    u = jax.nn.gelu(zu)
    v = rms_norm(jax.nn.gelu(zv), g_sgu)
    v = v.reshape(B, S // CHUNK, CHUNK, A_GROUPS, HEAD_DIM)
    causal = jnp.tril(jnp.ones((CHUNK, CHUNK), dtype=bool))
    ws = jnp.where(causal[None], w_s, jnp.zeros_like(w_s))
    z = jnp.einsum('gts,bcsgd->bctgd', ws, v) + b_s.T[None, None, :, :, None]
    return u * z.reshape(B, S, A_WIDTH)


def forgetting_attention(q, k, v, f_logit):
    B, S, _ = q.shape
    q = q.reshape(B, S, B_HEADS, HEAD_DIM).transpose(0, 2, 1, 3)
    k = k.reshape(B, S, B_HEADS, HEAD_DIM).transpose(0, 2, 1, 3)
    v = v.reshape(B, S, B_HEADS, HEAD_DIM).transpose(0, 2, 1, 3)
    log_f = jax.nn.log_sigmoid(f_logit.astype(jnp.float32))
    c = jnp.cumsum(log_f, axis=1).transpose(0, 2, 1)
    nb = S // QBLOCK
    qb = q.reshape(B, B_HEADS, nb, QBLOCK, HEAD_DIM).transpose(2, 0, 1, 3, 4)
    cb = c.reshape(B, B_HEADS, nb, QBLOCK).transpose(2, 0, 1, 3)
    kpos = jnp.arange(S)
    scale = HEAD_DIM ** -0.5

    def block(args):
        qi, ci, i = args
        s = jnp.einsum('bhqd,bhkd->bhqk', qi, k).astype(jnp.float32) * scale
        s = s + ci[..., None] - c[:, :, None, :]
        qpos = i * QBLOCK + jnp.arange(QBLOCK)
        s = jnp.where(kpos[None, :] <= qpos[:, None], s, NEG_INF)
        p = jax.nn.softmax(s, axis=-1).astype(v.dtype)
        return jnp.einsum('bhqk,bhkd->bhqd', p, v)

    o = lax.map(block, (qb, cb, jnp.arange(nb)))
    return o.transpose(1, 0, 3, 2, 4).reshape(B, S, B_WIDTH)


def memory_attention(qm, mem_n, w_mem_kv):
    B, S, _ = qm.shape
    M = mem_n.shape[1]
    kv = mem_n @ w_mem_kv
    km, vm = jnp.split(kv, 2, axis=-1)
    qm = qm.reshape(B, S, M_HEADS, HEAD_DIM)
    km = km.reshape(B, M, M_HEADS, HEAD_DIM)
    vm = vm.reshape(B, M, M_HEADS, HEAD_DIM)
    s = jnp.einsum('bshd,bmhd->bhsm', qm, km).astype(jnp.float32) * HEAD_DIM ** -0.5
    p = jax.nn.softmax(s, axis=-1).astype(vm.dtype)
    return jnp.einsum('bhsm,bmhd->bshd', p, vm).reshape(B, S, M_WIDTH)


def setup_inputs(seed: int = 0) -> dict:
    key = jax.random.key(seed)
    ks = jax.random.split(key, 24)
    f32 = jnp.float32

    def nrm(k, shape, scale):
        return jax.random.normal(k, shape, f32) * scale

    def gain(k, shape):
        return 1.0 + 0.02 * jax.random.normal(k, shape, f32)

    L = DEPTH
    return {
        "x": jax.random.normal(ks[0], (BATCH, SEQ, D_MODEL), f32),
        "mem": jax.random.normal(ks[1], (BATCH, MEM_TOKENS, D_MODEL), f32),
        "g_pre_mix": gain(ks[2], (L, D_MODEL)),
        "w_in": nrm(ks[3], (L, D_MODEL, IN_COLS), D_MODEL ** -0.5),
        "b_f": 3.0 + 0.5 * jax.random.normal(ks[4], (L, B_HEADS), f32),
        "g_sgu": gain(ks[5], (L, A_WIDTH)),
        "w_s": nrm(ks[6], (L, A_GROUPS, CHUNK, CHUNK), CHUNK ** -0.5),
        "b_s": gain(ks[7], (L, A_GROUPS, CHUNK)),
        "g_out_a": gain(ks[8], (L, A_WIDTH)),
        "g_out_b": gain(ks[9], (L, B_WIDTH)),
        "g_out_m": gain(ks[10], (L, M_WIDTH)),
        "g_mem": gain(ks[11], (L, D_MODEL)),
        "w_mem_kv": nrm(ks[12], (L, D_MODEL, 2 * M_WIDTH), D_MODEL ** -0.5),
        "w_out": nrm(ks[13], (L, D_MIX, D_MODEL), D_MIX ** -0.5),
        "g_post_mix": gain(ks[14], (L, D_MODEL)),
        "g_pre_ffn": gain(ks[15], (L, D_MODEL)),
        "w_gate": nrm(ks[16], (L, D_MODEL, D_FF), D_MODEL ** -0.5),
        "w_up": nrm(ks[17], (L, D_MODEL, D_FF), D_MODEL ** -0.5),
        "w_down": nrm(ks[18], (L, D_FF, D_MODEL), D_FF ** -0.5),
        "g_post_ffn": gain(ks[19], (L, D_MODEL)),
    }


def reference(x, mem, g_pre_mix, w_in, b_f, g_sgu, w_s, b_s, g_out_a, g_out_b, g_out_m,
              g_mem, w_mem_kv, w_out, g_post_mix, g_pre_ffn, w_gate, w_up, w_down, g_post_ffn):
    split_at = [A_WIDTH, 2 * A_WIDTH,
                2 * A_WIDTH + B_WIDTH, 2 * A_WIDTH + 2 * B_WIDTH, 2 * A_WIDTH + 3 * B_WIDTH,
                2 * A_WIDTH + 3 * B_WIDTH + B_HEADS]
    for l in range(DEPTH):
        h = rms_norm(x, g_pre_mix[l])
        proj = h @ w_in[l]
        zu, zv, q, k, v, f_logit, qm = jnp.split(proj, split_at, axis=-1)
        y_a = chunked_sgu(zu, zv, g_sgu[l], w_s[l], b_s[l])
        y_b = forgetting_attention(q, k, v, f_logit + b_f[l])
        y_m = memory_attention(qm, rms_norm(mem, g_mem[l]), w_mem_kv[l])
        y = jnp.concatenate([rms_norm(y_a, g_out_a[l]),
                             rms_norm(y_b, g_out_b[l]),
                             rms_norm(y_m, g_out_m[l])], axis=-1)
        x = x + rms_norm(y @ w_out[l], g_post_mix[l])
        h = rms_norm(x, g_pre_ffn[l])
        ff = (jax.nn.silu(h @ w_gate[l]) * (h @ w_up[l])) @ w_down[l]
        x = x + rms_norm(ff, g_post_ffn[l])
    return x
```

```python
import functools

import jax
import jax.numpy as jnp
from jax import lax
from jax.experimental import pallas as pl
from jax.experimental.pallas import tpu as pltpu

D_MODEL = 1024
HEAD_DIM = 64
A_GROUPS = 6
B_HEADS = 6
M_HEADS = 4
A_WIDTH = A_GROUPS * HEAD_DIM
B_WIDTH = B_HEADS * HEAD_DIM
M_WIDTH = M_HEADS * HEAD_DIM
CHUNK = 128
MEM_TOKENS = 256
D_FF = 2816
EPS = 1e-6
NEG_INF = -1e30

LANES = 128
PAIRS = B_HEADS // 2
F_OFF = 2 * A_WIDTH + 3 * B_WIDTH + M_WIDTH
IN_COLS_PAD = F_OFF + LANES
SCALE = HEAD_DIM ** -0.5

TM_IN = 512
TQ = 256
TK = 256
TM_FFN = 512
FF_SPLIT = 2

VMEM_LIMIT = 56 * 1024 * 1024

_F32 = jnp.float32
_BF16 = jnp.bfloat16


def _dot(a, b):
    return jnp.dot(a, b, preferred_element_type=_F32)


def _dot_nt(a, b):
    return lax.dot_general(a, b, (((1,), (1,)), ((), ())), preferred_element_type=_F32)


def _rms(x, g):
    return x * lax.rsqrt(jnp.mean(x * x, axis=-1, keepdims=True) + EPS) * g


def _gelu_tanh(x):
    return 0.5 * x * (1.0 + jnp.tanh(0.7978845608028654 * (x + 0.044715 * (x * x * x))))


def _log_sigmoid(x):
    return -(jnp.maximum(-x, 0.0) + jnp.log1p(jnp.exp(-jnp.abs(x))))


def _split3(x):
    hi = x.astype(_BF16).astype(_F32)
    r = x - hi
    mid = r.astype(_BF16).astype(_F32)
    lo = (r - mid).astype(_BF16).astype(_F32)
    return hi, mid, lo


def _mem_kv_kernel(mem_ref, g_ref, w_ref, km_ref, vm_ref):
    mn = _rms(mem_ref[0], g_ref[...]).astype(_BF16)
    kv = _dot(mn, w_ref[...])
    km = kv[:, :M_WIDTH]
    vm = kv[:, M_WIDTH:]
    head = lax.broadcasted_iota(jnp.int32, (MEM_TOKENS, M_WIDTH), 1) // HEAD_DIM
    for h in range(M_HEADS):
        km_ref[0, h] = jnp.where(head == h, km, 0.0).astype(_BF16)
        vm_ref[0, h] = jnp.where(head == h, vm, 0.0).astype(_BF16)


def _mem_kv(mem, g_mem, w_kv):
    batch = mem.shape[0]
    out = jax.ShapeDtypeStruct((batch, M_HEADS, MEM_TOKENS, M_WIDTH), _BF16)
    return pl.pallas_call(
        _mem_kv_kernel,
        grid=(batch,),
        in_specs=[
            pl.BlockSpec((1, MEM_TOKENS, D_MODEL), lambda b: (b, 0, 0)),
            pl.BlockSpec((1, D_MODEL), lambda b: (0, 0)),
            pl.BlockSpec((D_MODEL, 2 * M_WIDTH), lambda b: (0, 0)),
        ],
        out_specs=[
            pl.BlockSpec((1, M_HEADS, MEM_TOKENS, M_WIDTH), lambda b: (b, 0, 0, 0)),
            pl.BlockSpec((1, M_HEADS, MEM_TOKENS, M_WIDTH), lambda b: (b, 0, 0, 0)),
        ],
        out_shape=[out, out],
        compiler_params=pltpu.CompilerParams(dimension_semantics=("parallel",)),
        name="mem_kv",
    )(mem, g_mem, w_kv)


def _mix_in_kernel(x_ref, gpre_ref, win_ref, bf_ref, gsgu_ref, ws_ref, bs_ref, goa_ref, gom_ref,
                   km_ref, vm_ref, qa_ref, ka_ref, va_ref, ya_ref, ym_ref, carry_ref):
    tm = x_ref.shape[1]
    h = _rms(x_ref[0], gpre_ref[...]).astype(_BF16)

    u = _gelu_tanh(_dot(h, win_ref[:, 0:A_WIDTH]))
    vn = _rms(_gelu_tanh(_dot(h, win_ref[:, A_WIDTH:2 * A_WIDTH])), gsgu_ref[...])
    lane = lax.broadcasted_iota(jnp.int32, (CHUNK, LANES), 1)
    row_s = lax.broadcasted_iota(jnp.int32, (CHUNK, 2 * CHUNK), 0)
    col_s = lax.broadcasted_iota(jnp.int32, (CHUNK, 2 * CHUNK), 1) % CHUNK
    z_pairs = []
    for p in range(PAIRS):
        w_pair = jnp.where(col_s <= row_s, ws_ref[p], 0.0).astype(_BF16)
        z_chunks = []
        for c in range(tm // CHUNK):
            v_pair = vn[c * CHUNK:(c + 1) * CHUNK, p * LANES:(p + 1) * LANES]
            rhs = jnp.concatenate([jnp.where(lane < HEAD_DIM, v_pair, 0.0),
                                   jnp.where(lane >= HEAD_DIM, v_pair, 0.0)], axis=0).astype(_BF16)
            z_chunks.append(_dot(w_pair, rhs) + bs_ref[:, p * LANES:(p + 1) * LANES])
        z_pairs.append(jnp.concatenate(z_chunks, axis=0))
    y_a = u * jnp.concatenate(z_pairs, axis=1)
    ya_ref[0] = _rms(y_a, goa_ref[...]).astype(_BF16)

    qm = (_dot(h, win_ref[:, F_OFF - M_WIDTH:F_OFF]) * SCALE).astype(_BF16)
    o_m = jnp.zeros((tm, M_WIDTH), _F32)
    for hh in range(M_HEADS):
        s = _dot_nt(qm, km_ref[0, hh])
        e = jnp.exp(s - jnp.max(s, axis=-1, keepdims=True))
        p_m = (e * (1.0 / jnp.sum(e, axis=-1, keepdims=True))).astype(_BF16)
        o_m = o_m + _dot(p_m, vm_ref[0, hh])
    ym_ref[0] = _rms(o_m, gom_ref[...]).astype(_BF16)

    @pl.when(pl.program_id(1) == 0)
    def _():
        carry_ref[...] = jnp.zeros_like(carry_ref)

    lane_t = lax.broadcasted_iota(jnp.int32, (tm, LANES), 1)
    logf = _log_sigmoid(_dot(h, win_ref[:, F_OFF:IN_COLS_PAD]) + bf_ref[...])
    logf = jnp.where(lane_t < B_HEADS, logf, 0.0)
    hi, mid, lo = _split3(logf)
    packed = (hi + pltpu.roll(mid, 8, 1) + pltpu.roll(lo, 16, 1)).astype(_BF16)
    tril = (lax.broadcasted_iota(jnp.int32, (tm, tm), 1)
            <= lax.broadcasted_iota(jnp.int32, (tm, tm), 0)).astype(_BF16)
    cs = _dot(tril, packed)
    c = cs + pltpu.roll(cs, LANES - 8, 1) + pltpu.roll(cs, LANES - 16, 1) + carry_ref[...]
    carry_ref[...] = c[tm - 1:tm, :]
    c_hi, c_mid, c_lo = _split3(c)

    q = _dot(h, win_ref[:, 2 * A_WIDTH:2 * A_WIDTH + B_WIDTH]) * SCALE
    k = _dot(h, win_ref[:, 2 * A_WIDTH + B_WIDTH:2 * A_WIDTH + 2 * B_WIDTH])
    v = _dot(h, win_ref[:, 2 * A_WIDTH + 2 * B_WIDTH:2 * A_WIDTH + 3 * B_WIDTH])
    for hd in range(B_HEADS):
        p, half = hd // 2, hd % 2
        sl = slice(p * LANES, (p + 1) * LANES)
        data = (lane_t >= half * HEAD_DIM) & (lane_t < (half + 1) * HEAD_DIM)
        e0 = (1 - half) * HEAD_DIM
        chb, cmb, clb = c_hi[:, hd:hd + 1], c_mid[:, hd:hd + 1], c_lo[:, hd:hd + 1]
        q_ex = jnp.where(lane_t == e0, chb, jnp.where(lane_t == e0 + 1, cmb, jnp.where(
            lane_t == e0 + 2, clb, jnp.where((lane_t >= e0 + 3) & (lane_t < e0 + 6), 1.0, 0.0))))
        k_ex = jnp.where(lane_t == e0 + 3, -chb, jnp.where(lane_t == e0 + 4, -cmb, jnp.where(
            lane_t == e0 + 5, -clb, jnp.where((lane_t >= e0) & (lane_t < e0 + 3), 1.0, 0.0))))
        qa_ref[0, hd] = jnp.where(data, q[:, sl], q_ex).astype(_BF16)
        ka_ref[0, hd] = jnp.where(data, k[:, sl], k_ex).astype(_BF16)
        va_ref[0, hd] = jnp.where(data, v[:, sl], jnp.where(lane_t == e0, 1.0, 0.0)).astype(_BF16)


def _mix_in(x, g_pre, w_in_p, b_f_p, g_sgu, ws_cat, bs_exp, g_out_a, g_out_m, km, vm):
    batch, seq, _ = x.shape
    tm = TM_IN
    const2 = lambda b, i: (0, 0)
    tab = jax.ShapeDtypeStruct((batch, B_HEADS, seq, LANES), _BF16)
    tab_spec = pl.BlockSpec((1, B_HEADS, tm, LANES), lambda b, i: (b, 0, i, 0))
    mem_spec = pl.BlockSpec((1, M_HEADS, MEM_TOKENS, M_WIDTH), lambda b, i: (b, 0, 0, 0))
    return pl.pallas_call(
        _mix_in_kernel,
        grid=(batch, seq // tm),
        in_specs=[
            pl.BlockSpec((1, tm, D_MODEL), lambda b, i: (b, i, 0)),
            pl.BlockSpec((1, D_MODEL), const2),
            pl.BlockSpec((D_MODEL, IN_COLS_PAD), const2),
            pl.BlockSpec((1, LANES), const2),
            pl.BlockSpec((1, A_WIDTH), const2),
            pl.BlockSpec((PAIRS, CHUNK, 2 * CHUNK), lambda b, i: (0, 0, 0)),
            pl.BlockSpec((CHUNK, A_WIDTH), const2),
            pl.BlockSpec((1, A_WIDTH), const2),
            pl.BlockSpec((1, M_WIDTH), const2),
            mem_spec, mem_spec,
        ],
        out_specs=[
            tab_spec, tab_spec, tab_spec,
            pl.BlockSpec((1, tm, A_WIDTH), lambda b, i: (b, i, 0)),
            pl.BlockSpec((1, tm, M_WIDTH), lambda b, i: (b, i, 0)),
        ],
        out_shape=[tab, tab, tab,
                   jax.ShapeDtypeStruct((batch, seq, A_WIDTH), _BF16),
                   jax.ShapeDtypeStruct((batch, seq, M_WIDTH), _BF16)],
        scratch_shapes=[pltpu.VMEM((1, LANES), _F32)],
        compiler_params=pltpu.CompilerParams(dimension_semantics=("parallel", "arbitrary"),
                                             vmem_limit_bytes=VMEM_LIMIT),
        name="mix_in",
    )(x, g_pre, w_in_p, b_f_p, g_sgu, ws_cat, bs_exp, g_out_a, g_out_m, km, vm)


def _fox_out_kernel(qa_ref, ka_ref, va_ref, ya_ref, ym_ref, x_ref, wout_ref, gob_ref, gpost_ref, x1_ref):
    i = pl.program_id(1)
    lane = lax.broadcasted_iota(jnp.int32, (TQ, LANES), 1)
    causal = (lax.broadcasted_iota(jnp.int32, (TQ, TK), 1) <= lax.broadcasted_iota(jnp.int32, (TQ, TK), 0))

    def attend(hd, j, m, acc, masked):
        start = pl.multiple_of(j * TK, TK)
        s = _dot_nt(qa_ref[0, hd], ka_ref[0, hd, pl.ds(start, TK), :])
        if masked:
            s = jnp.where(causal, s, NEG_INF)
        m_new = jnp.maximum(m, jnp.max(s, axis=-1, keepdims=True))
        p = jnp.exp(s - m_new).astype(_BF16)
        acc = jnp.exp(m - m_new) * acc + _dot(p, va_ref[0, hd, pl.ds(start, TK), :])
        return m_new, acc

    heads = []
    for hd in range(B_HEADS):
        m0 = jnp.full((TQ, 1), NEG_INF, _F32)
        acc0 = jnp.zeros((TQ, LANES), _F32)
        m, acc = lax.fori_loop(0, i, lambda j, c, hd=hd: attend(hd, j, c[0], c[1], False), (m0, acc0))
        m, acc = attend(hd, i, m, acc, True)
        e0 = (1 - hd % 2) * HEAD_DIM
        heads.append(acc * (1.0 / acc[:, e0:e0 + 1]))
    y_b = jnp.concatenate([jnp.where(lane < HEAD_DIM, heads[2 * p], heads[2 * p + 1]) for p in range(PAIRS)],
                          axis=1)
    yb_n = _rms(y_b, gob_ref[...]).astype(_BF16)

    z = (_dot(ya_ref[0], wout_ref[0:A_WIDTH, :])
         + _dot(yb_n, wout_ref[A_WIDTH:A_WIDTH + B_WIDTH, :])
         + _dot(ym_ref[0], wout_ref[A_WIDTH + B_WIDTH:, :]))
    x1_ref[0] = x_ref[0] + _rms(z, gpost_ref[...])


def _fox_out(qa, ka, va, ya, ym, x, w_out, g_out_b, g_post):
    batch, seq, _ = x.shape
    assert TQ == TK
    const2 = lambda b, i: (0, 0)
    kv_spec = pl.BlockSpec((1, B_HEADS, seq, LANES), lambda b, i: (b, 0, 0, 0))
    return pl.pallas_call(
        _fox_out_kernel,
        grid=(batch, seq // TQ),
        in_specs=[
            pl.BlockSpec((1, B_HEADS, TQ, LANES), lambda b, i: (b, 0, i, 0)),
            kv_spec, kv_spec,
            pl.BlockSpec((1, TQ, A_WIDTH), lambda b, i: (b, i, 0)),
            pl.BlockSpec((1, TQ, M_WIDTH), lambda b, i: (b, i, 0)),
            pl.BlockSpec((1, TQ, D_MODEL), lambda b, i: (b, i, 0)),
            pl.BlockSpec((D_MODEL, D_MODEL), const2),
            pl.BlockSpec((1, B_WIDTH), const2),
            pl.BlockSpec((1, D_MODEL), const2),
        ],
        out_specs=pl.BlockSpec((1, TQ, D_MODEL), lambda b, i: (b, i, 0)),
        out_shape=jax.ShapeDtypeStruct(x.shape, x.dtype),
        compiler_params=pltpu.CompilerParams(dimension_semantics=("parallel", "parallel"),
                                             vmem_limit_bytes=VMEM_LIMIT),
        name="fox_out",
    )(qa, ka, va, ya, ym, x, w_out, g_out_b, g_post)


def _ffn_kernel(x_ref, gpre_ref, wg_ref, wu_ref, wd_ref, gpost_ref, o_ref):
    x = x_ref[...]
    h = _rms(x, gpre_ref[...]).astype(_BF16)
    step = D_FF // FF_SPLIT
    ff = None
    for c in range(FF_SPLIT):
        sl = slice(c * step, (c + 1) * step)
        g = _dot(h, wg_ref[:, sl])
        a = (g * (1.0 / (1.0 + jnp.exp(-g))) * _dot(h, wu_ref[:, sl])).astype(_BF16)
        part = _dot(a, wd_ref[sl, :])
        ff = part if ff is None else ff + part
    o_ref[...] = x + _rms(ff, gpost_ref[...])


def _ffn(x, g_pre, w_gate, w_up, w_down, g_post):
    tokens = x.shape[0]
    tm = TM_FFN
    const = lambda i: (0, 0)
    resident = functools.partial(pl.BlockSpec, index_map=const, pipeline_mode=pl.Buffered(1))
    return pl.pallas_call(
        _ffn_kernel,
        grid=(tokens // tm,),
        in_specs=[
            pl.BlockSpec((tm, D_MODEL), lambda i: (i, 0)),
            pl.BlockSpec((1, D_MODEL), const),
            resident((D_MODEL, D_FF)),
            resident((D_MODEL, D_FF)),
            resident((D_FF, D_MODEL)),
            pl.BlockSpec((1, D_MODEL), const),
        ],
        out_specs=pl.BlockSpec((tm, D_MODEL), lambda i: (i, 0)),
        out_shape=jax.ShapeDtypeStruct(x.shape, x.dtype),
        compiler_params=pltpu.CompilerParams(dimension_semantics=("parallel",),
                                             vmem_limit_bytes=VMEM_LIMIT),
        name="ffn",
    )(x, g_pre, w_gate, w_up, w_down, g_post)


def kernel(x, mem, g_pre_mix, w_in, b_f, g_sgu, w_s, b_s, g_out_a, g_out_b, g_out_m, g_mem, w_mem_kv, w_out,
           g_post_mix, g_pre_ffn, w_gate, w_up, w_down, g_post_ffn):
    batch, seq, d = x.shape
    depth = w_in.shape[0]
    row = lambda a: a.reshape(1, -1)
    f_lo = 2 * A_WIDTH + 3 * B_WIDTH
    for l in range(depth):
        w_in_p = jnp.concatenate(
            [w_in[l][:, :f_lo], w_in[l][:, f_lo + B_HEADS:], w_in[l][:, f_lo:f_lo + B_HEADS],
             jnp.zeros((d, LANES - B_HEADS), w_in.dtype)], axis=1).astype(_BF16)
        b_f_p = jnp.pad(b_f[l], (0, LANES - B_HEADS)).reshape(1, LANES)
        ws_cat = w_s[l].reshape(PAIRS, 2, CHUNK, CHUNK).transpose(0, 2, 1, 3).reshape(PAIRS, CHUNK, 2 * CHUNK)
        bs_exp = jnp.repeat(b_s[l].T, HEAD_DIM, axis=1)

        km, vm = _mem_kv(mem, row(g_mem[l]), w_mem_kv[l].astype(_BF16))
        qa, ka, va, ya, ym = _mix_in(x, row(g_pre_mix[l]), w_in_p, b_f_p, row(g_sgu[l]), ws_cat, bs_exp,
                                     row(g_out_a[l]), row(g_out_m[l]), km, vm)
        x = _fox_out(qa, ka, va, ya, ym, x, w_out[l].astype(_BF16), row(g_out_b[l]), row(g_post_mix[l]))
        x = _ffn(x.reshape(batch * seq, d), row(g_pre_ffn[l]), w_gate[l].astype(_BF16), w_up[l].astype(_BF16),
                 w_down[l].astype(_BF16), row(g_post_ffn[l])).reshape(batch, seq, d)
    return x
```

```python
import functools

import jax
import jax.numpy as jnp
from jax import lax
from jax.experimental import pallas as pl
from jax.experimental.pallas import tpu as pltpu

D_MODEL = 1024
HEAD_DIM = 64
A_GROUPS = 6
B_HEADS = 6
M_HEADS = 4
A_WIDTH = A_GROUPS * HEAD_DIM
B_WIDTH = B_HEADS * HEAD_DIM
M_WIDTH = M_HEADS * HEAD_DIM
CHUNK = 128
MEM_TOKENS = 256
D_FF = 2816
EPS = 1e-6
NEG_INF = -1e30

LANES = 128
PAIRS = B_HEADS // 2
F_OFF = 2 * A_WIDTH + 3 * B_WIDTH + M_WIDTH
IN_COLS_PAD = F_OFF + LANES
SCALE = HEAD_DIM ** -0.5

TM_IN = 512
TQ = 256
TK = 256
TM_FFN = 512
FF_SPLIT = 2

VMEM_LIMIT = 56 * 1024 * 1024

_F32 = jnp.float32
_BF16 = jnp.bfloat16


def _dot(a, b):
    return jnp.dot(a, b, preferred_element_type=_F32)


def _dot_nt(a, b):
    return lax.dot_general(a, b, (((1,), (1,)), ((), ())), preferred_element_type=_F32)


def _rms(x, g):
    return x * lax.rsqrt(jnp.mean(x * x, axis=-1, keepdims=True) + EPS) * g


def _gelu_tanh(x):
    return 0.5 * x * (1.0 + jnp.tanh(0.7978845608028654 * (x + 0.044715 * (x * x * x))))


def _log_sigmoid(x):
    return -(jnp.maximum(-x, 0.0) + jnp.log1p(jnp.exp(-jnp.abs(x))))


def _split3(x):
    hi = x.astype(_BF16).astype(_F32)
    r = x - hi
    mid = r.astype(_BF16).astype(_F32)
    lo = (r - mid).astype(_BF16).astype(_F32)
    return hi, mid, lo


def _mem_kv_kernel(mem_ref, g_ref, w_ref, km_ref, vm_ref):
    mn = _rms(mem_ref[0], g_ref[...]).astype(_BF16)
    kv = _dot(mn, w_ref[...])
    km = kv[:, :M_WIDTH]
    vm = kv[:, M_WIDTH:]
    head = lax.broadcasted_iota(jnp.int32, (MEM_TOKENS, M_WIDTH), 1) // HEAD_DIM
    for h in range(M_HEADS):
        km_ref[0, h] = jnp.where(head == h, km, 0.0).astype(_BF16)
        vm_ref[0, h] = jnp.where(head == h, vm, 0.0).astype(_BF16)


def _mem_kv(mem, g_mem, w_kv):
    batch = mem.shape[0]
    out = jax.ShapeDtypeStruct((batch, M_HEADS, MEM_TOKENS, M_WIDTH), _BF16)
    return pl.pallas_call(
        _mem_kv_kernel,
        grid=(batch,),
        in_specs=[
            pl.BlockSpec((1, MEM_TOKENS, D_MODEL), lambda b: (b, 0, 0)),
            pl.BlockSpec((1, D_MODEL), lambda b: (0, 0)),
            pl.BlockSpec((D_MODEL, 2 * M_WIDTH), lambda b: (0, 0)),
        ],
        out_specs=[
            pl.BlockSpec((1, M_HEADS, MEM_TOKENS, M_WIDTH), lambda b: (b, 0, 0, 0)),
            pl.BlockSpec((1, M_HEADS, MEM_TOKENS, M_WIDTH), lambda b: (b, 0, 0, 0)),
        ],
        out_shape=[out, out],
        compiler_params=pltpu.CompilerParams(dimension_semantics=("parallel",)),
        name="mem_kv",
    )(mem, g_mem, w_kv)


def _mix_in_kernel(x_ref, gpre_ref, win_ref, bf_ref, gsgu_ref, ws_ref, bs_ref, goa_ref, gom_ref,
                   km_ref, vm_ref, qa_ref, ka_ref, va_ref, ya_ref, ym_ref, carry_ref):
    tm = x_ref.shape[1]
    h = _rms(x_ref[0], gpre_ref[...]).astype(_BF16)

    u = _gelu_tanh(_dot(h, win_ref[:, 0:A_WIDTH]))
    vn = _rms(_gelu_tanh(_dot(h, win_ref[:, A_WIDTH:2 * A_WIDTH])), gsgu_ref[...])
    lane = lax.broadcasted_iota(jnp.int32, (CHUNK, LANES), 1)
    row_s = lax.broadcasted_iota(jnp.int32, (CHUNK, 2 * CHUNK), 0)
    col_s = lax.broadcasted_iota(jnp.int32, (CHUNK, 2 * CHUNK), 1) % CHUNK
    z_pairs = []
    for p in range(PAIRS):
        w_pair = jnp.where(col_s <= row_s, ws_ref[p], 0.0).astype(_BF16)
        z_chunks = []
        for c in range(tm // CHUNK):
            v_pair = vn[c * CHUNK:(c + 1) * CHUNK, p * LANES:(p + 1) * LANES]
            rhs = jnp.concatenate([jnp.where(lane < HEAD_DIM, v_pair, 0.0),
                                   jnp.where(lane >= HEAD_DIM, v_pair, 0.0)], axis=0).astype(_BF16)
            z_chunks.append(_dot(w_pair, rhs) + bs_ref[:, p * LANES:(p + 1) * LANES])
        z_pairs.append(jnp.concatenate(z_chunks, axis=0))
    y_a = u * jnp.concatenate(z_pairs, axis=1)
    ya_ref[0] = _rms(y_a, goa_ref[...]).astype(_BF16)

    qm = (_dot(h, win_ref[:, F_OFF - M_WIDTH:F_OFF]) * SCALE).astype(_BF16)
    o_m = jnp.zeros((tm, M_WIDTH), _F32)
    for hh in range(M_HEADS):
        s = _dot_nt(qm, km_ref[0, hh])
        e = jnp.exp(s - jnp.max(s, axis=-1, keepdims=True))
        p_m = (e * (1.0 / jnp.sum(e, axis=-1, keepdims=True))).astype(_BF16)
        o_m = o_m + _dot(p_m, vm_ref[0, hh])
    ym_ref[0] = _rms(o_m, gom_ref[...]).astype(_BF16)

    @pl.when(pl.program_id(1) == 0)
    def _():
        carry_ref[...] = jnp.zeros_like(carry_ref)

    lane_t = lax.broadcasted_iota(jnp.int32, (tm, LANES), 1)
    logf = _log_sigmoid(_dot(h, win_ref[:, F_OFF:IN_COLS_PAD]) + bf_ref[...])
    logf = jnp.where(lane_t < B_HEADS, logf, 0.0)
    hi, mid, lo = _split3(logf)
    packed = (hi + pltpu.roll(mid, 8, 1) + pltpu.roll(lo, 16, 1)).astype(_BF16)
    tril = (lax.broadcasted_iota(jnp.int32, (tm, tm), 1)
            <= lax.broadcasted_iota(jnp.int32, (tm, tm), 0)).astype(_BF16)
    cs = _dot(tril, packed)
    c = cs + pltpu.roll(cs, LANES - 8, 1) + pltpu.roll(cs, LANES - 16, 1) + carry_ref[...]
    carry_ref[...] = c[tm - 1:tm, :]
    c_hi, c_mid, c_lo = _split3(c)

    q = _dot(h, win_ref[:, 2 * A_WIDTH:2 * A_WIDTH + B_WIDTH]) * SCALE
    k = _dot(h, win_ref[:, 2 * A_WIDTH + B_WIDTH:2 * A_WIDTH + 2 * B_WIDTH])
    v = _dot(h, win_ref[:, 2 * A_WIDTH + 2 * B_WIDTH:2 * A_WIDTH + 3 * B_WIDTH])
    for hd in range(B_HEADS):
        p, half = hd // 2, hd % 2
        sl = slice(p * LANES, (p + 1) * LANES)
        data = (lane_t >= half * HEAD_DIM) & (lane_t < (half + 1) * HEAD_DIM)
        e0 = (1 - half) * HEAD_DIM
        chb, cmb, clb = c_hi[:, hd:hd + 1], c_mid[:, hd:hd + 1], c_lo[:, hd:hd + 1]
        q_ex = jnp.where(lane_t == e0, chb, jnp.where(lane_t == e0 + 1, cmb, jnp.where(
            lane_t == e0 + 2, clb, jnp.where((lane_t >= e0 + 3) & (lane_t < e0 + 6), 1.0, 0.0))))
        k_ex = jnp.where(lane_t == e0 + 3, -chb, jnp.where(lane_t == e0 + 4, -cmb, jnp.where(
            lane_t == e0 + 5, -clb, jnp.where((lane_t >= e0) & (lane_t < e0 + 3), 1.0, 0.0))))
        qa_ref[0, hd] = jnp.where(data, q[:, sl], q_ex).astype(_BF16)
        ka_ref[0, hd] = jnp.where(data, k[:, sl], k_ex).astype(_BF16)
        va_ref[0, hd] = jnp.where(data, v[:, sl], jnp.where(lane_t == e0, 1.0, 0.0)).astype(_BF16)


def _mix_in(x, g_pre, w_in_p, b_f_p, g_sgu, ws_cat, bs_exp, g_out_a, g_out_m, km, vm):
    batch, seq, _ = x.shape
    tm = TM_IN
    const2 = lambda b, i: (0, 0)
    tab = jax.ShapeDtypeStruct((batch, B_HEADS, seq, LANES), _BF16)
    tab_spec = pl.BlockSpec((1, B_HEADS, tm, LANES), lambda b, i: (b, 0, i, 0))
    mem_spec = pl.BlockSpec((1, M_HEADS, MEM_TOKENS, M_WIDTH), lambda b, i: (b, 0, 0, 0))
    return pl.pallas_call(
        _mix_in_kernel,
        grid=(batch, seq // tm),
        in_specs=[
            pl.BlockSpec((1, tm, D_MODEL), lambda b, i: (b, i, 0)),
            pl.BlockSpec((1, D_MODEL), const2),
            pl.BlockSpec((D_MODEL, IN_COLS_PAD), const2),
            pl.BlockSpec((1, LANES), const2),
            pl.BlockSpec((1, A_WIDTH), const2),
            pl.BlockSpec((PAIRS, CHUNK, 2 * CHUNK), lambda b, i: (0, 0, 0)),
            pl.BlockSpec((CHUNK, A_WIDTH), const2),
            pl.BlockSpec((1, A_WIDTH), const2),
            pl.BlockSpec((1, M_WIDTH), const2),
            mem_spec, mem_spec,
        ],
        out_specs=[
            tab_spec, tab_spec, tab_spec,
            pl.BlockSpec((1, tm, A_WIDTH), lambda b, i: (b, i, 0)),
            pl.BlockSpec((1, tm, M_WIDTH), lambda b, i: (b, i, 0)),
        ],
        out_shape=[tab, tab, tab,
                   jax.ShapeDtypeStruct((batch, seq, A_WIDTH), _BF16),
                   jax.ShapeDtypeStruct((batch, seq, M_WIDTH), _BF16)],
        scratch_shapes=[pltpu.VMEM((1, LANES), _F32)],
        compiler_params=pltpu.CompilerParams(dimension_semantics=("parallel", "arbitrary"),
                                             vmem_limit_bytes=VMEM_LIMIT),
        name="mix_in",
    )(x, g_pre, w_in_p, b_f_p, g_sgu, ws_cat, bs_exp, g_out_a, g_out_m, km, vm)


def _fox_out_kernel(qa_ref, ka_ref, va_ref, ya_ref, ym_ref, x_ref, wout_ref, gob_ref, gpost_ref, x1_ref):
    i = pl.program_id(1)
    lane = lax.broadcasted_iota(jnp.int32, (TQ, LANES), 1)
    causal = (lax.broadcasted_iota(jnp.int32, (TQ, TK), 1) <= lax.broadcasted_iota(jnp.int32, (TQ, TK), 0))

    def attend(j, carry, masked):
        start = pl.multiple_of(j * TK, TK)
        out = []
        for hd in range(B_HEADS):
            m, acc = carry[hd]
            s = _dot_nt(qa_ref[0, hd], ka_ref[0, hd, pl.ds(start, TK), :])
            if masked:
                s = jnp.where(causal, s, NEG_INF)
            m_new = jnp.maximum(m, jnp.max(s, axis=-1, keepdims=True))
            p = jnp.exp(s - m_new).astype(_BF16)
            acc = jnp.exp(m - m_new) * acc + _dot(p, va_ref[0, hd, pl.ds(start, TK), :])
            out.append((m_new, acc))
        return tuple(out)

    init = tuple((jnp.full((TQ, 1), NEG_INF, _F32), jnp.zeros((TQ, LANES), _F32)) for _ in range(B_HEADS))
    state = lax.fori_loop(0, i, lambda j, c: attend(j, c, False), init)
    state = attend(i, state, True)
    heads = []
    for hd in range(B_HEADS):
        acc = state[hd][1]
        e0 = (1 - hd % 2) * HEAD_DIM
        heads.append(acc * (1.0 / acc[:, e0:e0 + 1]))
    y_b = jnp.concatenate([jnp.where(lane < HEAD_DIM, heads[2 * p], heads[2 * p + 1]) for p in range(PAIRS)],
                          axis=1)
    yb_n = _rms(y_b, gob_ref[...]).astype(_BF16)

    z = (_dot(ya_ref[0], wout_ref[0:A_WIDTH, :])
         + _dot(yb_n, wout_ref[A_WIDTH:A_WIDTH + B_WIDTH, :])
         + _dot(ym_ref[0], wout_ref[A_WIDTH + B_WIDTH:, :]))
    x1_ref[0] = x_ref[0] + _rms(z, gpost_ref[...])


def _fox_out(qa, ka, va, ya, ym, x, w_out, g_out_b, g_post):
    batch, seq, _ = x.shape
    assert TQ == TK
    const2 = lambda b, i: (0, 0)
    kv_spec = pl.BlockSpec((1, B_HEADS, seq, LANES), lambda b, i: (b, 0, 0, 0))
    return pl.pallas_call(
        _fox_out_kernel,
        grid=(batch, seq // TQ),
        in_specs=[
            pl.BlockSpec((1, B_HEADS, TQ, LANES), lambda b, i: (b, 0, i, 0)),
            kv_spec, kv_spec,
            pl.BlockSpec((1, TQ, A_WIDTH), lambda b, i: (b, i, 0)),
            pl.BlockSpec((1, TQ, M_WIDTH), lambda b, i: (b, i, 0)),
            pl.BlockSpec((1, TQ, D_MODEL), lambda b, i: (b, i, 0)),
            pl.BlockSpec((D_MODEL, D_MODEL), const2),
            pl.BlockSpec((1, B_WIDTH), const2),
            pl.BlockSpec((1, D_MODEL), const2),
        ],
        out_specs=pl.BlockSpec((1, TQ, D_MODEL), lambda b, i: (b, i, 0)),
        out_shape=jax.ShapeDtypeStruct(x.shape, x.dtype),
        compiler_params=pltpu.CompilerParams(dimension_semantics=("parallel", "parallel"),
                                             vmem_limit_bytes=VMEM_LIMIT),
        name="fox_out",
    )(qa, ka, va, ya, ym, x, w_out, g_out_b, g_post)


def _ffn_kernel(x_ref, gpre_ref, wg_ref, wu_ref, wd_ref, gpost_ref, o_ref):
    x = x_ref[...]
    h = _rms(x, gpre_ref[...]).astype(_BF16)
    step = D_FF // FF_SPLIT
    ff = None
    for c in range(FF_SPLIT):
        sl = slice(c * step, (c + 1) * step)
        g = _dot(h, wg_ref[:, sl])
        a = (g * (1.0 / (1.0 + jnp.exp(-g))) * _dot(h, wu_ref[:, sl])).astype(_BF16)
        part = _dot(a, wd_ref[sl, :])
        ff = part if ff is None else ff + part
    o_ref[...] = x + _rms(ff, gpost_ref[...])


def _ffn(x, g_pre, w_gate, w_up, w_down, g_post):
    tokens = x.shape[0]
    tm = TM_FFN
    const = lambda i: (0, 0)
    resident = functools.partial(pl.BlockSpec, index_map=const, pipeline_mode=pl.Buffered(1))
    return pl.pallas_call(
        _ffn_kernel,
        grid=(tokens // tm,),
        in_specs=[
            pl.BlockSpec((tm, D_MODEL), lambda i: (i, 0)),
            pl.BlockSpec((1, D_MODEL), const),
            resident((D_MODEL, D_FF)),
            resident((D_MODEL, D_FF)),
            resident((D_FF, D_MODEL)),
            pl.BlockSpec((1, D_MODEL), const),
        ],
        out_specs=pl.BlockSpec((tm, D_MODEL), lambda i: (i, 0)),
        out_shape=jax.ShapeDtypeStruct(x.shape, x.dtype),
        compiler_params=pltpu.CompilerParams(dimension_semantics=("parallel",),
                                             vmem_limit_bytes=VMEM_LIMIT),
        name="ffn",
    )(x, g_pre, w_gate, w_up, w_down, g_post)


def kernel(x, mem, g_pre_mix, w_in, b_f, g_sgu, w_s, b_s, g_out_a, g_out_b, g_out_m, g_mem, w_mem_kv, w_out,
           g_post_mix, g_pre_ffn, w_gate, w_up, w_down, g_post_ffn):
    batch, seq, d = x.shape
    depth = w_in.shape[0]
    row = lambda a: a.reshape(1, -1)
    f_lo = 2 * A_WIDTH + 3 * B_WIDTH
    for l in range(depth):
        w_in_p = jnp.concatenate(
            [w_in[l][:, :f_lo], w_in[l][:, f_lo + B_HEADS:], w_in[l][:, f_lo:f_lo + B_HEADS],
             jnp.zeros((d, LANES - B_HEADS), w_in.dtype)], axis=1).astype(_BF16)
        b_f_p = jnp.pad(b_f[l], (0, LANES - B_HEADS)).reshape(1, LANES)
        ws_cat = w_s[l].reshape(PAIRS, 2, CHUNK, CHUNK).transpose(0, 2, 1, 3).reshape(PAIRS, CHUNK, 2 * CHUNK)
        bs_exp = jnp.repeat(b_s[l].T, HEAD_DIM, axis=1)

        km, vm = _mem_kv(mem, row(g_mem[l]), w_mem_kv[l].astype(_BF16))
        qa, ka, va, ya, ym = _mix_in(x, row(g_pre_mix[l]), w_in_p, b_f_p, row(g_sgu[l]), ws_cat, bs_exp,
                                     row(g_out_a[l]), row(g_out_m[l]), km, vm)
        x = _fox_out(qa, ka, va, ya, ym, x, w_out[l].astype(_BF16), row(g_out_b[l]), row(g_post_mix[l]))
        x = _ffn(x.reshape(batch * seq, d), row(g_pre_ffn[l]), w_gate[l].astype(_BF16), w_up[l].astype(_BF16),
                 w_down[l].astype(_BF16), row(g_post_ffn[l])).reshape(batch, seq, d)
    return x
```

```python
import functools

import jax
import jax.numpy as jnp
from jax import lax
from jax.experimental import pallas as pl
from jax.experimental.pallas import tpu as pltpu

D_MODEL = 1024
HEAD_DIM = 64
A_GROUPS = 6
B_HEADS = 6
M_HEADS = 4
A_WIDTH = A_GROUPS * HEAD_DIM
B_WIDTH = B_HEADS * HEAD_DIM
M_WIDTH = M_HEADS * HEAD_DIM
CHUNK = 128
MEM_TOKENS = 256
D_FF = 2816
EPS = 1e-6
NEG_INF = -1e30

LANES = 128
SUBLANES = 8
PAIRS = B_HEADS // 2
SCALE = HEAD_DIM ** -0.5

K_OFF = 2 * A_WIDTH
QM_OFF = K_OFF + B_WIDTH
NN_COLS = QM_OFF + M_WIDTH
V_ROW = B_WIDTH
F_ROW = 2 * B_WIDTH
F_ROWS = 64
NT_ROWS = F_ROW + F_ROWS
VT_ROWS = HEAD_DIM + 16

T_ATT = 512
TM_FFN = 512
FF_SPLIT = 2

VMEM_LIMIT = 56 * 1024 * 1024

_F32 = jnp.float32
_BF16 = jnp.bfloat16


def _dot(a, b):
    return jnp.dot(a, b, preferred_element_type=_F32)


def _dot_nt(a, b):
    return lax.dot_general(a, b, (((1,), (1,)), ((), ())), preferred_element_type=_F32)


def _rms(x, g):
    return x * lax.rsqrt(jnp.mean(x * x, axis=-1, keepdims=True) + EPS) * g


def _gelu_tanh(x):
    return 0.5 * x * (1.0 + jnp.tanh(0.7978845608028654 * (x + 0.044715 * (x * x * x))))


def _log_sigmoid(x):
    return -(jnp.maximum(-x, 0.0) + jnp.log1p(jnp.exp(-jnp.abs(x))))


def _split3(x):
    hi = x.astype(_BF16).astype(_F32)
    r = x - hi
    mid = r.astype(_BF16).astype(_F32)
    lo = (r - mid).astype(_BF16).astype(_F32)
    return hi, mid, lo


def _mem_kv_kernel(mem_ref, g_ref, w_ref, km_ref, vm_ref):
    mn = _rms(mem_ref[0], g_ref[...]).astype(_BF16)
    kv = _dot(mn, w_ref[...])
    km = kv[:, :M_WIDTH]
    vm = kv[:, M_WIDTH:]
    head = lax.broadcasted_iota(jnp.int32, (MEM_TOKENS, M_WIDTH), 1) // HEAD_DIM
    for h in range(M_HEADS):
        km_ref[0, h] = jnp.where(head == h, km, 0.0).astype(_BF16)
        vm_ref[0, h] = jnp.where(head == h, vm, 0.0).astype(_BF16)


def _mem_kv(mem, g_mem, w_kv):
    batch = mem.shape[0]
    out = jax.ShapeDtypeStruct((batch, M_HEADS, MEM_TOKENS, M_WIDTH), _BF16)
    return pl.pallas_call(
        _mem_kv_kernel,
        grid=(batch,),
        in_specs=[
            pl.BlockSpec((1, MEM_TOKENS, D_MODEL), lambda b: (b, 0, 0)),
            pl.BlockSpec((1, D_MODEL), lambda b: (0, 0)),
            pl.BlockSpec((D_MODEL, 2 * M_WIDTH), lambda b: (0, 0)),
        ],
        out_specs=[
            pl.BlockSpec((1, M_HEADS, MEM_TOKENS, M_WIDTH), lambda b: (b, 0, 0, 0)),
            pl.BlockSpec((1, M_HEADS, MEM_TOKENS, M_WIDTH), lambda b: (b, 0, 0, 0)),
        ],
        out_shape=[out, out],
        compiler_params=pltpu.CompilerParams(dimension_semantics=("parallel",)),
        name="mem_kv",
    )(mem, g_mem, w_kv)


def _mix_in_kernel(x_ref, gpre_ref, wnn_ref, wt_ref, bf_ref, gsgu_ref, ws_ref, bs_ref, goa_ref, gom_ref,
                   km_ref, vm_ref, qt_ref, ka_ref, vt_ref, ya_ref, ym_ref, carry_ref):
    tm = x_ref.shape[1]
    h = _rms(x_ref[0], gpre_ref[...]).astype(_BF16)

    u = _gelu_tanh(_dot(h, wnn_ref[:, 0:A_WIDTH]))
    vn = _rms(_gelu_tanh(_dot(h, wnn_ref[:, A_WIDTH:2 * A_WIDTH])), gsgu_ref[...])
    lane = lax.broadcasted_iota(jnp.int32, (CHUNK, LANES), 1)
    row_s = lax.broadcasted_iota(jnp.int32, (CHUNK, 2 * CHUNK), 0)
    col_s = lax.broadcasted_iota(jnp.int32, (CHUNK, 2 * CHUNK), 1) % CHUNK
    z_pairs = []
    for p in range(PAIRS):
        w_pair = jnp.where(col_s <= row_s, ws_ref[p], 0.0).astype(_BF16)
        z_chunks = []
        for c in range(tm // CHUNK):
            v_pair = vn[c * CHUNK:(c + 1) * CHUNK, p * LANES:(p + 1) * LANES]
            rhs = jnp.concatenate([jnp.where(lane < HEAD_DIM, v_pair, 0.0),
                                   jnp.where(lane >= HEAD_DIM, v_pair, 0.0)], axis=0).astype(_BF16)
            z_chunks.append(_dot(w_pair, rhs) + bs_ref[:, p * LANES:(p + 1) * LANES])
        z_pairs.append(jnp.concatenate(z_chunks, axis=0))
    y_a = u * jnp.concatenate(z_pairs, axis=1)
    ya_ref[0] = _rms(y_a, goa_ref[...]).astype(_BF16)

    qm = (_dot(h, wnn_ref[:, QM_OFF:NN_COLS]) * SCALE).astype(_BF16)
    o_m = jnp.zeros((tm, M_WIDTH), _F32)
    for hh in range(M_HEADS):
        s = _dot_nt(qm, km_ref[0, hh])
        e = jnp.exp(s - jnp.max(s, axis=-1, keepdims=True))
        p_m = (e * (1.0 / jnp.sum(e, axis=-1, keepdims=True))).astype(_BF16)
        o_m = o_m + _dot(p_m, vm_ref[0, hh])
    ym_ref[0] = _rms(o_m, gom_ref[...]).astype(_BF16)

    @pl.when(pl.program_id(1) == 0)
    def _():
        carry_ref[...] = jnp.zeros_like(carry_ref)

    rowf = lax.broadcasted_iota(jnp.int32, (F_ROWS, tm), 0)
    logf = _log_sigmoid(_dot_nt(wt_ref[F_ROW:NT_ROWS, :], h) + bf_ref[...])
    logf = jnp.where(rowf < SUBLANES * B_HEADS, logf, 0.0)
    parts = jnp.concatenate(_split3(logf), axis=0).astype(_BF16)
    triu = (lax.broadcasted_iota(jnp.int32, (tm, tm), 0)
            <= lax.broadcasted_iota(jnp.int32, (tm, tm), 1)).astype(_BF16)
    cs = _dot(parts, triu)
    c = cs[0:F_ROWS] + cs[F_ROWS:2 * F_ROWS] + cs[2 * F_ROWS:] + carry_ref[...]
    carry_ref[...] = c[:, tm - 1:tm]
    c_hi, c_mid, c_lo = _split3(c)

    j8 = rowf % SUBLANES
    cq = jnp.where(j8 == 0, c_hi, jnp.where(j8 == 1, c_mid, jnp.where(j8 == 2, c_lo,
                                                                        jnp.where(j8 < 6, 1.0, 0.0))))
    ck = jnp.where(j8 < 3, 1.0, jnp.where(j8 == 3, -c_hi, jnp.where(j8 == 4, -c_mid,
                                                                       jnp.where(j8 == 5, -c_lo, 0.0))))
    ck_t = jnp.concatenate([ck, jnp.zeros((LANES - F_ROWS, tm), _F32)], axis=0).T
    q_t = _dot_nt(wt_ref[0:B_WIDTH, :], h) * SCALE
    v_t = _dot_nt(wt_ref[V_ROW:V_ROW + B_WIDTH, :], h)
    k = _dot(h, wnn_ref[:, K_OFF:K_OFF + B_WIDTH])
    lane_t = lax.broadcasted_iota(jnp.int32, (tm, LANES), 1)
    zeros_q = jnp.zeros((LANES - HEAD_DIM - SUBLANES, tm), _F32)
    ones_v = jnp.where(lax.broadcasted_iota(jnp.int32, (VT_ROWS - HEAD_DIM, tm), 0) == 0, 1.0, 0.0)
    for hd in range(B_HEADS):
        p, half = hd // 2, hd % 2
        k_pair = k[:, p * LANES:(p + 1) * LANES]
        q_h = q_t[hd * HEAD_DIM:(hd + 1) * HEAD_DIM]
        cq_h = cq[SUBLANES * hd:SUBLANES * (hd + 1)]
        if half == 0:
            k_aug = jnp.where(lane_t < HEAD_DIM, k_pair, pltpu.roll(ck_t, HEAD_DIM - SUBLANES * hd, 1))
            q_aug = jnp.concatenate([q_h, cq_h, zeros_q], axis=0)
        else:
            k_aug = jnp.where(lane_t >= HEAD_DIM, k_pair, pltpu.roll(ck_t, LANES - SUBLANES * hd, 1))
            q_aug = jnp.concatenate([cq_h, zeros_q, q_h], axis=0)
        qt_ref[0, hd, 0] = q_aug.astype(_BF16)
        ka_ref[0, hd] = k_aug.astype(_BF16)
        vt_ref[0, hd, 0] = jnp.concatenate([v_t[hd * HEAD_DIM:(hd + 1) * HEAD_DIM], ones_v], axis=0).astype(_BF16)


def _mix_in(x, g_pre, w_nn, w_t, b_f_col, g_sgu, ws_cat, bs_exp, g_out_a, g_out_m, km, vm):
    batch, seq, _ = x.shape
    tm = T_ATT
    nblk = seq // tm
    const2 = lambda b, i: (0, 0)
    mem_spec = pl.BlockSpec((1, M_HEADS, MEM_TOKENS, M_WIDTH), lambda b, i: (b, 0, 0, 0))
    return pl.pallas_call(
        _mix_in_kernel,
        grid=(batch, nblk),
        in_specs=[
            pl.BlockSpec((1, tm, D_MODEL), lambda b, i: (b, i, 0)),
            pl.BlockSpec((1, D_MODEL), const2),
            pl.BlockSpec((D_MODEL, NN_COLS), const2),
            pl.BlockSpec((NT_ROWS, D_MODEL), const2),
            pl.BlockSpec((F_ROWS, 1), const2),
            pl.BlockSpec((1, A_WIDTH), const2),
            pl.BlockSpec((PAIRS, CHUNK, 2 * CHUNK), lambda b, i: (0, 0, 0)),
            pl.BlockSpec((CHUNK, A_WIDTH), const2),
            pl.BlockSpec((1, A_WIDTH), const2),
            pl.BlockSpec((1, M_WIDTH), const2),
            mem_spec, mem_spec,
        ],
        out_specs=[
            pl.BlockSpec((1, B_HEADS, 1, LANES, tm), lambda b, i: (b, 0, i, 0, 0)),
            pl.BlockSpec((1, B_HEADS, tm, LANES), lambda b, i: (b, 0, i, 0)),
            pl.BlockSpec((1, B_HEADS, 1, VT_ROWS, tm), lambda b, i: (b, 0, i, 0, 0)),
            pl.BlockSpec((1, tm, A_WIDTH), lambda b, i: (b, i, 0)),
            pl.BlockSpec((1, tm, M_WIDTH), lambda b, i: (b, i, 0)),
        ],
        out_shape=[jax.ShapeDtypeStruct((batch, B_HEADS, nblk, LANES, tm), _BF16),
                   jax.ShapeDtypeStruct((batch, B_HEADS, seq, LANES), _BF16),
                   jax.ShapeDtypeStruct((batch, B_HEADS, nblk, VT_ROWS, tm), _BF16),
                   jax.ShapeDtypeStruct((batch, seq, A_WIDTH), _BF16),
                   jax.ShapeDtypeStruct((batch, seq, M_WIDTH), _BF16)],
        scratch_shapes=[pltpu.VMEM((F_ROWS, 1), _F32)],
        compiler_params=pltpu.CompilerParams(dimension_semantics=("parallel", "arbitrary"),
                                             vmem_limit_bytes=VMEM_LIMIT),
        name="mix_in",
    )(x, g_pre, w_nn, w_t, b_f_col, g_sgu, ws_cat, bs_exp, g_out_a, g_out_m, km, vm)


def _fox_out_kernel(qt_ref, ka_ref, vt_ref, ya_ref, ym_ref, x_ref, wout_ref, gob_ref, gpost_ref, x1_ref):
    t = T_ATT
    i = pl.program_id(1)
    causal = (lax.broadcasted_iota(jnp.int32, (t, t), 0) <= lax.broadcasted_iota(jnp.int32, (t, t), 1))

    def attend(j, carry, masked):
        start = pl.multiple_of(j * t, t)
        out = []
        for hd in range(B_HEADS):
            m, acc = carry[hd]
            s = _dot(ka_ref[0, hd, pl.ds(start, t), :], qt_ref[0, hd, 0])
            if masked:
                s = jnp.where(causal, s, NEG_INF)
            m_new = jnp.maximum(m, jnp.max(s, axis=0, keepdims=True))
            p = jnp.exp(s - m_new).astype(_BF16)
            acc = jnp.exp(m - m_new) * acc + _dot(vt_ref[0, hd, j], p)
            out.append((m_new, acc))
        return tuple(out)

    init = tuple((jnp.full((1, t), NEG_INF, _F32), jnp.zeros((VT_ROWS, t), _F32)) for _ in range(B_HEADS))
    state = lax.fori_loop(0, i, lambda j, c: attend(j, c, False), init)
    state = attend(i, state, True)
    heads = []
    for hd in range(B_HEADS):
        acc = state[hd][1]
        heads.append(acc[0:HEAD_DIM] * (1.0 / acc[HEAD_DIM:HEAD_DIM + 1]))
    yb_t = jnp.concatenate(heads, axis=0)
    yb_t = yb_t * lax.rsqrt(jnp.mean(yb_t * yb_t, axis=0, keepdims=True) + EPS) * gob_ref[...]
    yb_n = yb_t.T.astype(_BF16)

    z = (_dot(ya_ref[0], wout_ref[0:A_WIDTH, :])
         + _dot(yb_n, wout_ref[A_WIDTH:A_WIDTH + B_WIDTH, :])
         + _dot(ym_ref[0], wout_ref[A_WIDTH + B_WIDTH:, :]))
    x1_ref[0] = x_ref[0] + _rms(z, gpost_ref[...])


def _fox_out(qt, ka, vt, ya, ym, x, w_out, g_out_b_col, g_post):
    batch, seq, _ = x.shape
    t = T_ATT
    nblk = seq // t
    const2 = lambda b, i: (0, 0)
    return pl.pallas_call(
        _fox_out_kernel,
        grid=(batch, nblk),
        in_specs=[
            pl.BlockSpec((1, B_HEADS, 1, LANES, t), lambda b, i: (b, 0, i, 0, 0)),
            pl.BlockSpec((1, B_HEADS, seq, LANES), lambda b, i: (b, 0, 0, 0)),
            pl.BlockSpec((1, B_HEADS, nblk, VT_ROWS, t), lambda b, i: (b, 0, 0, 0, 0)),
            pl.BlockSpec((1, t, A_WIDTH), lambda b, i: (b, i, 0)),
            pl.BlockSpec((1, t, M_WIDTH), lambda b, i: (b, i, 0)),
            pl.BlockSpec((1, t, D_MODEL), lambda b, i: (b, i, 0)),
            pl.BlockSpec((D_MODEL, D_MODEL), const2),
            pl.BlockSpec((B_WIDTH, 1), const2),
            pl.BlockSpec((1, D_MODEL), const2),
        ],
        out_specs=pl.BlockSpec((1, t, D_MODEL), lambda b, i: (b, i, 0)),
        out_shape=jax.ShapeDtypeStruct(x.shape, x.dtype),
        compiler_params=pltpu.CompilerParams(dimension_semantics=("parallel", "parallel"),
                                             vmem_limit_bytes=VMEM_LIMIT),
        name="fox_out",
    )(qt, ka, vt, ya, ym, x, w_out, g_out_b_col, g_post)


def _ffn_kernel(x_ref, gpre_ref, wg_ref, wu_ref, wd_ref, gpost_ref, o_ref):
    x = x_ref[...]
    h = _rms(x, gpre_ref[...]).astype(_BF16)
    step = D_FF // FF_SPLIT
    ff = None
    for c in range(FF_SPLIT):
        sl = slice(c * step, (c + 1) * step)
        g = _dot(h, wg_ref[:, sl])
        a = (g * (1.0 / (1.0 + jnp.exp(-g))) * _dot(h, wu_ref[:, sl])).astype(_BF16)
        part = _dot(a, wd_ref[sl, :])
        ff = part if ff is None else ff + part
    o_ref[...] = x + _rms(ff, gpost_ref[...])


def _ffn(x, g_pre, w_gate, w_up, w_down, g_post):
    tokens = x.shape[0]
    tm = TM_FFN
    const = lambda i: (0, 0)
    resident = functools.partial(pl.BlockSpec, index_map=const, pipeline_mode=pl.Buffered(1))
    return pl.pallas_call(
        _ffn_kernel,
        grid=(tokens // tm,),
        in_specs=[
            pl.BlockSpec((tm, D_MODEL), lambda i: (i, 0)),
            pl.BlockSpec((1, D_MODEL), const),
            resident((D_MODEL, D_FF)),
            resident((D_MODEL, D_FF)),
            resident((D_FF, D_MODEL)),
            pl.BlockSpec((1, D_MODEL), const),
        ],
        out_specs=pl.BlockSpec((tm, D_MODEL), lambda i: (i, 0)),
        out_shape=jax.ShapeDtypeStruct(x.shape, x.dtype),
        compiler_params=pltpu.CompilerParams(dimension_semantics=("parallel",),
                                             vmem_limit_bytes=VMEM_LIMIT),
        name="ffn",
    )(x, g_pre, w_gate, w_up, w_down, g_post)


def kernel(x, mem, g_pre_mix, w_in, b_f, g_sgu, w_s, b_s, g_out_a, g_out_b, g_out_m, g_mem, w_mem_kv, w_out,
           g_post_mix, g_pre_ffn, w_gate, w_up, w_down, g_post_ffn):
    batch, seq, d = x.shape
    depth = w_in.shape[0]
    row = lambda a: a.reshape(1, -1)
    q_lo = 2 * A_WIDTH
    f_lo = q_lo + 3 * B_WIDTH
    for l in range(depth):
        w = w_in[l]
        w_nn = jnp.concatenate([w[:, :q_lo], w[:, q_lo + B_WIDTH:q_lo + 2 * B_WIDTH], w[:, f_lo + B_HEADS:]],
                               axis=1).astype(_BF16)
        w_f = jnp.pad(jnp.repeat(w[:, f_lo:f_lo + B_HEADS], SUBLANES, axis=1),
                      ((0, 0), (0, F_ROWS - SUBLANES * B_HEADS)))
        w_t = jnp.concatenate([w[:, q_lo:q_lo + B_WIDTH], w[:, q_lo + 2 * B_WIDTH:f_lo], w_f],
                              axis=1).T.astype(_BF16)
        b_f_col = jnp.pad(jnp.repeat(b_f[l], SUBLANES), (0, F_ROWS - SUBLANES * B_HEADS)).reshape(F_ROWS, 1)
        ws_cat = w_s[l].reshape(PAIRS, 2, CHUNK, CHUNK).transpose(0, 2, 1, 3).reshape(PAIRS, CHUNK, 2 * CHUNK)
        bs_exp = jnp.repeat(b_s[l].T, HEAD_DIM, axis=1)

        km, vm = _mem_kv(mem, row(g_mem[l]), w_mem_kv[l].astype(_BF16))
        qt, ka, vt, ya, ym = _mix_in(x, row(g_pre_mix[l]), w_nn, w_t, b_f_col, row(g_sgu[l]), ws_cat, bs_exp,
                                     row(g_out_a[l]), row(g_out_m[l]), km, vm)
        x = _fox_out(qt, ka, vt, ya, ym, x, w_out[l].astype(_BF16), g_out_b[l].reshape(-1, 1),
                     row(g_post_mix[l]))
        x = _ffn(x.reshape(batch * seq, d), row(g_pre_ffn[l]), w_gate[l].astype(_BF16), w_up[l].astype(_BF16),
                 w_down[l].astype(_BF16), row(g_post_ffn[l])).reshape(batch, seq, d)
    return x
```

```python
import functools

import jax
import jax.numpy as jnp
from jax import lax
from jax.experimental import pallas as pl
from jax.experimental.pallas import tpu as pltpu

D_MODEL = 1024
HEAD_DIM = 64
A_GROUPS = 6
B_HEADS = 6
M_HEADS = 4
A_WIDTH = A_GROUPS * HEAD_DIM
B_WIDTH = B_HEADS * HEAD_DIM
M_WIDTH = M_HEADS * HEAD_DIM
CHUNK = 128
MEM_TOKENS = 256
D_FF = 2816
EPS = 1e-6
NEG_INF = -1e30

LANES = 128
SUBLANES = 8
PAIRS = B_HEADS // 2
SCALE = HEAD_DIM ** -0.5
LOG2E = 1.4426950408889634

K_OFF = 2 * A_WIDTH
QM_OFF = K_OFF + B_WIDTH
NN_COLS = QM_OFF + M_WIDTH
V_ROW = B_WIDTH
F_ROW = 2 * B_WIDTH
F_ROWS = 64
NT_ROWS = F_ROW + F_ROWS
VT_ROWS = HEAD_DIM + 16

T_ATT = 512
TM_FFN = 512
FF_SPLIT = 2
QK_AHEAD = 2

VMEM_LIMIT = 56 * 1024 * 1024

_F32 = jnp.float32
_BF16 = jnp.bfloat16


def _dot(a, b):
    return jnp.dot(a, b, preferred_element_type=_F32)


def _dot_nt(a, b):
    return lax.dot_general(a, b, (((1,), (1,)), ((), ())), preferred_element_type=_F32)


def _rms(x, g):
    return x * lax.rsqrt(jnp.mean(x * x, axis=-1, keepdims=True) + EPS) * g


def _gelu_tanh(x):
    return 0.5 * x * (1.0 + jnp.tanh(0.7978845608028654 * (x + 0.044715 * (x * x * x))))


def _log_sigmoid(x):
    return -(jnp.maximum(-x, 0.0) + jnp.log1p(jnp.exp(-jnp.abs(x))))


def _split3(x):
    hi = x.astype(_BF16).astype(_F32)
    r = x - hi
    mid = r.astype(_BF16).astype(_F32)
    lo = (r - mid).astype(_BF16).astype(_F32)
    return hi, mid, lo


def _mem_kv_kernel(mem_ref, g_ref, w_ref, km_ref, vm_ref):
    mn = _rms(mem_ref[0], g_ref[...]).astype(_BF16)
    kv = _dot(mn, w_ref[...])
    km = kv[:, :M_WIDTH]
    vm = kv[:, M_WIDTH:]
    head = lax.broadcasted_iota(jnp.int32, (MEM_TOKENS, M_WIDTH), 1) // HEAD_DIM
    for h in range(M_HEADS):
        km_ref[0, h] = jnp.where(head == h, km, 0.0).astype(_BF16)
        vm_ref[0, h] = jnp.where(head == h, vm, 0.0).astype(_BF16)


def _mem_kv(mem, g_mem, w_kv):
    batch = mem.shape[0]
    out = jax.ShapeDtypeStruct((batch, M_HEADS, MEM_TOKENS, M_WIDTH), _BF16)
    return pl.pallas_call(
        _mem_kv_kernel,
        grid=(batch,),
        in_specs=[
            pl.BlockSpec((1, MEM_TOKENS, D_MODEL), lambda b: (b, 0, 0)),
            pl.BlockSpec((1, D_MODEL), lambda b: (0, 0)),
            pl.BlockSpec((D_MODEL, 2 * M_WIDTH), lambda b: (0, 0)),
        ],
        out_specs=[
            pl.BlockSpec((1, M_HEADS, MEM_TOKENS, M_WIDTH), lambda b: (b, 0, 0, 0)),
            pl.BlockSpec((1, M_HEADS, MEM_TOKENS, M_WIDTH), lambda b: (b, 0, 0, 0)),
        ],
        out_shape=[out, out],
        compiler_params=pltpu.CompilerParams(dimension_semantics=("parallel",)),
        name="mem_kv",
    )(mem, g_mem, w_kv)


def _mix_in_kernel(x_ref, gpre_ref, wnn_ref, wt_ref, bf_ref, gsgu_ref, ws_ref, bs_ref, goa_ref, gom_ref,
                   km_ref, vm_ref, qt_ref, ka_ref, vt_ref, ya_ref, ym_ref, carry_ref):
    tm = x_ref.shape[1]
    h = _rms(x_ref[0], gpre_ref[...]).astype(_BF16)

    u = _gelu_tanh(_dot(h, wnn_ref[:, 0:A_WIDTH]))
    vn = _rms(_gelu_tanh(_dot(h, wnn_ref[:, A_WIDTH:2 * A_WIDTH])), gsgu_ref[...])
    lane = lax.broadcasted_iota(jnp.int32, (CHUNK, LANES), 1)
    row_s = lax.broadcasted_iota(jnp.int32, (CHUNK, 2 * CHUNK), 0)
    col_s = lax.broadcasted_iota(jnp.int32, (CHUNK, 2 * CHUNK), 1) % CHUNK
    z_pairs = []
    for p in range(PAIRS):
        w_pair = jnp.where(col_s <= row_s, ws_ref[p], 0.0).astype(_BF16)
        z_chunks = []
        for c in range(tm // CHUNK):
            v_pair = vn[c * CHUNK:(c + 1) * CHUNK, p * LANES:(p + 1) * LANES]
            rhs = jnp.concatenate([jnp.where(lane < HEAD_DIM, v_pair, 0.0),
                                   jnp.where(lane >= HEAD_DIM, v_pair, 0.0)], axis=0).astype(_BF16)
            z_chunks.append(_dot(w_pair, rhs) + bs_ref[:, p * LANES:(p + 1) * LANES])
        z_pairs.append(jnp.concatenate(z_chunks, axis=0))
    y_a = u * jnp.concatenate(z_pairs, axis=1)
    ya_ref[0] = _rms(y_a, goa_ref[...]).astype(_BF16)

    qm = (_dot(h, wnn_ref[:, QM_OFF:NN_COLS]) * SCALE).astype(_BF16)
    o_m = jnp.zeros((tm, M_WIDTH), _F32)
    for hh in range(M_HEADS):
        s = _dot_nt(qm, km_ref[0, hh])
        e = jnp.exp(s - jnp.max(s, axis=-1, keepdims=True))
        p_m = (e * (1.0 / jnp.sum(e, axis=-1, keepdims=True))).astype(_BF16)
        o_m = o_m + _dot(p_m, vm_ref[0, hh])
    ym_ref[0] = _rms(o_m, gom_ref[...]).astype(_BF16)

    @pl.when(pl.program_id(1) == 0)
    def _():
        carry_ref[...] = jnp.zeros_like(carry_ref)

    rowf = lax.broadcasted_iota(jnp.int32, (F_ROWS, tm), 0)
    logf = _log_sigmoid(_dot_nt(wt_ref[F_ROW:NT_ROWS, :], h) + bf_ref[...])
    logf = jnp.where(rowf < SUBLANES * B_HEADS, logf, 0.0)
    parts = jnp.concatenate(_split3(logf), axis=0).astype(_BF16)
    triu = (lax.broadcasted_iota(jnp.int32, (tm, tm), 0)
            <= lax.broadcasted_iota(jnp.int32, (tm, tm), 1)).astype(_BF16)
    cs = _dot(parts, triu)
    c = cs[0:F_ROWS] + cs[F_ROWS:2 * F_ROWS] + cs[2 * F_ROWS:] + carry_ref[...]
    carry_ref[...] = c[:, tm - 1:tm]
    c_hi, c_mid, c_lo = _split3(c * LOG2E)

    j8 = rowf % SUBLANES
    cq = jnp.where(j8 == 0, c_hi, jnp.where(j8 == 1, c_mid, jnp.where(j8 == 2, c_lo,
                                                                        jnp.where(j8 < 6, 1.0, 0.0))))
    ck = jnp.where(j8 < 3, 1.0, jnp.where(j8 == 3, -c_hi, jnp.where(j8 == 4, -c_mid,
                                                                       jnp.where(j8 == 5, -c_lo, 0.0))))
    ck_t = jnp.concatenate([ck, jnp.zeros((LANES - F_ROWS, tm), _F32)], axis=0).T
    q_t = _dot_nt(wt_ref[0:B_WIDTH, :], h) * (SCALE * LOG2E)
    v_t = _dot_nt(wt_ref[V_ROW:V_ROW + B_WIDTH, :], h)
    k = _dot(h, wnn_ref[:, K_OFF:K_OFF + B_WIDTH])
    lane_t = lax.broadcasted_iota(jnp.int32, (tm, LANES), 1)
    zeros_q = jnp.zeros((LANES - HEAD_DIM - SUBLANES, tm), _F32)
    ones_v = jnp.where(lax.broadcasted_iota(jnp.int32, (VT_ROWS - HEAD_DIM, tm), 0) == 0, 1.0, 0.0)
    for hd in range(B_HEADS):
        p, half = hd // 2, hd % 2
        k_pair = k[:, p * LANES:(p + 1) * LANES]
        q_h = q_t[hd * HEAD_DIM:(hd + 1) * HEAD_DIM]
        cq_h = cq[SUBLANES * hd:SUBLANES * (hd + 1)]
        if half == 0:
            k_aug = jnp.where(lane_t < HEAD_DIM, k_pair, pltpu.roll(ck_t, HEAD_DIM - SUBLANES * hd, 1))
            q_aug = jnp.concatenate([q_h, cq_h, zeros_q], axis=0)
        else:
            k_aug = jnp.where(lane_t >= HEAD_DIM, k_pair, pltpu.roll(ck_t, LANES - SUBLANES * hd, 1))
            q_aug = jnp.concatenate([cq_h, zeros_q, q_h], axis=0)
        qt_ref[0, hd, 0] = q_aug.astype(_BF16)
        ka_ref[0, hd] = k_aug.astype(_BF16)
        vt_ref[0, hd, 0] = jnp.concatenate([v_t[hd * HEAD_DIM:(hd + 1) * HEAD_DIM], ones_v], axis=0).astype(_BF16)


def _mix_in(x, g_pre, w_nn, w_t, b_f_col, g_sgu, ws_cat, bs_exp, g_out_a, g_out_m, km, vm):
    batch, seq, _ = x.shape
    tm = T_ATT
    nblk = seq // tm
    const2 = lambda b, i: (0, 0)
    mem_spec = pl.BlockSpec((1, M_HEADS, MEM_TOKENS, M_WIDTH), lambda b, i: (b, 0, 0, 0))
    return pl.pallas_call(
        _mix_in_kernel,
        grid=(batch, nblk),
        in_specs=[
            pl.BlockSpec((1, tm, D_MODEL), lambda b, i: (b, i, 0)),
            pl.BlockSpec((1, D_MODEL), const2),
            pl.BlockSpec((D_MODEL, NN_COLS), const2),
            pl.BlockSpec((NT_ROWS, D_MODEL), const2),
            pl.BlockSpec((F_ROWS, 1), const2),
            pl.BlockSpec((1, A_WIDTH), const2),
            pl.BlockSpec((PAIRS, CHUNK, 2 * CHUNK), lambda b, i: (0, 0, 0)),
            pl.BlockSpec((CHUNK, A_WIDTH), const2),
            pl.BlockSpec((1, A_WIDTH), const2),
            pl.BlockSpec((1, M_WIDTH), const2),
            mem_spec, mem_spec,
        ],
        out_specs=[
            pl.BlockSpec((1, B_HEADS, 1, LANES, tm), lambda b, i: (b, 0, i, 0, 0)),
            pl.BlockSpec((1, B_HEADS, tm, LANES), lambda b, i: (b, 0, i, 0)),
            pl.BlockSpec((1, B_HEADS, 1, VT_ROWS, tm), lambda b, i: (b, 0, i, 0, 0)),
            pl.BlockSpec((1, tm, A_WIDTH), lambda b, i: (b, i, 0)),
            pl.BlockSpec((1, tm, M_WIDTH), lambda b, i: (b, i, 0)),
        ],
        out_shape=[jax.ShapeDtypeStruct((batch, B_HEADS, nblk, LANES, tm), _BF16),
                   jax.ShapeDtypeStruct((batch, B_HEADS, seq, LANES), _BF16),
                   jax.ShapeDtypeStruct((batch, B_HEADS, nblk, VT_ROWS, tm), _BF16),
                   jax.ShapeDtypeStruct((batch, seq, A_WIDTH), _BF16),
                   jax.ShapeDtypeStruct((batch, seq, M_WIDTH), _BF16)],
        scratch_shapes=[pltpu.VMEM((F_ROWS, 1), _F32)],
        compiler_params=pltpu.CompilerParams(dimension_semantics=("parallel", "arbitrary"),
                                             vmem_limit_bytes=VMEM_LIMIT),
        name="mix_in",
    )(x, g_pre, w_nn, w_t, b_f_col, g_sgu, ws_cat, bs_exp, g_out_a, g_out_m, km, vm)


def _fox_out_kernel(qt_ref, ka_ref, vt_ref, ya_ref, ym_ref, x_ref, wout_ref, gob_ref, gpost_ref, x1_ref):
    t = T_ATT
    i = pl.program_id(1)
    causal = (lax.broadcasted_iota(jnp.int32, (t, t), 0) <= lax.broadcasted_iota(jnp.int32, (t, t), 1))

    def attend(j, carry, masked):
        start = pl.multiple_of(j * t, t)
        out = []
        qk = lambda hd: _dot(ka_ref[0, hd, pl.ds(start, t), :], qt_ref[0, hd, 0])
        scores = [qk(hd) if hd < QK_AHEAD else None for hd in range(B_HEADS)]
        for hd in range(B_HEADS):
            m, acc = carry[hd]
            s = scores[hd]
            if masked:
                s = jnp.where(causal, s, NEG_INF)
            m_new = jnp.maximum(m, jnp.max(s, axis=0, keepdims=True))
            p = jnp.exp2(s - m_new).astype(_BF16)
            if hd + QK_AHEAD < B_HEADS:
                scores[hd + QK_AHEAD] = qk(hd + QK_AHEAD)
            acc = jnp.exp2(m - m_new) * acc + _dot(vt_ref[0, hd, j], p)
            out.append((m_new, acc))
        return tuple(out)

    init = tuple((jnp.full((1, t), NEG_INF, _F32), jnp.zeros((VT_ROWS, t), _F32)) for _ in range(B_HEADS))
    state = lax.fori_loop(0, i, lambda j, c: attend(j, c, False), init)
    state = attend(i, state, True)
    heads = []
    for hd in range(B_HEADS):
        acc = state[hd][1]
        heads.append(acc[0:HEAD_DIM] * (1.0 / acc[HEAD_DIM:HEAD_DIM + 1]))
    yb_t = jnp.concatenate(heads, axis=0)
    yb_t = yb_t * lax.rsqrt(jnp.mean(yb_t * yb_t, axis=0, keepdims=True) + EPS) * gob_ref[...]
    yb_n = yb_t.T.astype(_BF16)

    z = (_dot(ya_ref[0], wout_ref[0:A_WIDTH, :])
         + _dot(yb_n, wout_ref[A_WIDTH:A_WIDTH + B_WIDTH, :])
         + _dot(ym_ref[0], wout_ref[A_WIDTH + B_WIDTH:, :]))
    x1_ref[0] = x_ref[0] + _rms(z, gpost_ref[...])


def _fox_out(qt, ka, vt, ya, ym, x, w_out, g_out_b_col, g_post):
    batch, seq, _ = x.shape
    t = T_ATT
    nblk = seq // t
    const2 = lambda b, i: (0, 0)
    return pl.pallas_call(
        _fox_out_kernel,
        grid=(batch, nblk),
        in_specs=[
            pl.BlockSpec((1, B_HEADS, 1, LANES, t), lambda b, i: (b, 0, i, 0, 0)),
            pl.BlockSpec((1, B_HEADS, seq, LANES), lambda b, i: (b, 0, 0, 0)),
            pl.BlockSpec((1, B_HEADS, nblk, VT_ROWS, t), lambda b, i: (b, 0, 0, 0, 0)),
            pl.BlockSpec((1, t, A_WIDTH), lambda b, i: (b, i, 0)),
            pl.BlockSpec((1, t, M_WIDTH), lambda b, i: (b, i, 0)),
            pl.BlockSpec((1, t, D_MODEL), lambda b, i: (b, i, 0)),
            pl.BlockSpec((D_MODEL, D_MODEL), const2),
            pl.BlockSpec((B_WIDTH, 1), const2),
            pl.BlockSpec((1, D_MODEL), const2),
        ],
        out_specs=pl.BlockSpec((1, t, D_MODEL), lambda b, i: (b, i, 0)),
        out_shape=jax.ShapeDtypeStruct(x.shape, x.dtype),
        compiler_params=pltpu.CompilerParams(dimension_semantics=("parallel", "parallel"),
                                             vmem_limit_bytes=VMEM_LIMIT),
        name="fox_out",
    )(qt, ka, vt, ya, ym, x, w_out, g_out_b_col, g_post)


def _ffn_kernel(x_ref, gpre_ref, wg_ref, wu_ref, wd_ref, gpost_ref, o_ref):
    x = x_ref[...]
    h = _rms(x, gpre_ref[...]).astype(_BF16)
    step = D_FF // FF_SPLIT
    ff = None
    for c in range(FF_SPLIT):
        sl = slice(c * step, (c + 1) * step)
        g = _dot(h, wg_ref[:, sl])
        a = (g * (1.0 / (1.0 + jnp.exp(-g))) * _dot(h, wu_ref[:, sl])).astype(_BF16)
        part = _dot(a, wd_ref[sl, :])
        ff = part if ff is None else ff + part
    o_ref[...] = x + _rms(ff, gpost_ref[...])


def _ffn(x, g_pre, w_gate, w_up, w_down, g_post):
    tokens = x.shape[0]
    tm = TM_FFN
    const = lambda i: (0, 0)
    resident = functools.partial(pl.BlockSpec, index_map=const, pipeline_mode=pl.Buffered(1))
    return pl.pallas_call(
        _ffn_kernel,
        grid=(tokens // tm,),
        in_specs=[
            pl.BlockSpec((tm, D_MODEL), lambda i: (i, 0)),
            pl.BlockSpec((1, D_MODEL), const),
            resident((D_MODEL, D_FF)),
            resident((D_MODEL, D_FF)),
            resident((D_FF, D_MODEL)),
            pl.BlockSpec((1, D_MODEL), const),
        ],
        out_specs=pl.BlockSpec((tm, D_MODEL), lambda i: (i, 0)),
        out_shape=jax.ShapeDtypeStruct(x.shape, x.dtype),
        compiler_params=pltpu.CompilerParams(dimension_semantics=("parallel",),
                                             vmem_limit_bytes=VMEM_LIMIT),
        name="ffn",
    )(x, g_pre, w_gate, w_up, w_down, g_post)


def kernel(x, mem, g_pre_mix, w_in, b_f, g_sgu, w_s, b_s, g_out_a, g_out_b, g_out_m, g_mem, w_mem_kv, w_out,
           g_post_mix, g_pre_ffn, w_gate, w_up, w_down, g_post_ffn):
    batch, seq, d = x.shape
    depth = w_in.shape[0]
    row = lambda a: a.reshape(1, -1)
    q_lo = 2 * A_WIDTH
    f_lo = q_lo + 3 * B_WIDTH
    for l in range(depth):
        w = w_in[l]
        w_nn = jnp.concatenate([w[:, :q_lo], w[:, q_lo + B_WIDTH:q_lo + 2 * B_WIDTH], w[:, f_lo + B_HEADS:]],
                               axis=1).astype(_BF16)
        w_f = jnp.pad(jnp.repeat(w[:, f_lo:f_lo + B_HEADS], SUBLANES, axis=1),
                      ((0, 0), (0, F_ROWS - SUBLANES * B_HEADS)))
        w_t = jnp.concatenate([w[:, q_lo:q_lo + B_WIDTH], w[:, q_lo + 2 * B_WIDTH:f_lo], w_f],
                              axis=1).T.astype(_BF16)
        b_f_col = jnp.pad(jnp.repeat(b_f[l], SUBLANES), (0, F_ROWS - SUBLANES * B_HEADS)).reshape(F_ROWS, 1)
        ws_cat = w_s[l].reshape(PAIRS, 2, CHUNK, CHUNK).transpose(0, 2, 1, 3).reshape(PAIRS, CHUNK, 2 * CHUNK)
        bs_exp = jnp.repeat(b_s[l].T, HEAD_DIM, axis=1)

        km, vm = _mem_kv(mem, row(g_mem[l]), w_mem_kv[l].astype(_BF16))
        qt, ka, vt, ya, ym = _mix_in(x, row(g_pre_mix[l]), w_nn, w_t, b_f_col, row(g_sgu[l]), ws_cat, bs_exp,
                                     row(g_out_a[l]), row(g_out_m[l]), km, vm)
        x = _fox_out(qt, ka, vt, ya, ym, x, w_out[l].astype(_BF16), g_out_b[l].reshape(-1, 1),
                     row(g_post_mix[l]))
        x = _ffn(x.reshape(batch * seq, d), row(g_pre_ffn[l]), w_gate[l].astype(_BF16), w_up[l].astype(_BF16),
                 w_down[l].astype(_BF16), row(g_post_ffn[l])).reshape(batch, seq, d)
    return x
```

```python
import functools

import jax
import jax.numpy as jnp
from jax import lax
from jax.experimental import pallas as pl
from jax.experimental.pallas import tpu as pltpu

D_MODEL = 1024
HEAD_DIM = 64
A_GROUPS = 6
B_HEADS = 6
M_HEADS = 4
A_WIDTH = A_GROUPS * HEAD_DIM
B_WIDTH = B_HEADS * HEAD_DIM
M_WIDTH = M_HEADS * HEAD_DIM
CHUNK = 128
MEM_TOKENS = 256
D_FF = 2816
EPS = 1e-6
NEG_INF = -1e30

LANES = 128
SUBLANES = 8
PAIRS = B_HEADS // 2
SCALE = HEAD_DIM ** -0.5
LOG2E = 1.4426950408889634

K_OFF = 2 * A_WIDTH
QM_OFF = K_OFF + B_WIDTH
NN_COLS = QM_OFF + M_WIDTH
V_ROW = B_WIDTH
F_ROW = 2 * B_WIDTH
F_ROWS = 64
NT_ROWS = F_ROW + F_ROWS
VT_ROWS = HEAD_DIM + 16

T_ATT = 512
TM_FFN = 1024
FF_ROW_SPLIT = 2
FF_SPLIT = 11
QK_AHEAD = 2

VMEM_LIMIT = 56 * 1024 * 1024

_F32 = jnp.float32
_BF16 = jnp.bfloat16


def _dot(a, b):
    return jnp.dot(a, b, preferred_element_type=_F32)


def _dot_nt(a, b):
    return lax.dot_general(a, b, (((1,), (1,)), ((), ())), preferred_element_type=_F32)


def _rms(x, g):
    return x * lax.rsqrt(jnp.mean(x * x, axis=-1, keepdims=True) + EPS) * g


def _gelu_tanh(x):
    return 0.5 * x * (1.0 + jnp.tanh(0.7978845608028654 * (x + 0.044715 * (x * x * x))))


def _log_sigmoid(x):
    return -(jnp.maximum(-x, 0.0) + jnp.log1p(jnp.exp(-jnp.abs(x))))


def _split3(x):
    hi = x.astype(_BF16).astype(_F32)
    r = x - hi
    mid = r.astype(_BF16).astype(_F32)
    lo = (r - mid).astype(_BF16).astype(_F32)
    return hi, mid, lo


def _mem_kv_kernel(mem_ref, g_ref, w_ref, km_ref, vm_ref):
    mn = _rms(mem_ref[0], g_ref[...]).astype(_BF16)
    kv = _dot(mn, w_ref[...])
    km = kv[:, :M_WIDTH]
    vm = kv[:, M_WIDTH:]
    head = lax.broadcasted_iota(jnp.int32, (MEM_TOKENS, M_WIDTH), 1) // HEAD_DIM
    for h in range(M_HEADS):
        km_ref[0, h] = jnp.where(head == h, km, 0.0).astype(_BF16)
        vm_ref[0, h] = jnp.where(head == h, vm, 0.0).astype(_BF16)


def _mem_kv(mem, g_mem, w_kv):
    batch = mem.shape[0]
    out = jax.ShapeDtypeStruct((batch, M_HEADS, MEM_TOKENS, M_WIDTH), _BF16)
    return pl.pallas_call(
        _mem_kv_kernel,
        grid=(batch,),
        in_specs=[
            pl.BlockSpec((1, MEM_TOKENS, D_MODEL), lambda b: (b, 0, 0)),
            pl.BlockSpec((1, D_MODEL), lambda b: (0, 0)),
            pl.BlockSpec((D_MODEL, 2 * M_WIDTH), lambda b: (0, 0)),
        ],
        out_specs=[
            pl.BlockSpec((1, M_HEADS, MEM_TOKENS, M_WIDTH), lambda b: (b, 0, 0, 0)),
            pl.BlockSpec((1, M_HEADS, MEM_TOKENS, M_WIDTH), lambda b: (b, 0, 0, 0)),
        ],
        out_shape=[out, out],
        compiler_params=pltpu.CompilerParams(dimension_semantics=("parallel",)),
        name="mem_kv",
    )(mem, g_mem, w_kv)


def _mix_in_kernel(x_ref, gpre_ref, wnn_ref, wt_ref, bf_ref, gsgu_ref, ws_ref, bs_ref, goa_ref, gom_ref,
                   km_ref, vm_ref, qt_ref, ka_ref, vt_ref, ya_ref, ym_ref, carry_ref):
    tm = x_ref.shape[1]

    @pl.when(pl.program_id(1) == 0)
    def _():
        carry_ref[...] = jnp.zeros_like(carry_ref)

    h = _rms(x_ref[0], gpre_ref[...]).astype(_BF16)
    t_zu = _dot(h, wnn_ref[:, 0:A_WIDTH])
    t_zv = _dot(h, wnn_ref[:, A_WIDTH:2 * A_WIDTH])

    t_qm = _dot(h, wnn_ref[:, QM_OFF:NN_COLS])
    u = _gelu_tanh(t_zu)
    v_t = _dot_nt(wt_ref[V_ROW:V_ROW + B_WIDTH, :], h)
    vn = _rms(_gelu_tanh(t_zv), gsgu_ref[...])
    q_t = _dot_nt(wt_ref[0:B_WIDTH, :], h)

    t_f = _dot_nt(wt_ref[F_ROW:NT_ROWS, :], h)
    rowf = lax.broadcasted_iota(jnp.int32, (F_ROWS, tm), 0)
    logf = _log_sigmoid(t_f + bf_ref[...])
    logf = jnp.where(rowf < SUBLANES * B_HEADS, logf, 0.0)
    parts = jnp.concatenate(_split3(logf), axis=0).astype(_BF16)
    triu = (lax.broadcasted_iota(jnp.int32, (tm, tm), 0)
            <= lax.broadcasted_iota(jnp.int32, (tm, tm), 1)).astype(_BF16)
    cs = _dot(parts, triu)
    c = cs[0:F_ROWS] + cs[F_ROWS:2 * F_ROWS] + cs[2 * F_ROWS:] + carry_ref[...]
    carry_ref[...] = c[:, tm - 1:tm]
    c_hi, c_mid, c_lo = _split3(c * LOG2E)
    j8 = rowf % SUBLANES
    cq = jnp.where(j8 == 0, c_hi, jnp.where(j8 == 1, c_mid, jnp.where(j8 == 2, c_lo,
                                                                        jnp.where(j8 < 6, 1.0, 0.0))))
    ck = jnp.where(j8 < 3, 1.0, jnp.where(j8 == 3, -c_hi, jnp.where(j8 == 4, -c_mid,
                                                                       jnp.where(j8 == 5, -c_lo, 0.0))))
    ck_t = jnp.concatenate([ck, jnp.zeros((LANES - F_ROWS, tm), _F32)], axis=0).T


    ones_v = jnp.where(lax.broadcasted_iota(jnp.int32, (VT_ROWS - HEAD_DIM, tm), 0) == 0, 1.0, 0.0)
    for hd in range(B_HEADS):
        vt_ref[0, hd, 0] = jnp.concatenate([v_t[hd * HEAD_DIM:(hd + 1) * HEAD_DIM], ones_v], axis=0).astype(_BF16)

    lane = lax.broadcasted_iota(jnp.int32, (CHUNK, LANES), 1)
    row_s = lax.broadcasted_iota(jnp.int32, (CHUNK, 2 * CHUNK), 0)
    col_s = lax.broadcasted_iota(jnp.int32, (CHUNK, 2 * CHUNK), 1) % CHUNK
    z_pairs = []
    for p in range(PAIRS):
        w_pair = jnp.where(col_s <= row_s, ws_ref[p], 0.0).astype(_BF16)
        z_chunks = []
        for c_i in range(tm // CHUNK):
            v_pair = vn[c_i * CHUNK:(c_i + 1) * CHUNK, p * LANES:(p + 1) * LANES]
            rhs = jnp.concatenate([jnp.where(lane < HEAD_DIM, v_pair, 0.0),
                                   jnp.where(lane >= HEAD_DIM, v_pair, 0.0)], axis=0).astype(_BF16)
            z_chunks.append(_dot(w_pair, rhs) + bs_ref[:, p * LANES:(p + 1) * LANES])
        z_pairs.append(jnp.concatenate(z_chunks, axis=0))
    k = _dot(h, wnn_ref[:, K_OFF:K_OFF + B_WIDTH])
    y_a = u * jnp.concatenate(z_pairs, axis=1)
    ya_ref[0] = _rms(y_a, goa_ref[...]).astype(_BF16)

    q_t = q_t * (SCALE * LOG2E)
    zeros_q = jnp.zeros((LANES - HEAD_DIM - SUBLANES, tm), _F32)
    for hd in range(B_HEADS):
        q_h = q_t[hd * HEAD_DIM:(hd + 1) * HEAD_DIM]
        cq_h = cq[SUBLANES * hd:SUBLANES * (hd + 1)]
        q_aug = [q_h, cq_h, zeros_q] if hd % 2 == 0 else [cq_h, zeros_q, q_h]
        qt_ref[0, hd, 0] = jnp.concatenate(q_aug, axis=0).astype(_BF16)

    lane_t = lax.broadcasted_iota(jnp.int32, (tm, LANES), 1)

    def k_table(hd):
        k_pair = k[:, (hd // 2) * LANES:(hd // 2 + 1) * LANES]
        if hd % 2 == 0:
            k_aug = jnp.where(lane_t < HEAD_DIM, k_pair, pltpu.roll(ck_t, HEAD_DIM - SUBLANES * hd, 1))
        else:
            k_aug = jnp.where(lane_t >= HEAD_DIM, k_pair, pltpu.roll(ck_t, LANES - SUBLANES * hd, 1))
        ka_ref[0, hd] = k_aug.astype(_BF16)

    qm = (t_qm * SCALE).astype(_BF16)
    qk_m = lambda hh: _dot_nt(qm, km_ref[0, hh])
    s_m = [qk_m(0), qk_m(1)]
    o_m = jnp.zeros((tm, M_WIDTH), _F32)
    for hh in range(M_HEADS):
        s = s_m[hh]
        e = jnp.exp(s - jnp.max(s, axis=-1, keepdims=True))
        p_m = (e * (1.0 / jnp.sum(e, axis=-1, keepdims=True))).astype(_BF16)
        if hh + 2 < M_HEADS:
            s_m.append(qk_m(hh + 2))
        o_m = o_m + _dot(p_m, vm_ref[0, hh])
        for hd in range(2 * hh, min(2 * hh + 2, B_HEADS)):
            k_table(hd)
    ym_ref[0] = _rms(o_m, gom_ref[...]).astype(_BF16)


def _mix_in(x, g_pre, w_nn, w_t, b_f_col, g_sgu, ws_cat, bs_exp, g_out_a, g_out_m, km, vm):
    batch, seq, _ = x.shape
    tm = T_ATT
    nblk = seq // tm
    const2 = lambda b, i: (0, 0)
    mem_spec = pl.BlockSpec((1, M_HEADS, MEM_TOKENS, M_WIDTH), lambda b, i: (b, 0, 0, 0))
    return pl.pallas_call(
        _mix_in_kernel,
        grid=(batch, nblk),
        in_specs=[
            pl.BlockSpec((1, tm, D_MODEL), lambda b, i: (b, i, 0)),
            pl.BlockSpec((1, D_MODEL), const2),
            pl.BlockSpec((D_MODEL, NN_COLS), const2),
            pl.BlockSpec((NT_ROWS, D_MODEL), const2),
            pl.BlockSpec((F_ROWS, 1), const2),
            pl.BlockSpec((1, A_WIDTH), const2),
            pl.BlockSpec((PAIRS, CHUNK, 2 * CHUNK), lambda b, i: (0, 0, 0)),
            pl.BlockSpec((CHUNK, A_WIDTH), const2),
            pl.BlockSpec((1, A_WIDTH), const2),
            pl.BlockSpec((1, M_WIDTH), const2),
            mem_spec, mem_spec,
        ],
        out_specs=[
            pl.BlockSpec((1, B_HEADS, 1, LANES, tm), lambda b, i: (b, 0, i, 0, 0)),
            pl.BlockSpec((1, B_HEADS, tm, LANES), lambda b, i: (b, 0, i, 0)),
            pl.BlockSpec((1, B_HEADS, 1, VT_ROWS, tm), lambda b, i: (b, 0, i, 0, 0)),
            pl.BlockSpec((1, tm, A_WIDTH), lambda b, i: (b, i, 0)),
            pl.BlockSpec((1, tm, M_WIDTH), lambda b, i: (b, i, 0)),
        ],
        out_shape=[jax.ShapeDtypeStruct((batch, B_HEADS, nblk, LANES, tm), _BF16),
                   jax.ShapeDtypeStruct((batch, B_HEADS, seq, LANES), _BF16),
                   jax.ShapeDtypeStruct((batch, B_HEADS, nblk, VT_ROWS, tm), _BF16),
                   jax.ShapeDtypeStruct((batch, seq, A_WIDTH), _BF16),
                   jax.ShapeDtypeStruct((batch, seq, M_WIDTH), _BF16)],
        scratch_shapes=[pltpu.VMEM((F_ROWS, 1), _F32)],
        compiler_params=pltpu.CompilerParams(dimension_semantics=("parallel", "arbitrary"),
                                             vmem_limit_bytes=VMEM_LIMIT),
        name="mix_in",
    )(x, g_pre, w_nn, w_t, b_f_col, g_sgu, ws_cat, bs_exp, g_out_a, g_out_m, km, vm)


def _fox_out_kernel(qt_ref, ka_ref, vt_ref, ya_ref, ym_ref, x_ref, wout_ref, gob_ref, gpost_ref, x1_ref):
    t = T_ATT
    i = pl.program_id(1)
    causal = (lax.broadcasted_iota(jnp.int32, (t, t), 0) <= lax.broadcasted_iota(jnp.int32, (t, t), 1))

    def attend(j, carry, masked):
        start = pl.multiple_of(j * t, t)
        out = []
        qk = lambda hd: _dot(ka_ref[0, hd, pl.ds(start, t), :], qt_ref[0, hd, 0])
        scores = [qk(hd) if hd < QK_AHEAD else None for hd in range(B_HEADS)]
        for hd in range(B_HEADS):
            m, acc = carry[hd]
            s = scores[hd]
            if masked:
                s = jnp.where(causal, s, NEG_INF)
            m_new = jnp.maximum(m, jnp.max(s, axis=0, keepdims=True))
            p = jnp.exp2(s - m_new).astype(_BF16)
            if hd + QK_AHEAD < B_HEADS:
                scores[hd + QK_AHEAD] = qk(hd + QK_AHEAD)
            acc = jnp.exp2(m - m_new) * acc + _dot(vt_ref[0, hd, j], p)
            out.append((m_new, acc))
        return tuple(out)

    init = tuple((jnp.full((1, t), NEG_INF, _F32), jnp.zeros((VT_ROWS, t), _F32)) for _ in range(B_HEADS))
    state = lax.fori_loop(0, i, lambda j, c: attend(j, c, False), init)
    state = attend(i, state, True)
    heads = []
    for hd in range(B_HEADS):
        acc = state[hd][1]
        heads.append(acc[0:HEAD_DIM] * (1.0 / acc[HEAD_DIM:HEAD_DIM + 1]))
    yb_t = jnp.concatenate(heads, axis=0)
    yb_t = yb_t * lax.rsqrt(jnp.mean(yb_t * yb_t, axis=0, keepdims=True) + EPS) * gob_ref[...]
    yb_n = yb_t.T.astype(_BF16)

    z = (_dot(ya_ref[0], wout_ref[0:A_WIDTH, :])
         + _dot(yb_n, wout_ref[A_WIDTH:A_WIDTH + B_WIDTH, :])
         + _dot(ym_ref[0], wout_ref[A_WIDTH + B_WIDTH:, :]))
    x1_ref[0] = x_ref[0] + _rms(z, gpost_ref[...])


def _fox_out(qt, ka, vt, ya, ym, x, w_out, g_out_b_col, g_post):
    batch, seq, _ = x.shape
    t = T_ATT
    nblk = seq // t
    const2 = lambda b, i: (0, 0)
    return pl.pallas_call(
        _fox_out_kernel,
        grid=(batch, nblk),
        in_specs=[
            pl.BlockSpec((1, B_HEADS, 1, LANES, t), lambda b, i: (b, 0, i, 0, 0)),
            pl.BlockSpec((1, B_HEADS, seq, LANES), lambda b, i: (b, 0, 0, 0)),
            pl.BlockSpec((1, B_HEADS, nblk, VT_ROWS, t), lambda b, i: (b, 0, 0, 0, 0)),
            pl.BlockSpec((1, t, A_WIDTH), lambda b, i: (b, i, 0)),
            pl.BlockSpec((1, t, M_WIDTH), lambda b, i: (b, i, 0)),
            pl.BlockSpec((1, t, D_MODEL), lambda b, i: (b, i, 0)),
            pl.BlockSpec((D_MODEL, D_MODEL), const2),
            pl.BlockSpec((B_WIDTH, 1), const2),
            pl.BlockSpec((1, D_MODEL), const2),
        ],
        out_specs=pl.BlockSpec((1, t, D_MODEL), lambda b, i: (b, i, 0)),
        out_shape=jax.ShapeDtypeStruct(x.shape, x.dtype),
        compiler_params=pltpu.CompilerParams(dimension_semantics=("parallel", "parallel"),
                                             vmem_limit_bytes=VMEM_LIMIT),
        name="fox_out",
    )(qt, ka, vt, ya, ym, x, w_out, g_out_b_col, g_post)


def _ffn_kernel(x_ref, gpre_ref, wg_ref, wu_ref, wd_ref, gpost_ref, o_ref):
    step = D_FF // FF_SPLIT
    rows = x_ref.shape[0] // FF_ROW_SPLIT
    for r in range(FF_ROW_SPLIT):
        x = x_ref[r * rows:(r + 1) * rows, :]
        h = _rms(x, gpre_ref[...]).astype(_BF16)
        ff = None
        for c in range(FF_SPLIT):
            sl = slice(c * step, (c + 1) * step)
            g = _dot(h, wg_ref[:, sl])
            a = (g * (1.0 / (1.0 + jnp.exp(-g))) * _dot(h, wu_ref[:, sl])).astype(_BF16)
            part = _dot(a, wd_ref[sl, :])
            ff = part if ff is None else ff + part
        o_ref[r * rows:(r + 1) * rows, :] = x + _rms(ff, gpost_ref[...])


def _ffn(x, g_pre, w_gate, w_up, w_down, g_post):
    tokens = x.shape[0]
    tm = TM_FFN
    const = lambda i: (0, 0)
    resident = functools.partial(pl.BlockSpec, index_map=const, pipeline_mode=pl.Buffered(1))
    return pl.pallas_call(
        _ffn_kernel,
        grid=(tokens // tm,),
        in_specs=[
            pl.BlockSpec((tm, D_MODEL), lambda i: (i, 0)),
            pl.BlockSpec((1, D_MODEL), const),
            resident((D_MODEL, D_FF)),
            resident((D_MODEL, D_FF)),
            resident((D_FF, D_MODEL)),
            pl.BlockSpec((1, D_MODEL), const),
        ],
        out_specs=pl.BlockSpec((tm, D_MODEL), lambda i: (i, 0)),
        out_shape=jax.ShapeDtypeStruct(x.shape, x.dtype),
        compiler_params=pltpu.CompilerParams(dimension_semantics=("parallel",),
                                             vmem_limit_bytes=VMEM_LIMIT),
        name="ffn",
    )(x, g_pre, w_gate, w_up, w_down, g_post)


def kernel(x, mem, g_pre_mix, w_in, b_f, g_sgu, w_s, b_s, g_out_a, g_out_b, g_out_m, g_mem, w_mem_kv, w_out,
           g_post_mix, g_pre_ffn, w_gate, w_up, w_down, g_post_ffn):
    batch, seq, d = x.shape
    depth = w_in.shape[0]
    row = lambda a: a.reshape(1, -1)
    q_lo = 2 * A_WIDTH
    f_lo = q_lo + 3 * B_WIDTH
    for l in range(depth):
        w = w_in[l]
        w_nn = jnp.concatenate([w[:, :q_lo], w[:, q_lo + B_WIDTH:q_lo + 2 * B_WIDTH], w[:, f_lo + B_HEADS:]],
                               axis=1).astype(_BF16)
        w_f = jnp.pad(jnp.repeat(w[:, f_lo:f_lo + B_HEADS], SUBLANES, axis=1),
                      ((0, 0), (0, F_ROWS - SUBLANES * B_HEADS)))
        w_t = jnp.concatenate([w[:, q_lo:q_lo + B_WIDTH], w[:, q_lo + 2 * B_WIDTH:f_lo], w_f],
                              axis=1).T.astype(_BF16)
        b_f_col = jnp.pad(jnp.repeat(b_f[l], SUBLANES), (0, F_ROWS - SUBLANES * B_HEADS)).reshape(F_ROWS, 1)
        ws_cat = w_s[l].reshape(PAIRS, 2, CHUNK, CHUNK).transpose(0, 2, 1, 3).reshape(PAIRS, CHUNK, 2 * CHUNK)
        bs_exp = jnp.repeat(b_s[l].T, HEAD_DIM, axis=1)

        km, vm = _mem_kv(mem, row(g_mem[l]), w_mem_kv[l].astype(_BF16))
        qt, ka, vt, ya, ym = _mix_in(x, row(g_pre_mix[l]), w_nn, w_t, b_f_col, row(g_sgu[l]), ws_cat, bs_exp,
                                     row(g_out_a[l]), row(g_out_m[l]), km, vm)
        x = _fox_out(qt, ka, vt, ya, ym, x, w_out[l].astype(_BF16), g_out_b[l].reshape(-1, 1),
                     row(g_post_mix[l]))
        x = _ffn(x.reshape(batch * seq, d), row(g_pre_ffn[l]), w_gate[l].astype(_BF16), w_up[l].astype(_BF16),
                 w_down[l].astype(_BF16), row(g_post_ffn[l])).reshape(batch, seq, d)
    return x
```

```python
import functools

import jax
import jax.numpy as jnp
from jax import lax
from jax.experimental import pallas as pl
from jax.experimental.pallas import tpu as pltpu

D_MODEL = 1024
HEAD_DIM = 64
A_GROUPS = 6
B_HEADS = 6
M_HEADS = 4
A_WIDTH = A_GROUPS * HEAD_DIM
B_WIDTH = B_HEADS * HEAD_DIM
M_WIDTH = M_HEADS * HEAD_DIM
CHUNK = 128
MEM_TOKENS = 256
D_FF = 2816
EPS = 1e-6
NEG_INF = -1e30

LANES = 128
SUBLANES = 8
PAIRS = B_HEADS // 2
SCALE = HEAD_DIM ** -0.5
LOG2E = 1.4426950408889634

K_OFF = 2 * A_WIDTH
NN_COLS = K_OFF + B_WIDTH
V_ROW = B_WIDTH
F_ROW = 2 * B_WIDTH
F_ROWS = 64
QM_ROW = F_ROW + F_ROWS
NT_ROWS = QM_ROW + M_WIDTH
VT_ROWS = HEAD_DIM + 16

T_ATT = 512
TM_FFN = 1024
FF_ROW_SPLIT = 2
FF_OVERLAP_AT = 1
FF_SPLIT = 11
Q_SPLIT = 2
PV_BEHIND = 0
QK_AHEAD = 4

VMEM_LIMIT = 56 * 1024 * 1024

_F32 = jnp.float32
_BF16 = jnp.bfloat16


def _dot(a, b):
    return jnp.dot(a, b, preferred_element_type=_F32)


def _dot_nt(a, b):
    return lax.dot_general(a, b, (((1,), (1,)), ((), ())), preferred_element_type=_F32)


def _rms(x, g):
    return x * lax.rsqrt(jnp.mean(x * x, axis=-1, keepdims=True) + EPS) * g


def _gelu_tanh(x):
    return 0.5 * x * (1.0 + jnp.tanh(0.7978845608028654 * (x + 0.044715 * (x * x * x))))


def _log_sigmoid(x):
    return -(jnp.maximum(-x, 0.0) + jnp.log1p(jnp.exp(-jnp.abs(x))))


def _split3(x):
    hi = x.astype(_BF16).astype(_F32)
    r = x - hi
    mid = r.astype(_BF16).astype(_F32)
    lo = (r - mid).astype(_BF16).astype(_F32)
    return hi, mid, lo


def _mem_kv_kernel(mem_ref, g_ref, wk_ref, wvt_ref, km_ref, vmt_ref):
    mn = _rms(mem_ref[0], g_ref[...]).astype(_BF16)
    km = _dot(mn, wk_ref[...])
    vm_t = _dot_nt(wvt_ref[...], mn)
    head_k = lax.broadcasted_iota(jnp.int32, (MEM_TOKENS, M_WIDTH), 1) // HEAD_DIM
    head_v = lax.broadcasted_iota(jnp.int32, (M_WIDTH, MEM_TOKENS), 0) // HEAD_DIM
    for h in range(M_HEADS):
        km_ref[0, h] = jnp.where(head_k == h, km, 0.0).astype(_BF16)
        vmt_ref[0, h] = jnp.where(head_v == h, vm_t, 0.0).astype(_BF16)


def _mem_kv(mem, g_mem, w_k, w_v_t):
    batch = mem.shape[0]
    return pl.pallas_call(
        _mem_kv_kernel,
        grid=(batch,),
        in_specs=[
            pl.BlockSpec((1, MEM_TOKENS, D_MODEL), lambda b: (b, 0, 0)),
            pl.BlockSpec((1, D_MODEL), lambda b: (0, 0)),
            pl.BlockSpec((D_MODEL, M_WIDTH), lambda b: (0, 0)),
            pl.BlockSpec((M_WIDTH, D_MODEL), lambda b: (0, 0)),
        ],
        out_specs=[
            pl.BlockSpec((1, M_HEADS, MEM_TOKENS, M_WIDTH), lambda b: (b, 0, 0, 0)),
            pl.BlockSpec((1, M_HEADS, M_WIDTH, MEM_TOKENS), lambda b: (b, 0, 0, 0)),
        ],
        out_shape=[jax.ShapeDtypeStruct((batch, M_HEADS, MEM_TOKENS, M_WIDTH), _BF16),
                   jax.ShapeDtypeStruct((batch, M_HEADS, M_WIDTH, MEM_TOKENS), _BF16)],
        compiler_params=pltpu.CompilerParams(dimension_semantics=("parallel",)),
        name="mem_kv",
    )(mem, g_mem, w_k, w_v_t)


def _mix_in_kernel(x_ref, gpre_ref, wnn_ref, wt_ref, bf_ref, gsgu_ref, ws_ref, bs_ref, goa_ref, gom_ref,
                   km_ref, vmt_ref, qt_ref, ka_ref, vt_ref, ya_ref, ym_ref, carry_ref):
    tm = x_ref.shape[1]

    @pl.when(pl.program_id(1) == 0)
    def _():
        carry_ref[...] = jnp.zeros_like(carry_ref)

    h = _rms(x_ref[0], gpre_ref[...]).astype(_BF16)
    t_zu = _dot(h, wnn_ref[:, 0:A_WIDTH])
    t_zv = _dot(h, wnn_ref[:, A_WIDTH:2 * A_WIDTH])

    t_qm = _dot_nt(wt_ref[QM_ROW:NT_ROWS, :], h)
    u = _gelu_tanh(t_zu)
    v_t = _dot_nt(wt_ref[V_ROW:V_ROW + B_WIDTH, :], h)
    vn = _rms(_gelu_tanh(t_zv), gsgu_ref[...])
    q_t = _dot_nt(wt_ref[0:B_WIDTH, :], h)

    t_f = _dot_nt(wt_ref[F_ROW:F_ROW + F_ROWS, :], h)
    rowf = lax.broadcasted_iota(jnp.int32, (F_ROWS, tm), 0)
    logf = _log_sigmoid(t_f + bf_ref[...])
    logf = jnp.where(rowf < SUBLANES * B_HEADS, logf, 0.0)
    parts = jnp.concatenate(_split3(logf), axis=0).astype(_BF16)
    triu = (lax.broadcasted_iota(jnp.int32, (tm, tm), 0)
            <= lax.broadcasted_iota(jnp.int32, (tm, tm), 1)).astype(_BF16)
    cs = _dot(parts, triu)
    c = cs[0:F_ROWS] + cs[F_ROWS:2 * F_ROWS] + cs[2 * F_ROWS:] + carry_ref[...]
    carry_ref[...] = c[:, tm - 1:tm]
    c_hi, c_mid, c_lo = _split3(c * LOG2E)
    j8 = rowf % SUBLANES
    cq = jnp.where(j8 == 0, c_hi, jnp.where(j8 == 1, c_mid, jnp.where(j8 == 2, c_lo,
                                                                        jnp.where(j8 < 6, 1.0, 0.0))))
    ck = jnp.where(j8 < 3, 1.0, jnp.where(j8 == 3, -c_hi, jnp.where(j8 == 4, -c_mid,
                                                                       jnp.where(j8 == 5, -c_lo, 0.0))))
    ck_t = jnp.concatenate([ck, jnp.zeros((LANES - F_ROWS, tm), _F32)], axis=0).T


    ones_v = jnp.where(lax.broadcasted_iota(jnp.int32, (VT_ROWS - HEAD_DIM, tm), 0) == 0, 1.0, 0.0)
    for hd in range(B_HEADS):
        vt_ref[0, hd, 0] = jnp.concatenate([v_t[hd * HEAD_DIM:(hd + 1) * HEAD_DIM], ones_v], axis=0).astype(_BF16)

    lane = lax.broadcasted_iota(jnp.int32, (CHUNK, LANES), 1)
    row_s = lax.broadcasted_iota(jnp.int32, (CHUNK, 2 * CHUNK), 0)
    col_s = lax.broadcasted_iota(jnp.int32, (CHUNK, 2 * CHUNK), 1) % CHUNK
    z_pairs = []
    for p in range(PAIRS):
        w_pair = jnp.where(col_s <= row_s, ws_ref[p], 0.0).astype(_BF16)
        z_chunks = []
        for c_i in range(tm // CHUNK):
            v_pair = vn[c_i * CHUNK:(c_i + 1) * CHUNK, p * LANES:(p + 1) * LANES]
            rhs = jnp.concatenate([jnp.where(lane < HEAD_DIM, v_pair, 0.0),
                                   jnp.where(lane >= HEAD_DIM, v_pair, 0.0)], axis=0).astype(_BF16)
            z_chunks.append(_dot(w_pair, rhs) + bs_ref[:, p * LANES:(p + 1) * LANES])
        z_pairs.append(jnp.concatenate(z_chunks, axis=0))
    k = _dot(h, wnn_ref[:, K_OFF:K_OFF + B_WIDTH])
    y_a = u * jnp.concatenate(z_pairs, axis=1)
    ya_ref[0] = _rms(y_a, goa_ref[...]).astype(_BF16)

    q_t = q_t * (SCALE * LOG2E)
    zeros_q = jnp.zeros((LANES - HEAD_DIM - SUBLANES, tm), _F32)
    for hd in range(B_HEADS):
        q_h = q_t[hd * HEAD_DIM:(hd + 1) * HEAD_DIM]
        cq_h = cq[SUBLANES * hd:SUBLANES * (hd + 1)]
        q_aug = [q_h, cq_h, zeros_q] if hd % 2 == 0 else [cq_h, zeros_q, q_h]
        qt_ref[0, hd, 0] = jnp.concatenate(q_aug, axis=0).astype(_BF16)

    lane_t = lax.broadcasted_iota(jnp.int32, (tm, LANES), 1)

    def k_table(hd):
        k_pair = k[:, (hd // 2) * LANES:(hd // 2 + 1) * LANES]
        if hd % 2 == 0:
            k_aug = jnp.where(lane_t < HEAD_DIM, k_pair, pltpu.roll(ck_t, HEAD_DIM - SUBLANES * hd, 1))
        else:
            k_aug = jnp.where(lane_t >= HEAD_DIM, k_pair, pltpu.roll(ck_t, LANES - SUBLANES * hd, 1))
        ka_ref[0, hd] = k_aug.astype(_BF16)

    qm_t = (t_qm * (SCALE * LOG2E)).astype(_BF16)
    qk_m = lambda hh: _dot(km_ref[0, hh], qm_t)
    s_m = [qk_m(0), qk_m(1)]
    o_t = jnp.zeros((M_WIDTH, tm), _F32)
    for hh in range(M_HEADS):
        s = s_m[hh]
        e = jnp.exp2(s - jnp.max(s, axis=0, keepdims=True))
        p_m = (e * (1.0 / jnp.sum(e, axis=0, keepdims=True))).astype(_BF16)
        if hh + 2 < M_HEADS:
            s_m.append(qk_m(hh + 2))
        o_t = o_t + _dot(vmt_ref[0, hh], p_m)
        for hd in range(2 * hh, min(2 * hh + 2, B_HEADS)):
            k_table(hd)
    ym_t = o_t * lax.rsqrt(jnp.mean(o_t * o_t, axis=0, keepdims=True) + EPS) * gom_ref[...]
    ym_ref[0] = ym_t.T.astype(_BF16)


def _mix_in(x, g_pre, w_nn, w_t, b_f_col, g_sgu, ws_cat, bs_exp, g_out_a, g_out_m_col, km, vm_t):
    batch, seq, _ = x.shape
    tm = T_ATT
    nblk = seq // tm
    const2 = lambda b, i: (0, 0)
    mem_spec = pl.BlockSpec((1, M_HEADS, MEM_TOKENS, M_WIDTH), lambda b, i: (b, 0, 0, 0))
    return pl.pallas_call(
        _mix_in_kernel,
        grid=(batch, nblk),
        in_specs=[
            pl.BlockSpec((1, tm, D_MODEL), lambda b, i: (b, i, 0)),
            pl.BlockSpec((1, D_MODEL), const2),
            pl.BlockSpec((D_MODEL, NN_COLS), const2),
            pl.BlockSpec((NT_ROWS, D_MODEL), const2),
            pl.BlockSpec((F_ROWS, 1), const2),
            pl.BlockSpec((1, A_WIDTH), const2),
            pl.BlockSpec((PAIRS, CHUNK, 2 * CHUNK), lambda b, i: (0, 0, 0)),
            pl.BlockSpec((CHUNK, A_WIDTH), const2),
            pl.BlockSpec((1, A_WIDTH), const2),
            pl.BlockSpec((M_WIDTH, 1), const2),
            mem_spec, mem_spec,
        ],
        out_specs=[
            pl.BlockSpec((1, B_HEADS, 1, LANES, tm), lambda b, i: (b, 0, i, 0, 0)),
            pl.BlockSpec((1, B_HEADS, tm, LANES), lambda b, i: (b, 0, i, 0)),
            pl.BlockSpec((1, B_HEADS, 1, VT_ROWS, tm), lambda b, i: (b, 0, i, 0, 0)),
            pl.BlockSpec((1, tm, A_WIDTH), lambda b, i: (b, i, 0)),
            pl.BlockSpec((1, tm, M_WIDTH), lambda b, i: (b, i, 0)),
        ],
        out_shape=[jax.ShapeDtypeStruct((batch, B_HEADS, nblk, LANES, tm), _BF16),
                   jax.ShapeDtypeStruct((batch, B_HEADS, seq, LANES), _BF16),
                   jax.ShapeDtypeStruct((batch, B_HEADS, nblk, VT_ROWS, tm), _BF16),
                   jax.ShapeDtypeStruct((batch, seq, A_WIDTH), _BF16),
                   jax.ShapeDtypeStruct((batch, seq, M_WIDTH), _BF16)],
        scratch_shapes=[pltpu.VMEM((F_ROWS, 1), _F32)],
        compiler_params=pltpu.CompilerParams(dimension_semantics=("parallel", "arbitrary"),
                                             vmem_limit_bytes=VMEM_LIMIT),
        name="mix_in",
    )(x, g_pre, w_nn, w_t, b_f_col, g_sgu, ws_cat, bs_exp, g_out_a, g_out_m_col, km, vm_t)


def _fox_out_kernel(qt_ref, ka_ref, vt_ref, ya_ref, ym_ref, x_ref, wout_ref, gob_ref, gpost_ref, x1_ref):
    t = T_ATT
    i = pl.program_id(1)
    causal = (lax.broadcasted_iota(jnp.int32, (t, t), 0) <= lax.broadcasted_iota(jnp.int32, (t, t), 1))

    w = t // Q_SPLIT
    chains = [(hd, qh) for hd in range(B_HEADS) for qh in range(Q_SPLIT)]

    def attend(j, carry, diag):
        start = pl.multiple_of(j * t, t)

        def qk(c):
            hd, qh = chains[c]
            nk = (qh + 1) * w if diag else t
            return _dot(ka_ref[0, hd, pl.ds(start, nk), :], qt_ref[0, hd, 0, :, qh * w:(qh + 1) * w])

        scores = {c: qk(c) for c in range(QK_AHEAD)}
        out = []
        pending = []

        def pv(item):
            hd, m, m_new, acc, p = item
            nk = p.shape[0]
            out.append((m_new, jnp.exp2(m - m_new) * acc + _dot(vt_ref[0, hd, j, :, 0:nk], p)))

        for c, (hd, qh) in enumerate(chains):
            m, acc = carry[c]
            s = scores.pop(c)
            nk = s.shape[0]
            if diag:
                s = jnp.where(causal[0:nk, qh * w:(qh + 1) * w], s, NEG_INF)
            m_new = jnp.maximum(m, jnp.max(s, axis=0, keepdims=True))
            p = jnp.exp2(s - m_new).astype(_BF16)
            if c + QK_AHEAD < len(chains):
                scores[c + QK_AHEAD] = qk(c + QK_AHEAD)
            pending.append((hd, m, m_new, acc, p))
            if len(pending) > PV_BEHIND:
                pv(pending.pop(0))
        while pending:
            pv(pending.pop(0))
        return tuple(out)

    init = tuple((jnp.full((1, w), NEG_INF, _F32), jnp.zeros((VT_ROWS, w), _F32)) for _ in chains)
    state = lax.fori_loop(0, i, lambda j, c: attend(j, c, False), init)
    state = attend(i, state, True)
    heads = []
    for hd in range(B_HEADS):
        acc = jnp.concatenate([state[hd * Q_SPLIT + qh][1] for qh in range(Q_SPLIT)], axis=1)
        heads.append(acc[0:HEAD_DIM] * (1.0 / acc[HEAD_DIM:HEAD_DIM + 1]))
    yb_t = jnp.concatenate(heads, axis=0)
    yb_t = yb_t * lax.rsqrt(jnp.mean(yb_t * yb_t, axis=0, keepdims=True) + EPS) * gob_ref[...]
    yb_n = yb_t.T.astype(_BF16)

    z = (_dot(ya_ref[0], wout_ref[0:A_WIDTH, :])
         + _dot(yb_n, wout_ref[A_WIDTH:A_WIDTH + B_WIDTH, :])
         + _dot(ym_ref[0], wout_ref[A_WIDTH + B_WIDTH:, :]))
    x1_ref[0] = x_ref[0] + _rms(z, gpost_ref[...])


def _fox_out(qt, ka, vt, ya, ym, x, w_out, g_out_b_col, g_post):
    batch, seq, _ = x.shape
    t = T_ATT
    nblk = seq // t
    const2 = lambda b, i: (0, 0)
    return pl.pallas_call(
        _fox_out_kernel,
        grid=(batch, nblk),
        in_specs=[
            pl.BlockSpec((1, B_HEADS, 1, LANES, t), lambda b, i: (b, 0, i, 0, 0)),
            pl.BlockSpec((1, B_HEADS, seq, LANES), lambda b, i: (b, 0, 0, 0)),
            pl.BlockSpec((1, B_HEADS, nblk, VT_ROWS, t), lambda b, i: (b, 0, 0, 0, 0)),
            pl.BlockSpec((1, t, A_WIDTH), lambda b, i: (b, i, 0)),
            pl.BlockSpec((1, t, M_WIDTH), lambda b, i: (b, i, 0)),
            pl.BlockSpec((1, t, D_MODEL), lambda b, i: (b, i, 0)),
            pl.BlockSpec((D_MODEL, D_MODEL), const2),
            pl.BlockSpec((B_WIDTH, 1), const2),
            pl.BlockSpec((1, D_MODEL), const2),
        ],
        out_specs=pl.BlockSpec((1, t, D_MODEL), lambda b, i: (b, i, 0)),
        out_shape=jax.ShapeDtypeStruct(x.shape, x.dtype),
        compiler_params=pltpu.CompilerParams(dimension_semantics=("parallel", "parallel"),
                                             vmem_limit_bytes=VMEM_LIMIT),
        name="fox_out",
    )(qt, ka, vt, ya, ym, x, w_out, g_out_b_col, g_post)


def _ffn_kernel(x_ref, gpre_ref, wg_ref, wu_ref, wd_ref, gpost_ref, o_ref):
    step = D_FF // FF_SPLIT
    rows = x_ref.shape[0] // FF_ROW_SPLIT

    def pre_norm(r):
        x = x_ref[r * rows:(r + 1) * rows, :]
        return x, _rms(x, gpre_ref[...]).astype(_BF16)

    def finish(r, x, ff):
        o_ref[r * rows:(r + 1) * rows, :] = x + _rms(ff, gpost_ref[...])

    cur = pre_norm(0)
    done = None
    for r in range(FF_ROW_SPLIT):
        x, h = cur
        ff = None
        for c in range(FF_SPLIT):
            sl = slice(c * step, (c + 1) * step)
            g = _dot(h, wg_ref[:, sl])
            a = (g * (1.0 / (1.0 + jnp.exp(-g))) * _dot(h, wu_ref[:, sl])).astype(_BF16)
            part = _dot(a, wd_ref[sl, :])
            ff = part if ff is None else ff + part
            if c == FF_OVERLAP_AT:
                if done is not None:
                    finish(*done)
                if r + 1 < FF_ROW_SPLIT:
                    cur = pre_norm(r + 1)
        done = (r, x, ff)
    finish(*done)


def _ffn(x, g_pre, w_gate, w_up, w_down, g_post):
    tokens = x.shape[0]
    tm = TM_FFN
    const = lambda i: (0, 0)
    resident = functools.partial(pl.BlockSpec, index_map=const, pipeline_mode=pl.Buffered(1))
    return pl.pallas_call(
        _ffn_kernel,
        grid=(tokens // tm,),
        in_specs=[
            pl.BlockSpec((tm, D_MODEL), lambda i: (i, 0)),
            pl.BlockSpec((1, D_MODEL), const),
            resident((D_MODEL, D_FF)),
            resident((D_MODEL, D_FF)),
            resident((D_FF, D_MODEL)),
            pl.BlockSpec((1, D_MODEL), const),
        ],
        out_specs=pl.BlockSpec((tm, D_MODEL), lambda i: (i, 0)),
        out_shape=jax.ShapeDtypeStruct(x.shape, x.dtype),
        compiler_params=pltpu.CompilerParams(dimension_semantics=("parallel",),
                                             vmem_limit_bytes=VMEM_LIMIT),
        name="ffn",
    )(x, g_pre, w_gate, w_up, w_down, g_post)


def kernel(x, mem, g_pre_mix, w_in, b_f, g_sgu, w_s, b_s, g_out_a, g_out_b, g_out_m, g_mem, w_mem_kv, w_out,
           g_post_mix, g_pre_ffn, w_gate, w_up, w_down, g_post_ffn):
    batch, seq, d = x.shape
    depth = w_in.shape[0]
    row = lambda a: a.reshape(1, -1)
    q_lo = 2 * A_WIDTH
    f_lo = q_lo + 3 * B_WIDTH
    for l in range(depth):
        w = w_in[l]
        w_nn = jnp.concatenate([w[:, :q_lo], w[:, q_lo + B_WIDTH:q_lo + 2 * B_WIDTH]],
                               axis=1).astype(_BF16)
        w_f = jnp.pad(jnp.repeat(w[:, f_lo:f_lo + B_HEADS], SUBLANES, axis=1),
                      ((0, 0), (0, F_ROWS - SUBLANES * B_HEADS)))
        w_t = jnp.concatenate([w[:, q_lo:q_lo + B_WIDTH], w[:, q_lo + 2 * B_WIDTH:f_lo], w_f, w[:, f_lo + B_HEADS:]],
                              axis=1).T.astype(_BF16)
        b_f_col = jnp.pad(jnp.repeat(b_f[l], SUBLANES), (0, F_ROWS - SUBLANES * B_HEADS)).reshape(F_ROWS, 1)
        ws_cat = w_s[l].reshape(PAIRS, 2, CHUNK, CHUNK).transpose(0, 2, 1, 3).reshape(PAIRS, CHUNK, 2 * CHUNK)
        bs_exp = jnp.repeat(b_s[l].T, HEAD_DIM, axis=1)

        km, vm_t = _mem_kv(mem, row(g_mem[l]), w_mem_kv[l][:, :M_WIDTH].astype(_BF16),
                           w_mem_kv[l][:, M_WIDTH:].T.astype(_BF16))
        qt, ka, vt, ya, ym = _mix_in(x, row(g_pre_mix[l]), w_nn, w_t, b_f_col, row(g_sgu[l]), ws_cat, bs_exp,
                                     row(g_out_a[l]), g_out_m[l].reshape(-1, 1), km, vm_t)
        x = _fox_out(qt, ka, vt, ya, ym, x, w_out[l].astype(_BF16), g_out_b[l].reshape(-1, 1),
                     row(g_post_mix[l]))
        x = _ffn(x.reshape(batch * seq, d), row(g_pre_ffn[l]), w_gate[l].astype(_BF16), w_up[l].astype(_BF16),
                 w_down[l].astype(_BF16), row(g_post_ffn[l])).reshape(batch, seq, d)
    return x
```

```python
import functools

import jax
import jax.numpy as jnp
from jax import lax
from jax.experimental import pallas as pl
from jax.experimental.pallas import tpu as pltpu

D_MODEL = 1024
HEAD_DIM = 64
A_GROUPS = 6
B_HEADS = 6
M_HEADS = 4
A_WIDTH = A_GROUPS * HEAD_DIM
B_WIDTH = B_HEADS * HEAD_DIM
M_WIDTH = M_HEADS * HEAD_DIM
CHUNK = 128
MEM_TOKENS = 256
D_FF = 2816
EPS = 1e-6
NEG_INF = -1e30

LANES = 128
SUBLANES = 8
PAIRS = B_HEADS // 2
SCALE = HEAD_DIM ** -0.5
LOG2E = 1.4426950408889634

K_OFF = 2 * A_WIDTH
NN_COLS = K_OFF + B_WIDTH
V_ROW = B_WIDTH
F_ROW = 2 * B_WIDTH
F_ROWS = 64
QM_ROW = F_ROW + F_ROWS
NT_ROWS = QM_ROW + M_WIDTH
VT_ROWS = HEAD_DIM + 16

T_ATT = 512
TM_FFN = 1024
FF_ROW_SPLIT = 2
FF_OVERLAP_AT = 1
FF_SPLIT = 11
Q_SPLIT = 2
PV_BEHIND = 0
QK_AHEAD = 4

VMEM_LIMIT = 56 * 1024 * 1024

_F32 = jnp.float32
_BF16 = jnp.bfloat16


def _dot(a, b):
    return jnp.dot(a, b, preferred_element_type=_F32)


def _dot_nt(a, b):
    return lax.dot_general(a, b, (((1,), (1,)), ((), ())), preferred_element_type=_F32)


def _rms(x, g):
    return x * lax.rsqrt(jnp.mean(x * x, axis=-1, keepdims=True) + EPS) * g


def _gelu_tanh(x):
    return 0.5 * x * (1.0 + jnp.tanh(0.7978845608028654 * (x + 0.044715 * (x * x * x))))


def _log_sigmoid(x):
    return -(jnp.maximum(-x, 0.0) + jnp.log1p(jnp.exp(-jnp.abs(x))))


def _split3(x):
    hi = x.astype(_BF16).astype(_F32)
    r = x - hi
    mid = r.astype(_BF16).astype(_F32)
    lo = (r - mid).astype(_BF16).astype(_F32)
    return hi, mid, lo


def _mem_kv_kernel(mem_ref, g_ref, wk_ref, wvt_ref, km_ref, vmt_ref):
    mn = _rms(mem_ref[0], g_ref[...]).astype(_BF16)
    km = _dot(mn, wk_ref[...])
    vm_t = _dot_nt(wvt_ref[...], mn)
    head_k = lax.broadcasted_iota(jnp.int32, (MEM_TOKENS, M_WIDTH), 1) // HEAD_DIM
    head_v = lax.broadcasted_iota(jnp.int32, (M_WIDTH, MEM_TOKENS), 0) // HEAD_DIM
    for h in range(M_HEADS):
        km_ref[0, h] = jnp.where(head_k == h, km, 0.0).astype(_BF16)
        vmt_ref[0, h] = jnp.where(head_v == h, vm_t, 0.0).astype(_BF16)


def _mem_kv(mem, g_mem, w_k, w_v_t):
    batch = mem.shape[0]
    return pl.pallas_call(
        _mem_kv_kernel,
        grid=(batch,),
        in_specs=[
            pl.BlockSpec((1, MEM_TOKENS, D_MODEL), lambda b: (b, 0, 0)),
            pl.BlockSpec((1, D_MODEL), lambda b: (0, 0)),
            pl.BlockSpec((D_MODEL, M_WIDTH), lambda b: (0, 0)),
            pl.BlockSpec((M_WIDTH, D_MODEL), lambda b: (0, 0)),
        ],
        out_specs=[
            pl.BlockSpec((1, M_HEADS, MEM_TOKENS, M_WIDTH), lambda b: (b, 0, 0, 0)),
            pl.BlockSpec((1, M_HEADS, M_WIDTH, MEM_TOKENS), lambda b: (b, 0, 0, 0)),
        ],
        out_shape=[jax.ShapeDtypeStruct((batch, M_HEADS, MEM_TOKENS, M_WIDTH), _BF16),
                   jax.ShapeDtypeStruct((batch, M_HEADS, M_WIDTH, MEM_TOKENS), _BF16)],
        compiler_params=pltpu.CompilerParams(dimension_semantics=("parallel",)),
        name="mem_kv",
    )(mem, g_mem, w_k, w_v_t)


def _mix_in_kernel(x_ref, gpre_ref, wnn_ref, wt_ref, bf_ref, gsgu_ref, ws_ref, bs_ref, goa_ref, gom_ref,
                   km_ref, vmt_ref, qt_ref, ka_ref, vt_ref, ya_ref, ym_ref, carry_ref):
    tm = x_ref.shape[1]

    @pl.when(pl.program_id(1) == 0)
    def _():
        carry_ref[...] = jnp.zeros_like(carry_ref)

    v = {}

    def st_h():
        v["h"] = _rms(x_ref[0], gpre_ref[...]).astype(_BF16)

    def st_zu():
        v["t_zu"] = _dot(v["h"], wnn_ref[:, 0:A_WIDTH])

    def st_zv():
        v["t_zv"] = _dot(v["h"], wnn_ref[:, A_WIDTH:2 * A_WIDTH])

    def st_k():
        v["k"] = _dot(v["h"], wnn_ref[:, K_OFF:K_OFF + B_WIDTH])

    def st_qt():
        v["q_t"] = _dot_nt(wt_ref[0:B_WIDTH, :], v["h"]) * (SCALE * LOG2E)

    def st_vt():
        v["v_t"] = _dot_nt(wt_ref[V_ROW:V_ROW + B_WIDTH, :], v["h"])

    def st_f():
        v["t_f"] = _dot_nt(wt_ref[F_ROW:F_ROW + F_ROWS, :], v["h"])

    def st_qm():
        v["qm_t"] = (_dot_nt(wt_ref[QM_ROW:NT_ROWS, :], v["h"]) * (SCALE * LOG2E)).astype(_BF16)

    def st_u():
        v["u"] = _gelu_tanh(v["t_zu"])

    def st_vn():
        v["vn"] = _rms(_gelu_tanh(v["t_zv"]), gsgu_ref[...])

    def st_vt_store():
        ones_v = jnp.where(lax.broadcasted_iota(jnp.int32, (VT_ROWS - HEAD_DIM, tm), 0) == 0, 1.0, 0.0)
        for hd in range(B_HEADS):
            vt_ref[0, hd, 0] = jnp.concatenate([v["v_t"][hd * HEAD_DIM:(hd + 1) * HEAD_DIM], ones_v],
                                               axis=0).astype(_BF16)

    rowf = lax.broadcasted_iota(jnp.int32, (F_ROWS, tm), 0)

    def st_gate_parts():
        logf = _log_sigmoid(v["t_f"] + bf_ref[...])
        logf = jnp.where(rowf < SUBLANES * B_HEADS, logf, 0.0)
        v["parts"] = jnp.concatenate(_split3(logf), axis=0).astype(_BF16)

    def st_gate_dot():
        triu = (lax.broadcasted_iota(jnp.int32, (tm, tm), 0)
                <= lax.broadcasted_iota(jnp.int32, (tm, tm), 1)).astype(_BF16)
        v["cs"] = _dot(v["parts"], triu)

    def st_gate_bias():
        cs = v["cs"]
        c = cs[0:F_ROWS] + cs[F_ROWS:2 * F_ROWS] + cs[2 * F_ROWS:] + carry_ref[...]
        carry_ref[...] = c[:, tm - 1:tm]
        c_hi, c_mid, c_lo = _split3(c * LOG2E)
        j8 = rowf % SUBLANES
        v["cq"] = jnp.where(j8 == 0, c_hi, jnp.where(j8 == 1, c_mid, jnp.where(j8 == 2, c_lo,
                                                                                 jnp.where(j8 < 6, 1.0, 0.0))))
        ck = jnp.where(j8 < 3, 1.0, jnp.where(j8 == 3, -c_hi, jnp.where(j8 == 4, -c_mid,
                                                                           jnp.where(j8 == 5, -c_lo, 0.0))))
        v["ck_t"] = jnp.concatenate([ck, jnp.zeros((LANES - F_ROWS, tm), _F32)], axis=0).T

    def st_sgu():
        lane = lax.broadcasted_iota(jnp.int32, (CHUNK, LANES), 1)
        row_s = lax.broadcasted_iota(jnp.int32, (CHUNK, 2 * CHUNK), 0)
        col_s = lax.broadcasted_iota(jnp.int32, (CHUNK, 2 * CHUNK), 1) % CHUNK
        z_pairs = []
        for p in range(PAIRS):
            w_pair = jnp.where(col_s <= row_s, ws_ref[p], 0.0).astype(_BF16)
            z_chunks = []
            for c_i in range(tm // CHUNK):
                v_pair = v["vn"][c_i * CHUNK:(c_i + 1) * CHUNK, p * LANES:(p + 1) * LANES]
                rhs = jnp.concatenate([jnp.where(lane < HEAD_DIM, v_pair, 0.0),
                                       jnp.where(lane >= HEAD_DIM, v_pair, 0.0)], axis=0).astype(_BF16)
                z_chunks.append(_dot(w_pair, rhs) + bs_ref[:, p * LANES:(p + 1) * LANES])
            z_pairs.append(jnp.concatenate(z_chunks, axis=0))
        v["z"] = jnp.concatenate(z_pairs, axis=1)

    def st_ya():
        ya_ref[0] = _rms(v["u"] * v["z"], goa_ref[...]).astype(_BF16)

    def st_q_tables():
        zeros_q = jnp.zeros((LANES - HEAD_DIM - SUBLANES, tm), _F32)
        for hd in range(B_HEADS):
            q_h = v["q_t"][hd * HEAD_DIM:(hd + 1) * HEAD_DIM]
            cq_h = v["cq"][SUBLANES * hd:SUBLANES * (hd + 1)]
            q_aug = [q_h, cq_h, zeros_q] if hd % 2 == 0 else [cq_h, zeros_q, q_h]
            qt_ref[0, hd, 0] = jnp.concatenate(q_aug, axis=0).astype(_BF16)

    def st_k_tables(heads):
        lane_t = lax.broadcasted_iota(jnp.int32, (tm, LANES), 1)
        for hd in heads:
            k_pair = v["k"][:, (hd // 2) * LANES:(hd // 2 + 1) * LANES]
            if hd % 2 == 0:
                k_aug = jnp.where(lane_t < HEAD_DIM, k_pair, pltpu.roll(v["ck_t"], HEAD_DIM - SUBLANES * hd, 1))
            else:
                k_aug = jnp.where(lane_t >= HEAD_DIM, k_pair, pltpu.roll(v["ck_t"], LANES - SUBLANES * hd, 1))
            ka_ref[0, hd] = k_aug.astype(_BF16)

    def st_mem_qk(hh):
        v["s_m", hh] = _dot(km_ref[0, hh], v["qm_t"])

    def st_mem_softmax(hh):
        s = v.pop(("s_m", hh))
        e = jnp.exp2(s - jnp.max(s, axis=0, keepdims=True))
        v["p_m", hh] = (e * (1.0 / jnp.sum(e, axis=0, keepdims=True))).astype(_BF16)

    def st_mem_pv(hh):
        part = _dot(vmt_ref[0, hh], v.pop(("p_m", hh)))
        v["o_t"] = part if hh == 0 else v["o_t"] + part

    def st_ym():
        o_t = v["o_t"]
        ym_t = o_t * lax.rsqrt(jnp.mean(o_t * o_t, axis=0, keepdims=True) + EPS) * gom_ref[...]
        ym_ref[0] = ym_t.T.astype(_BF16)

    P = functools.partial
    for stage in (st_h, st_zu, st_zv, st_qm, st_u, st_vt, st_f, st_vn, st_qt, st_vt_store,
                  P(st_mem_qk, 0), P(st_mem_qk, 1), st_gate_parts, st_sgu, st_gate_dot, P(st_mem_softmax, 0),
                  st_k, P(st_mem_qk, 2), st_ya, P(st_mem_pv, 0), P(st_mem_softmax, 1), st_gate_bias,
                  P(st_mem_qk, 3), P(st_mem_pv, 1), P(st_mem_softmax, 2), st_q_tables, P(st_mem_pv, 2),
                  P(st_mem_softmax, 3), P(st_k_tables, (0, 1, 2)), P(st_mem_pv, 3), P(st_k_tables, (3, 4, 5)),
                  st_ym):
        stage()


def _mix_in(x, g_pre, w_nn, w_t, b_f_col, g_sgu, ws_cat, bs_exp, g_out_a, g_out_m_col, km, vm_t):
    batch, seq, _ = x.shape
    tm = T_ATT
    nblk = seq // tm
    const2 = lambda b, i: (0, 0)
    mem_spec = pl.BlockSpec((1, M_HEADS, MEM_TOKENS, M_WIDTH), lambda b, i: (b, 0, 0, 0))
    return pl.pallas_call(
        _mix_in_kernel,
        grid=(batch, nblk),
        in_specs=[
            pl.BlockSpec((1, tm, D_MODEL), lambda b, i: (b, i, 0)),
            pl.BlockSpec((1, D_MODEL), const2),
            pl.BlockSpec((D_MODEL, NN_COLS), const2),
            pl.BlockSpec((NT_ROWS, D_MODEL), const2),
            pl.BlockSpec((F_ROWS, 1), const2),
            pl.BlockSpec((1, A_WIDTH), const2),
            pl.BlockSpec((PAIRS, CHUNK, 2 * CHUNK), lambda b, i: (0, 0, 0)),
            pl.BlockSpec((CHUNK, A_WIDTH), const2),
            pl.BlockSpec((1, A_WIDTH), const2),
            pl.BlockSpec((M_WIDTH, 1), const2),
            mem_spec, mem_spec,
        ],
        out_specs=[
            pl.BlockSpec((1, B_HEADS, 1, LANES, tm), lambda b, i: (b, 0, i, 0, 0)),
            pl.BlockSpec((1, B_HEADS, tm, LANES), lambda b, i: (b, 0, i, 0)),
            pl.BlockSpec((1, B_HEADS, 1, VT_ROWS, tm), lambda b, i: (b, 0, i, 0, 0)),
            pl.BlockSpec((1, tm, A_WIDTH), lambda b, i: (b, i, 0)),
            pl.BlockSpec((1, tm, M_WIDTH), lambda b, i: (b, i, 0)),
        ],
        out_shape=[jax.ShapeDtypeStruct((batch, B_HEADS, nblk, LANES, tm), _BF16),
                   jax.ShapeDtypeStruct((batch, B_HEADS, seq, LANES), _BF16),
                   jax.ShapeDtypeStruct((batch, B_HEADS, nblk, VT_ROWS, tm), _BF16),
                   jax.ShapeDtypeStruct((batch, seq, A_WIDTH), _BF16),
                   jax.ShapeDtypeStruct((batch, seq, M_WIDTH), _BF16)],
        scratch_shapes=[pltpu.VMEM((F_ROWS, 1), _F32)],
        compiler_params=pltpu.CompilerParams(dimension_semantics=("parallel", "arbitrary"),
                                             vmem_limit_bytes=VMEM_LIMIT),
        name="mix_in",
    )(x, g_pre, w_nn, w_t, b_f_col, g_sgu, ws_cat, bs_exp, g_out_a, g_out_m_col, km, vm_t)


def _fox_out_kernel(qt_ref, ka_ref, vt_ref, ya_ref, ym_ref, x_ref, wout_ref, gob_ref, gpost_ref, x1_ref):
    t = T_ATT
    i = pl.program_id(1)
    causal = (lax.broadcasted_iota(jnp.int32, (t, t), 0) <= lax.broadcasted_iota(jnp.int32, (t, t), 1))

    w = t // Q_SPLIT
    chains = [(hd, qh) for hd in range(B_HEADS) for qh in range(Q_SPLIT)]

    def attend(j, carry, diag):
        start = j * t

        def qk(c):
            hd, qh = chains[c]
            nk = (qh + 1) * w if diag else t
            return _dot(ka_ref[0, hd, pl.ds(start, nk), :], qt_ref[0, hd, 0, :, qh * w:(qh + 1) * w])

        scores = {c: qk(c) for c in range(QK_AHEAD)}
        out = []
        pending = []

        def pv(item):
            hd, m, m_new, acc, p = item
            nk = p.shape[0]
            out.append((m_new, jnp.exp2(m - m_new) * acc + _dot(vt_ref[0, hd, j, :, 0:nk], p)))

        for c, (hd, qh) in enumerate(chains):
            m, acc = carry[c]
            s = scores.pop(c)
            nk = s.shape[0]
            if diag:
                s = jnp.where(causal[0:nk, qh * w:(qh + 1) * w], s, NEG_INF)
            m_new = jnp.maximum(m, jnp.max(s, axis=0, keepdims=True))
            p = jnp.exp2(s - m_new).astype(_BF16)
            if c + QK_AHEAD < len(chains):
                scores[c + QK_AHEAD] = qk(c + QK_AHEAD)
            pending.append((hd, m, m_new, acc, p))
            if len(pending) > PV_BEHIND:
                pv(pending.pop(0))
        while pending:
            pv(pending.pop(0))
        return tuple(out)

    def query_tile(n_off):
        z_am = _dot(ya_ref[0], wout_ref[0:A_WIDTH, :]) + _dot(ym_ref[0], wout_ref[A_WIDTH + B_WIDTH:, :])
        state = tuple((jnp.full((1, w), NEG_INF, _F32), jnp.zeros((VT_ROWS, w), _F32)) for _ in chains)
        for j in range(n_off):
            state = attend(j, state, False)
        state = attend(n_off, state, True)
        heads = []
        for hd in range(B_HEADS):
            acc = jnp.concatenate([state[hd * Q_SPLIT + qh][1] for qh in range(Q_SPLIT)], axis=1)
            heads.append(acc[0:HEAD_DIM] * (1.0 / acc[HEAD_DIM:HEAD_DIM + 1]))
        yb_t = jnp.concatenate(heads, axis=0)
        yb_t = yb_t * lax.rsqrt(jnp.mean(yb_t * yb_t, axis=0, keepdims=True) + EPS) * gob_ref[...]
        yb_n = yb_t.T.astype(_BF16)
        z = z_am + _dot(yb_n, wout_ref[A_WIDTH:A_WIDTH + B_WIDTH, :])
        x1_ref[0] = x_ref[0] + _rms(z, gpost_ref[...])

    for n_off in range(ka_ref.shape[2] // t):
        pl.when(i == n_off)(functools.partial(query_tile, n_off))


def _fox_out(qt, ka, vt, ya, ym, x, w_out, g_out_b_col, g_post):
    batch, seq, _ = x.shape
    t = T_ATT
    nblk = seq // t
    const2 = lambda b, i: (0, 0)
    return pl.pallas_call(
        _fox_out_kernel,
        grid=(batch, nblk),
        in_specs=[
            pl.BlockSpec((1, B_HEADS, 1, LANES, t), lambda b, i: (b, 0, i, 0, 0)),
            pl.BlockSpec((1, B_HEADS, seq, LANES), lambda b, i: (b, 0, 0, 0)),
            pl.BlockSpec((1, B_HEADS, nblk, VT_ROWS, t), lambda b, i: (b, 0, 0, 0, 0)),
            pl.BlockSpec((1, t, A_WIDTH), lambda b, i: (b, i, 0)),
            pl.BlockSpec((1, t, M_WIDTH), lambda b, i: (b, i, 0)),
            pl.BlockSpec((1, t, D_MODEL), lambda b, i: (b, i, 0)),
            pl.BlockSpec((D_MODEL, D_MODEL), const2),
            pl.BlockSpec((B_WIDTH, 1), const2),
            pl.BlockSpec((1, D_MODEL), const2),
        ],
        out_specs=pl.BlockSpec((1, t, D_MODEL), lambda b, i: (b, i, 0)),
        out_shape=jax.ShapeDtypeStruct(x.shape, x.dtype),
        compiler_params=pltpu.CompilerParams(dimension_semantics=("parallel", "parallel"),
                                             vmem_limit_bytes=VMEM_LIMIT),
        name="fox_out",
    )(qt, ka, vt, ya, ym, x, w_out, g_out_b_col, g_post)


def _ffn_kernel(x_ref, gpre_ref, wg_ref, wu_ref, wd_ref, gpost_ref, o_ref):
    step = D_FF // FF_SPLIT
    rows = x_ref.shape[0] // FF_ROW_SPLIT

    def pre_norm(r):
        x = x_ref[r * rows:(r + 1) * rows, :]
        return x, _rms(x, gpre_ref[...]).astype(_BF16)

    def finish(r, x, ff):
        o_ref[r * rows:(r + 1) * rows, :] = x + _rms(ff, gpost_ref[...])

    cur = pre_norm(0)
    done = None
    for r in range(FF_ROW_SPLIT):
        x, h = cur
        ff = None
        for c in range(FF_SPLIT):
            sl = slice(c * step, (c + 1) * step)
            g = _dot(h, wg_ref[:, sl])
            a = (g * (1.0 / (1.0 + jnp.exp(-g))) * _dot(h, wu_ref[:, sl])).astype(_BF16)
            part = _dot(a, wd_ref[sl, :])
            ff = part if ff is None else ff + part
            if c == FF_OVERLAP_AT:
                if done is not None:
                    finish(*done)
                if r + 1 < FF_ROW_SPLIT:
                    cur = pre_norm(r + 1)
        done = (r, x, ff)
    finish(*done)


def _ffn(x, g_pre, w_gate, w_up, w_down, g_post):
    tokens = x.shape[0]
    tm = TM_FFN
    const = lambda i: (0, 0)
    resident = functools.partial(pl.BlockSpec, index_map=const, pipeline_mode=pl.Buffered(1))
    return pl.pallas_call(
        _ffn_kernel,
        grid=(tokens // tm,),
        in_specs=[
            pl.BlockSpec((tm, D_MODEL), lambda i: (i, 0)),
            pl.BlockSpec((1, D_MODEL), const),
            resident((D_MODEL, D_FF)),
            resident((D_MODEL, D_FF)),
            resident((D_FF, D_MODEL)),
            pl.BlockSpec((1, D_MODEL), const),
        ],
        out_specs=pl.BlockSpec((tm, D_MODEL), lambda i: (i, 0)),
        out_shape=jax.ShapeDtypeStruct(x.shape, x.dtype),
        compiler_params=pltpu.CompilerParams(dimension_semantics=("parallel",),
                                             vmem_limit_bytes=VMEM_LIMIT),
        name="ffn",
    )(x, g_pre, w_gate, w_up, w_down, g_post)


def kernel(x, mem, g_pre_mix, w_in, b_f, g_sgu, w_s, b_s, g_out_a, g_out_b, g_out_m, g_mem, w_mem_kv, w_out,
           g_post_mix, g_pre_ffn, w_gate, w_up, w_down, g_post_ffn):
    batch, seq, d = x.shape
    depth = w_in.shape[0]
    row = lambda a: a.reshape(1, -1)
    q_lo = 2 * A_WIDTH
    f_lo = q_lo + 3 * B_WIDTH
    for l in range(depth):
        w = w_in[l]
        w_nn = jnp.concatenate([w[:, :q_lo], w[:, q_lo + B_WIDTH:q_lo + 2 * B_WIDTH]],
                               axis=1).astype(_BF16)
        w_f = jnp.pad(jnp.repeat(w[:, f_lo:f_lo + B_HEADS], SUBLANES, axis=1),
                      ((0, 0), (0, F_ROWS - SUBLANES * B_HEADS)))
        w_t = jnp.concatenate([w[:, q_lo:q_lo + B_WIDTH], w[:, q_lo + 2 * B_WIDTH:f_lo], w_f, w[:, f_lo + B_HEADS:]],
                              axis=1).T.astype(_BF16)
        b_f_col = jnp.pad(jnp.repeat(b_f[l], SUBLANES), (0, F_ROWS - SUBLANES * B_HEADS)).reshape(F_ROWS, 1)
        ws_cat = w_s[l].reshape(PAIRS, 2, CHUNK, CHUNK).transpose(0, 2, 1, 3).reshape(PAIRS, CHUNK, 2 * CHUNK)
        bs_exp = jnp.repeat(b_s[l].T, HEAD_DIM, axis=1)

        km, vm_t = _mem_kv(mem, row(g_mem[l]), w_mem_kv[l][:, :M_WIDTH].astype(_BF16),
                           w_mem_kv[l][:, M_WIDTH:].T.astype(_BF16))
        qt, ka, vt, ya, ym = _mix_in(x, row(g_pre_mix[l]), w_nn, w_t, b_f_col, row(g_sgu[l]), ws_cat, bs_exp,
                                     row(g_out_a[l]), g_out_m[l].reshape(-1, 1), km, vm_t)
        x = _fox_out(qt, ka, vt, ya, ym, x, w_out[l].astype(_BF16), g_out_b[l].reshape(-1, 1),
                     row(g_post_mix[l]))
        x = _ffn(x.reshape(batch * seq, d), row(g_pre_ffn[l]), w_gate[l].astype(_BF16), w_up[l].astype(_BF16),
                 w_down[l].astype(_BF16), row(g_post_ffn[l])).reshape(batch, seq, d)
    return x
```

```python
import functools

import jax
import jax.numpy as jnp
from jax import lax
from jax.experimental import pallas as pl
from jax.experimental.pallas import tpu as pltpu

D_MODEL = 1024
HEAD_DIM = 64
A_GROUPS = 6
B_HEADS = 6
M_HEADS = 4
A_WIDTH = A_GROUPS * HEAD_DIM
B_WIDTH = B_HEADS * HEAD_DIM
M_WIDTH = M_HEADS * HEAD_DIM
CHUNK = 128
MEM_TOKENS = 256
D_FF = 2816
EPS = 1e-6
NEG_INF = -1e30

LANES = 128
SUBLANES = 8
PAIRS = B_HEADS // 2
SCALE = HEAD_DIM ** -0.5
LOG2E = 1.4426950408889634

K_OFF = 2 * A_WIDTH
NN_COLS = K_OFF + B_WIDTH
V_ROW = B_WIDTH
F_ROW = 2 * B_WIDTH
F_ROWS = 64
QM_ROW = F_ROW + F_ROWS
NT_ROWS = QM_ROW + M_WIDTH
VT_ROWS = HEAD_DIM + 16

T_ATT = 512
MIX_TILES = 4
FOX_TILES = 2
Q_SPLIT = 2
QK_AHEAD = 4
TM_FFN = 1024
FF_ROW_SPLIT = 2
FF_SPLIT = 11
FF_OVERLAP_AT = 1

VMEM_LIMIT = 56 * 1024 * 1024

_F32 = jnp.float32
_BF16 = jnp.bfloat16


def _dot(a, b):
    return jnp.dot(a, b, preferred_element_type=_F32)


def _dot_nt(a, b):
    return lax.dot_general(a, b, (((1,), (1,)), ((), ())), preferred_element_type=_F32)


def _rms(x, g):
    return x * lax.rsqrt(jnp.mean(x * x, axis=-1, keepdims=True) + EPS) * g


def _gelu_tanh(x):
    return 0.5 * x * (1.0 + jnp.tanh(0.7978845608028654 * (x + 0.044715 * (x * x * x))))


def _log_sigmoid(x):
    return -(jnp.maximum(-x, 0.0) + jnp.log1p(jnp.exp(-jnp.abs(x))))


def _split3(x):
    hi = x.astype(_BF16).astype(_F32)
    r = x - hi
    mid = r.astype(_BF16).astype(_F32)
    lo = (r - mid).astype(_BF16).astype(_F32)
    return hi, mid, lo


def _mem_kv_kernel(mem_ref, g_ref, wk_ref, wvt_ref, km_ref, vmt_ref):
    mn = _rms(mem_ref[0], g_ref[...]).astype(_BF16)
    km = _dot(mn, wk_ref[...])
    vm_t = _dot_nt(wvt_ref[...], mn)
    head_k = lax.broadcasted_iota(jnp.int32, (MEM_TOKENS, M_WIDTH), 1) // HEAD_DIM
    head_v = lax.broadcasted_iota(jnp.int32, (M_WIDTH, MEM_TOKENS), 0) // HEAD_DIM
    for h in range(M_HEADS):
        km_ref[0, h] = jnp.where(head_k == h, km, 0.0).astype(_BF16)
        vmt_ref[0, h] = jnp.where(head_v == h, vm_t, 0.0).astype(_BF16)


def _mem_kv(mem, g_mem, w_k, w_v_t):
    batch = mem.shape[0]
    return pl.pallas_call(
        _mem_kv_kernel,
        grid=(batch,),
        in_specs=[
            pl.BlockSpec((1, MEM_TOKENS, D_MODEL), lambda b: (b, 0, 0)),
            pl.BlockSpec((1, D_MODEL), lambda b: (0, 0)),
            pl.BlockSpec((D_MODEL, M_WIDTH), lambda b: (0, 0)),
            pl.BlockSpec((M_WIDTH, D_MODEL), lambda b: (0, 0)),
        ],
        out_specs=[
            pl.BlockSpec((1, M_HEADS, MEM_TOKENS, M_WIDTH), lambda b: (b, 0, 0, 0)),
            pl.BlockSpec((1, M_HEADS, M_WIDTH, MEM_TOKENS), lambda b: (b, 0, 0, 0)),
        ],
        out_shape=[jax.ShapeDtypeStruct((batch, M_HEADS, MEM_TOKENS, M_WIDTH), _BF16),
                   jax.ShapeDtypeStruct((batch, M_HEADS, M_WIDTH, MEM_TOKENS), _BF16)],
        compiler_params=pltpu.CompilerParams(dimension_semantics=("parallel",)),
        name="mem_kv",
    )(mem, g_mem, w_k, w_v_t)


def _mix_in_kernel(x_ref, gpre_ref, wnn_ref, wt_ref, bf_ref, gsgu_ref, ws_ref, bs_ref, goa_ref, gom_ref,
                   km_ref, vmt_ref, qt_ref, ka_ref, vt_ref, ya_ref, ym_ref, carry_ref):
    tm = T_ATT

    @pl.when(pl.program_id(1) == 0)
    def _():
        carry_ref[...] = jnp.zeros_like(carry_ref)

    carry = {"c": carry_ref[...]}

    def tile_stages(local):
        rows = slice(local * tm, (local + 1) * tm)
        v = {}

        def st_h():
            v["h"] = _rms(x_ref[0, rows, :], gpre_ref[...]).astype(_BF16)

        def st_zu():
            v["t_zu"] = _dot(v["h"], wnn_ref[:, 0:A_WIDTH])

        def st_zv():
            v["t_zv"] = _dot(v["h"], wnn_ref[:, A_WIDTH:2 * A_WIDTH])

        def st_k():
            v["k"] = _dot(v["h"], wnn_ref[:, K_OFF:K_OFF + B_WIDTH])

        def st_qt():
            v["q_t"] = _dot_nt(wt_ref[0:B_WIDTH, :], v["h"]) * (SCALE * LOG2E)

        def st_vt():
            v["v_t"] = _dot_nt(wt_ref[V_ROW:V_ROW + B_WIDTH, :], v["h"])

        def st_f():
            v["t_f"] = _dot_nt(wt_ref[F_ROW:F_ROW + F_ROWS, :], v["h"])

        def st_qm():
            v["qm_t"] = (_dot_nt(wt_ref[QM_ROW:NT_ROWS, :], v["h"]) * (SCALE * LOG2E)).astype(_BF16)

        def st_u():
            v["u"] = _gelu_tanh(v["t_zu"])

        def st_vn():
            v["vn"] = _rms(_gelu_tanh(v["t_zv"]), gsgu_ref[...])

        def st_vt_store():
            ones_v = jnp.where(lax.broadcasted_iota(jnp.int32, (VT_ROWS - HEAD_DIM, tm), 0) == 0, 1.0, 0.0)
            for hd in range(B_HEADS):
                vt_ref[0, hd, local] = jnp.concatenate([v["v_t"][hd * HEAD_DIM:(hd + 1) * HEAD_DIM], ones_v],
                                                       axis=0).astype(_BF16)

        rowf = lax.broadcasted_iota(jnp.int32, (F_ROWS, tm), 0)

        def st_gate_parts():
            logf = _log_sigmoid(v["t_f"] + bf_ref[...])
            logf = jnp.where(rowf < SUBLANES * B_HEADS, logf, 0.0)
            v["parts"] = jnp.concatenate(_split3(logf), axis=0).astype(_BF16)

        def st_gate_dot():
            triu = (lax.broadcasted_iota(jnp.int32, (tm, tm), 0)
                    <= lax.broadcasted_iota(jnp.int32, (tm, tm), 1)).astype(_BF16)
            v["cs"] = _dot(v["parts"], triu)

        def st_gate_bias():
            cs = v["cs"]
            c = cs[0:F_ROWS] + cs[F_ROWS:2 * F_ROWS] + cs[2 * F_ROWS:] + carry["c"]
            carry["c"] = c[:, tm - 1:tm]
            c_hi, c_mid, c_lo = _split3(c * LOG2E)
            j8 = rowf % SUBLANES
            v["cq"] = jnp.where(j8 == 0, c_hi, jnp.where(j8 == 1, c_mid, jnp.where(j8 == 2, c_lo,
                                                                                     jnp.where(j8 < 6, 1.0, 0.0))))
            ck = jnp.where(j8 < 3, 1.0, jnp.where(j8 == 3, -c_hi, jnp.where(j8 == 4, -c_mid,
                                                                               jnp.where(j8 == 5, -c_lo, 0.0))))
            v["ck_t"] = jnp.concatenate([ck, jnp.zeros((LANES - F_ROWS, tm), _F32)], axis=0).T

        def st_sgu():
            lane = lax.broadcasted_iota(jnp.int32, (CHUNK, LANES), 1)
            row_s = lax.broadcasted_iota(jnp.int32, (CHUNK, 2 * CHUNK), 0)
            col_s = lax.broadcasted_iota(jnp.int32, (CHUNK, 2 * CHUNK), 1) % CHUNK
            z_pairs = []
            for p in range(PAIRS):
                w_pair = jnp.where(col_s <= row_s, ws_ref[p], 0.0).astype(_BF16)
                z_chunks = []
                for c_i in range(tm // CHUNK):
                    v_pair = v["vn"][c_i * CHUNK:(c_i + 1) * CHUNK, p * LANES:(p + 1) * LANES]
                    rhs = jnp.concatenate([jnp.where(lane < HEAD_DIM, v_pair, 0.0),
                                           jnp.where(lane >= HEAD_DIM, v_pair, 0.0)], axis=0).astype(_BF16)
                    z_chunks.append(_dot(w_pair, rhs) + bs_ref[:, p * LANES:(p + 1) * LANES])
                z_pairs.append(jnp.concatenate(z_chunks, axis=0))
            v["z"] = jnp.concatenate(z_pairs, axis=1)

        def st_ya():
            ya_ref[0, rows, :] = _rms(v["u"] * v["z"], goa_ref[...]).astype(_BF16)

        def st_q_tables():
            zeros_q = jnp.zeros((LANES - HEAD_DIM - SUBLANES, tm), _F32)
            for hd in range(B_HEADS):
                q_h = v["q_t"][hd * HEAD_DIM:(hd + 1) * HEAD_DIM]
                cq_h = v["cq"][SUBLANES * hd:SUBLANES * (hd + 1)]
                q_aug = [q_h, cq_h, zeros_q] if hd % 2 == 0 else [cq_h, zeros_q, q_h]
                qt_ref[0, hd, local] = jnp.concatenate(q_aug, axis=0).astype(_BF16)

        def st_k_tables(heads):
            lane_t = lax.broadcasted_iota(jnp.int32, (tm, LANES), 1)
            for hd in heads:
                k_pair = v["k"][:, (hd // 2) * LANES:(hd // 2 + 1) * LANES]
                if hd % 2 == 0:
                    k_aug = jnp.where(lane_t < HEAD_DIM, k_pair, pltpu.roll(v["ck_t"], HEAD_DIM - SUBLANES * hd, 1))
                else:
                    k_aug = jnp.where(lane_t >= HEAD_DIM, k_pair, pltpu.roll(v["ck_t"], LANES - SUBLANES * hd, 1))
                ka_ref[0, hd, rows, :] = k_aug.astype(_BF16)

        def st_mem_qk(hh):
            v["s_m", hh] = _dot(km_ref[0, hh], v["qm_t"])

        def st_mem_softmax(hh):
            s = v.pop(("s_m", hh))
            e = jnp.exp2(s - jnp.max(s, axis=0, keepdims=True))
            v["p_m", hh] = (e * (1.0 / jnp.sum(e, axis=0, keepdims=True))).astype(_BF16)

        def st_mem_pv(hh):
            part = _dot(vmt_ref[0, hh], v.pop(("p_m", hh)))
            v["o_t"] = part if hh == 0 else v["o_t"] + part

        def st_ym():
            o_t = v["o_t"]
            ym_t = o_t * lax.rsqrt(jnp.mean(o_t * o_t, axis=0, keepdims=True) + EPS) * gom_ref[...]
            ym_ref[0, rows, :] = ym_t.T.astype(_BF16)

        P = functools.partial
        return (st_h, st_zu, st_zv, st_qm, st_u, st_vt, st_f, st_vn, st_qt, st_vt_store,
                P(st_mem_qk, 0), P(st_mem_qk, 1), st_gate_parts, st_sgu, st_gate_dot, P(st_mem_softmax, 0),
                st_k, P(st_mem_qk, 2), st_ya, P(st_mem_pv, 0), P(st_mem_softmax, 1), st_gate_bias,
                P(st_mem_qk, 3), P(st_mem_pv, 1), P(st_mem_softmax, 2), st_q_tables, P(st_mem_pv, 2),
                P(st_mem_softmax, 3), P(st_k_tables, (0, 1, 2)), P(st_mem_pv, 3), P(st_k_tables, (3, 4, 5)),
                st_ym)

    for local in range(MIX_TILES):
        for stage in tile_stages(local):
            stage()
    carry_ref[...] = carry["c"]


def _mix_in(x, g_pre, w_nn, w_t, b_f_col, g_sgu, ws_cat, bs_exp, g_out_a, g_out_m_col, km, vm_t):
    batch, seq, _ = x.shape
    tm = T_ATT
    nblk = seq // tm
    rows = MIX_TILES * tm
    const2 = lambda b, i: (0, 0)
    mem_spec = pl.BlockSpec((1, M_HEADS, MEM_TOKENS, M_WIDTH), lambda b, i: (b, 0, 0, 0))
    return pl.pallas_call(
        _mix_in_kernel,
        grid=(batch, nblk // MIX_TILES),
        in_specs=[
            pl.BlockSpec((1, rows, D_MODEL), lambda b, i: (b, i, 0)),
            pl.BlockSpec((1, D_MODEL), const2),
            pl.BlockSpec((D_MODEL, NN_COLS), const2),
            pl.BlockSpec((NT_ROWS, D_MODEL), const2),
            pl.BlockSpec((F_ROWS, 1), const2),
            pl.BlockSpec((1, A_WIDTH), const2),
            pl.BlockSpec((PAIRS, CHUNK, 2 * CHUNK), lambda b, i: (0, 0, 0)),
            pl.BlockSpec((CHUNK, A_WIDTH), const2),
            pl.BlockSpec((1, A_WIDTH), const2),
            pl.BlockSpec((M_WIDTH, 1), const2),
            mem_spec, mem_spec,
        ],
        out_specs=[
            pl.BlockSpec((1, B_HEADS, MIX_TILES, LANES, tm), lambda b, i: (b, 0, i, 0, 0)),
            pl.BlockSpec((1, B_HEADS, rows, LANES), lambda b, i: (b, 0, i, 0)),
            pl.BlockSpec((1, B_HEADS, MIX_TILES, VT_ROWS, tm), lambda b, i: (b, 0, i, 0, 0)),
            pl.BlockSpec((1, rows, A_WIDTH), lambda b, i: (b, i, 0)),
            pl.BlockSpec((1, rows, M_WIDTH), lambda b, i: (b, i, 0)),
        ],
        out_shape=[jax.ShapeDtypeStruct((batch, B_HEADS, nblk, LANES, tm), _BF16),
                   jax.ShapeDtypeStruct((batch, B_HEADS, seq, LANES), _BF16),
                   jax.ShapeDtypeStruct((batch, B_HEADS, nblk, VT_ROWS, tm), _BF16),
                   jax.ShapeDtypeStruct((batch, seq, A_WIDTH), _BF16),
                   jax.ShapeDtypeStruct((batch, seq, M_WIDTH), _BF16)],
        scratch_shapes=[pltpu.VMEM((F_ROWS, 1), _F32)],
        compiler_params=pltpu.CompilerParams(dimension_semantics=("parallel", "arbitrary"),
                                             vmem_limit_bytes=VMEM_LIMIT),
        name="mix_in",
    )(x, g_pre, w_nn, w_t, b_f_col, g_sgu, ws_cat, bs_exp, g_out_a, g_out_m_col, km, vm_t)


def _fox_out_kernel(qt_ref, ka_ref, vt_ref, ya_ref, ym_ref, x_ref, wout_ref, gob_ref, gpost_ref, x1_ref):
    t = T_ATT
    causal = (lax.broadcasted_iota(jnp.int32, (t, t), 0) <= lax.broadcasted_iota(jnp.int32, (t, t), 1))

    w = t // Q_SPLIT
    chains = [(hd, qh) for hd in range(B_HEADS) for qh in range(Q_SPLIT)]

    def attend(local, j, carry, diag):
        start = j * t

        def qk(c):
            hd, qh = chains[c]
            nk = (qh + 1) * w if diag else t
            return _dot(ka_ref[0, hd, pl.ds(start, nk), :], qt_ref[0, hd, local, :, qh * w:(qh + 1) * w])

        scores = {c: qk(c) for c in range(QK_AHEAD)}
        out = []
        for c, (hd, qh) in enumerate(chains):
            m, acc = carry[c]
            s = scores.pop(c)
            nk = s.shape[0]
            if diag:
                s = jnp.where(causal[0:nk, qh * w:(qh + 1) * w], s, NEG_INF)
            m_new = jnp.maximum(m, jnp.max(s, axis=0, keepdims=True))
            p = jnp.exp2(s - m_new).astype(_BF16)
            if c + QK_AHEAD < len(chains):
                scores[c + QK_AHEAD] = qk(c + QK_AHEAD)
            out.append((m_new, jnp.exp2(m - m_new) * acc + _dot(vt_ref[0, hd, j, :, 0:nk], p)))
        return tuple(out)

    def query_tile(local, n_off):
        rows = slice(local * t, (local + 1) * t)
        z_am = (_dot(ya_ref[0, rows, :], wout_ref[0:A_WIDTH, :])
                + _dot(ym_ref[0, rows, :], wout_ref[A_WIDTH + B_WIDTH:, :]))
        state = tuple((jnp.full((1, w), NEG_INF, _F32), jnp.zeros((VT_ROWS, w), _F32)) for _ in chains)
        for j in range(n_off):
            state = attend(local, j, state, False)
        state = attend(local, n_off, state, True)
        heads = []
        for hd in range(B_HEADS):
            acc = jnp.concatenate([state[hd * Q_SPLIT + qh][1] for qh in range(Q_SPLIT)], axis=1)
            heads.append(acc[0:HEAD_DIM] * (1.0 / acc[HEAD_DIM:HEAD_DIM + 1]))
        yb_t = jnp.concatenate(heads, axis=0)
        yb_t = yb_t * lax.rsqrt(jnp.mean(yb_t * yb_t, axis=0, keepdims=True) + EPS) * gob_ref[...]
        yb_n = yb_t.T.astype(_BF16)
        z = z_am + _dot(yb_n, wout_ref[A_WIDTH:A_WIDTH + B_WIDTH, :])
        x1_ref[0, rows, :] = x_ref[0, rows, :] + _rms(z, gpost_ref[...])

    def step(first_tile):
        for local in range(FOX_TILES):
            query_tile(local, first_tile + local)

    for s in range(ka_ref.shape[2] // (t * FOX_TILES)):
        pl.when(pl.program_id(1) == s)(functools.partial(step, s * FOX_TILES))


def _fox_out(qt, ka, vt, ya, ym, x, w_out, g_out_b_col, g_post):
    batch, seq, _ = x.shape
    t = T_ATT
    nblk = seq // t
    rows = FOX_TILES * t
    const2 = lambda b, i: (0, 0)
    return pl.pallas_call(
        _fox_out_kernel,
        grid=(batch, nblk // FOX_TILES),
        in_specs=[
            pl.BlockSpec((1, B_HEADS, FOX_TILES, LANES, t), lambda b, i: (b, 0, i, 0, 0)),
            pl.BlockSpec((1, B_HEADS, seq, LANES), lambda b, i: (b, 0, 0, 0)),
            pl.BlockSpec((1, B_HEADS, nblk, VT_ROWS, t), lambda b, i: (b, 0, 0, 0, 0)),
            pl.BlockSpec((1, rows, A_WIDTH), lambda b, i: (b, i, 0)),
            pl.BlockSpec((1, rows, M_WIDTH), lambda b, i: (b, i, 0)),
            pl.BlockSpec((1, rows, D_MODEL), lambda b, i: (b, i, 0)),
            pl.BlockSpec((D_MODEL, D_MODEL), const2),
            pl.BlockSpec((B_WIDTH, 1), const2),
            pl.BlockSpec((1, D_MODEL), const2),
        ],
        out_specs=pl.BlockSpec((1, rows, D_MODEL), lambda b, i: (b, i, 0)),
        out_shape=jax.ShapeDtypeStruct(x.shape, x.dtype),
        compiler_params=pltpu.CompilerParams(dimension_semantics=("parallel", "parallel"),
                                             vmem_limit_bytes=VMEM_LIMIT),
        name="fox_out",
    )(qt, ka, vt, ya, ym, x, w_out, g_out_b_col, g_post)


def _ffn_kernel(x_ref, gpre_ref, wg_ref, wu_ref, wd_ref, gpost_ref, o_ref):
    step = D_FF // FF_SPLIT
    rows = x_ref.shape[0] // FF_ROW_SPLIT

    def pre_norm(r):
        x = x_ref[r * rows:(r + 1) * rows, :]
        return x, _rms(x, gpre_ref[...]).astype(_BF16)

    def finish(r, x, ff):
        o_ref[r * rows:(r + 1) * rows, :] = x + _rms(ff, gpost_ref[...])

    cur = pre_norm(0)
    done = None
    for r in range(FF_ROW_SPLIT):
        x, h = cur
        ff = None
        for c in range(FF_SPLIT):
            sl = slice(c * step, (c + 1) * step)
            g = _dot(h, wg_ref[:, sl])
            a = (g * (1.0 / (1.0 + jnp.exp(-g))) * _dot(h, wu_ref[:, sl])).astype(_BF16)
            part = _dot(a, wd_ref[sl, :])
            ff = part if ff is None else ff + part
            if c == FF_OVERLAP_AT:
                if done is not None:
                    finish(*done)
                if r + 1 < FF_ROW_SPLIT:
                    cur = pre_norm(r + 1)
        done = (r, x, ff)
    finish(*done)


def _ffn(x, g_pre, w_gate, w_up, w_down, g_post):
    tokens = x.shape[0]
    tm = TM_FFN
    const = lambda i: (0, 0)
    resident = functools.partial(pl.BlockSpec, index_map=const, pipeline_mode=pl.Buffered(1))
    return pl.pallas_call(
        _ffn_kernel,
        grid=(tokens // tm,),
        in_specs=[
            pl.BlockSpec((tm, D_MODEL), lambda i: (i, 0)),
            pl.BlockSpec((1, D_MODEL), const),
            resident((D_MODEL, D_FF)),
            resident((D_MODEL, D_FF)),
            resident((D_FF, D_MODEL)),
            pl.BlockSpec((1, D_MODEL), const),
        ],
        out_specs=pl.BlockSpec((tm, D_MODEL), lambda i: (i, 0)),
        out_shape=jax.ShapeDtypeStruct(x.shape, x.dtype),
        compiler_params=pltpu.CompilerParams(dimension_semantics=("parallel",),
                                             vmem_limit_bytes=VMEM_LIMIT),
        name="ffn",
    )(x, g_pre, w_gate, w_up, w_down, g_post)


def kernel(x, mem, g_pre_mix, w_in, b_f, g_sgu, w_s, b_s, g_out_a, g_out_b, g_out_m, g_mem, w_mem_kv, w_out,
           g_post_mix, g_pre_ffn, w_gate, w_up, w_down, g_post_ffn):
    batch, seq, d = x.shape
    depth = w_in.shape[0]
    row = lambda a: a.reshape(1, -1)
    q_lo = 2 * A_WIDTH
    f_lo = q_lo + 3 * B_WIDTH
    for l in range(depth):
        w = w_in[l]
        w_nn = jnp.concatenate([w[:, :q_lo], w[:, q_lo + B_WIDTH:q_lo + 2 * B_WIDTH]],
                               axis=1).astype(_BF16)
        w_f = jnp.pad(jnp.repeat(w[:, f_lo:f_lo + B_HEADS], SUBLANES, axis=1),
                      ((0, 0), (0, F_ROWS - SUBLANES * B_HEADS)))
        w_t = jnp.concatenate([w[:, q_lo:q_lo + B_WIDTH], w[:, q_lo + 2 * B_WIDTH:f_lo], w_f, w[:, f_lo + B_HEADS:]],
                              axis=1).T.astype(_BF16)
        b_f_col = jnp.pad(jnp.repeat(b_f[l], SUBLANES), (0, F_ROWS - SUBLANES * B_HEADS)).reshape(F_ROWS, 1)
        ws_cat = w_s[l].reshape(PAIRS, 2, CHUNK, CHUNK).transpose(0, 2, 1, 3).reshape(PAIRS, CHUNK, 2 * CHUNK)
        bs_exp = jnp.repeat(b_s[l].T, HEAD_DIM, axis=1)

        km, vm_t = _mem_kv(mem, row(g_mem[l]), w_mem_kv[l][:, :M_WIDTH].astype(_BF16),
                           w_mem_kv[l][:, M_WIDTH:].T.astype(_BF16))
        qt, ka, vt, ya, ym = _mix_in(x, row(g_pre_mix[l]), w_nn, w_t, b_f_col, row(g_sgu[l]), ws_cat, bs_exp,
                                     row(g_out_a[l]), g_out_m[l].reshape(-1, 1), km, vm_t)
        x = _fox_out(qt, ka, vt, ya, ym, x, w_out[l].astype(_BF16), g_out_b[l].reshape(-1, 1),
                     row(g_post_mix[l]))
        x = _ffn(x.reshape(batch * seq, d), row(g_pre_ffn[l]), w_gate[l].astype(_BF16), w_up[l].astype(_BF16),
                 w_down[l].astype(_BF16), row(g_post_ffn[l])).reshape(batch, seq, d)
    return x
```

```python
import functools

import jax
import jax.numpy as jnp
from jax import lax
from jax.experimental import pallas as pl
from jax.experimental.pallas import tpu as pltpu

D_MODEL = 1024
HEAD_DIM = 64
A_GROUPS = 6
B_HEADS = 6
M_HEADS = 4
A_WIDTH = A_GROUPS * HEAD_DIM
B_WIDTH = B_HEADS * HEAD_DIM
M_WIDTH = M_HEADS * HEAD_DIM
CHUNK = 128
MEM_TOKENS = 256
D_FF = 2816
EPS = 1e-6
NEG_INF = -1e30

LANES = 128
SUBLANES = 8
PAIRS = B_HEADS // 2
SCALE = HEAD_DIM ** -0.5
LOG2E = 1.4426950408889634

K_OFF = 2 * A_WIDTH
NN_COLS = K_OFF + B_WIDTH
V_ROW = B_WIDTH
F_ROW = 2 * B_WIDTH
F_ROWS = 64
QM_ROW = F_ROW + F_ROWS
NT_ROWS = QM_ROW + M_WIDTH
VT_ROWS = HEAD_DIM + 16

T_ATT = 512
MIX_TILES = 4
FOX_TILES = 2
Q_SPLIT = 2
QK_AHEAD = 4
TM_FFN = 2048
FF_ROW_SPLIT = 4
FF_SPLIT = 11
FF_OVERLAP_AT = 1

VMEM_LIMIT = 56 * 1024 * 1024

_F32 = jnp.float32
_BF16 = jnp.bfloat16


def _dot(a, b):
    return jnp.dot(a, b, preferred_element_type=_F32)


def _dot_nt(a, b):
    return lax.dot_general(a, b, (((1,), (1,)), ((), ())), preferred_element_type=_F32)


def _rms(x, g):
    return x * lax.rsqrt(jnp.mean(x * x, axis=-1, keepdims=True) + EPS) * g


def _gelu_tanh(x):
    return 0.5 * x * (1.0 + jnp.tanh(0.7978845608028654 * (x + 0.044715 * (x * x * x))))


def _log_sigmoid(x):
    return -(jnp.maximum(-x, 0.0) + jnp.log1p(jnp.exp(-jnp.abs(x))))


def _split3(x):
    hi = x.astype(_BF16).astype(_F32)
    r = x - hi
    mid = r.astype(_BF16).astype(_F32)
    lo = (r - mid).astype(_BF16).astype(_F32)
    return hi, mid, lo


def _mix_in_kernel(x_ref, gpre_ref, wnn_ref, wt_ref, bf_ref, gsgu_ref, ws_ref, bs_ref, goa_ref, gom_ref,
                   mem_ref, gmem_ref, wmk_ref, wmvt_ref, qt_ref, ka_ref, vt_ref, ya_ref, ym_ref,
                   carry_ref, km_ref, vmt_ref):
    tm = T_ATT

    mn = _rms(mem_ref[0], gmem_ref[...]).astype(_BF16)
    km = _dot(mn, wmk_ref[...])
    vm_t = _dot_nt(wmvt_ref[...], mn)
    head_k = lax.broadcasted_iota(jnp.int32, (MEM_TOKENS, M_WIDTH), 1) // HEAD_DIM
    head_v = lax.broadcasted_iota(jnp.int32, (M_WIDTH, MEM_TOKENS), 0) // HEAD_DIM
    for hh in range(M_HEADS):
        km_ref[hh] = jnp.where(head_k == hh, km, 0.0).astype(_BF16)
        vmt_ref[hh] = jnp.where(head_v == hh, vm_t, 0.0).astype(_BF16)

    @pl.when(pl.program_id(1) == 0)
    def _():
        carry_ref[...] = jnp.zeros_like(carry_ref)

    carry = {"c": carry_ref[...]}

    def tile_stages(local):
        rows = slice(local * tm, (local + 1) * tm)
        v = {}

        def st_h():
            v["h"] = _rms(x_ref[0, rows, :], gpre_ref[...]).astype(_BF16)

        def st_zu():
            v["t_zu"] = _dot(v["h"], wnn_ref[:, 0:A_WIDTH])

        def st_zv():
            v["t_zv"] = _dot(v["h"], wnn_ref[:, A_WIDTH:2 * A_WIDTH])

        def st_k():
            v["k"] = _dot(v["h"], wnn_ref[:, K_OFF:K_OFF + B_WIDTH])

        def st_qt():
            v["q_t"] = _dot_nt(wt_ref[0:B_WIDTH, :], v["h"]) * (SCALE * LOG2E)

        def st_vt():
            v["v_t"] = _dot_nt(wt_ref[V_ROW:V_ROW + B_WIDTH, :], v["h"])

        def st_f():
            v["t_f"] = _dot_nt(wt_ref[F_ROW:F_ROW + F_ROWS, :], v["h"])

        def st_qm():
            v["qm_t"] = (_dot_nt(wt_ref[QM_ROW:NT_ROWS, :], v["h"]) * (SCALE * LOG2E)).astype(_BF16)

        def st_u():
            v["u"] = _gelu_tanh(v["t_zu"])

        def st_vn():
            v["vn"] = _rms(_gelu_tanh(v["t_zv"]), gsgu_ref[...])

        def st_vt_store():
            ones_v = jnp.where(lax.broadcasted_iota(jnp.int32, (VT_ROWS - HEAD_DIM, tm), 0) == 0, 1.0, 0.0)
            for hd in range(B_HEADS):
                vt_ref[0, hd, local] = jnp.concatenate([v["v_t"][hd * HEAD_DIM:(hd + 1) * HEAD_DIM], ones_v],
                                                       axis=0).astype(_BF16)

        rowf = lax.broadcasted_iota(jnp.int32, (F_ROWS, tm), 0)

        def st_gate_parts():
            logf = _log_sigmoid(v["t_f"] + bf_ref[...])
            logf = jnp.where(rowf < SUBLANES * B_HEADS, logf, 0.0)
            v["parts"] = jnp.concatenate(_split3(logf), axis=0).astype(_BF16)

        def st_gate_dot():
            triu = (lax.broadcasted_iota(jnp.int32, (tm, tm), 0)
                    <= lax.broadcasted_iota(jnp.int32, (tm, tm), 1)).astype(_BF16)
            v["cs"] = _dot(v["parts"], triu)

        def st_gate_bias():
            cs = v["cs"]
            c = cs[0:F_ROWS] + cs[F_ROWS:2 * F_ROWS] + cs[2 * F_ROWS:] + carry["c"]
            carry["c"] = c[:, tm - 1:tm]
            c_hi, c_mid, c_lo = _split3(c * LOG2E)
            j8 = rowf % SUBLANES
            v["cq"] = jnp.where(j8 == 0, c_hi, jnp.where(j8 == 1, c_mid, jnp.where(j8 == 2, c_lo,
                                                                                     jnp.where(j8 < 6, 1.0, 0.0))))
            ck = jnp.where(j8 < 3, 1.0, jnp.where(j8 == 3, -c_hi, jnp.where(j8 == 4, -c_mid,
                                                                               jnp.where(j8 == 5, -c_lo, 0.0))))
            v["ck_t"] = jnp.concatenate([ck, jnp.zeros((LANES - F_ROWS, tm), _F32)], axis=0).T

        def st_sgu():
            lane = lax.broadcasted_iota(jnp.int32, (CHUNK, LANES), 1)
            row_s = lax.broadcasted_iota(jnp.int32, (CHUNK, 2 * CHUNK), 0)
            col_s = lax.broadcasted_iota(jnp.int32, (CHUNK, 2 * CHUNK), 1) % CHUNK
            z_pairs = []
            for p in range(PAIRS):
                w_pair = jnp.where(col_s <= row_s, ws_ref[p], 0.0).astype(_BF16)
                z_chunks = []
                for c_i in range(tm // CHUNK):
                    v_pair = v["vn"][c_i * CHUNK:(c_i + 1) * CHUNK, p * LANES:(p + 1) * LANES]
                    rhs = jnp.concatenate([jnp.where(lane < HEAD_DIM, v_pair, 0.0),
                                           jnp.where(lane >= HEAD_DIM, v_pair, 0.0)], axis=0).astype(_BF16)
                    z_chunks.append(_dot(w_pair, rhs) + bs_ref[:, p * LANES:(p + 1) * LANES])
                z_pairs.append(jnp.concatenate(z_chunks, axis=0))
            v["z"] = jnp.concatenate(z_pairs, axis=1)

        def st_ya():
            ya_ref[0, rows, :] = _rms(v["u"] * v["z"], goa_ref[...]).astype(_BF16)

        def st_q_tables():
            zeros_q = jnp.zeros((LANES - HEAD_DIM - SUBLANES, tm), _F32)
            for hd in range(B_HEADS):
                q_h = v["q_t"][hd * HEAD_DIM:(hd + 1) * HEAD_DIM]
                cq_h = v["cq"][SUBLANES * hd:SUBLANES * (hd + 1)]
                q_aug = [q_h, cq_h, zeros_q] if hd % 2 == 0 else [cq_h, zeros_q, q_h]
                qt_ref[0, hd, local] = jnp.concatenate(q_aug, axis=0).astype(_BF16)

        def st_k_tables(heads):
            lane_t = lax.broadcasted_iota(jnp.int32, (tm, LANES), 1)
            for hd in heads:
                k_pair = v["k"][:, (hd // 2) * LANES:(hd // 2 + 1) * LANES]
                if hd % 2 == 0:
                    k_aug = jnp.where(lane_t < HEAD_DIM, k_pair, pltpu.roll(v["ck_t"], HEAD_DIM - SUBLANES * hd, 1))
                else:
                    k_aug = jnp.where(lane_t >= HEAD_DIM, k_pair, pltpu.roll(v["ck_t"], LANES - SUBLANES * hd, 1))
                ka_ref[0, hd, rows, :] = k_aug.astype(_BF16)

        def st_mem_qk(hh):
            v["s_m", hh] = _dot(km_ref[hh], v["qm_t"])

        def st_mem_softmax(hh):
            s = v.pop(("s_m", hh))
            e = jnp.exp2(s - jnp.max(s, axis=0, keepdims=True))
            v["p_m", hh] = (e * (1.0 / jnp.sum(e, axis=0, keepdims=True))).astype(_BF16)

        def st_mem_pv(hh):
            part = _dot(vmt_ref[hh], v.pop(("p_m", hh)))
            v["o_t"] = part if hh == 0 else v["o_t"] + part

        def st_ym():
            o_t = v["o_t"]
            ym_t = o_t * lax.rsqrt(jnp.mean(o_t * o_t, axis=0, keepdims=True) + EPS) * gom_ref[...]
            ym_ref[0, rows, :] = ym_t.T.astype(_BF16)

        P = functools.partial
        return (st_h, st_zu, st_zv, st_qm, st_u, st_vt, st_f, st_vn, st_qt, st_vt_store,
                P(st_mem_qk, 0), P(st_mem_qk, 1), st_gate_parts, st_sgu, st_gate_dot, P(st_mem_softmax, 0),
                st_k, P(st_mem_qk, 2), st_ya, P(st_mem_pv, 0), P(st_mem_softmax, 1), st_gate_bias,
                P(st_mem_qk, 3), P(st_mem_pv, 1), P(st_mem_softmax, 2), st_q_tables, P(st_mem_pv, 2),
                P(st_mem_softmax, 3), P(st_k_tables, (0, 1, 2)), P(st_mem_pv, 3), P(st_k_tables, (3, 4, 5)),
                st_ym)

    for local in range(MIX_TILES):
        for stage in tile_stages(local):
            stage()
    carry_ref[...] = carry["c"]


def _mix_in(x, g_pre, w_nn, w_t, b_f_col, g_sgu, ws_cat, bs_exp, g_out_a, g_out_m_col, mem, g_mem, w_mk, w_mv_t):
    batch, seq, _ = x.shape
    tm = T_ATT
    nblk = seq // tm
    rows = MIX_TILES * tm
    const2 = lambda b, i: (0, 0)
    return pl.pallas_call(
        _mix_in_kernel,
        grid=(batch, nblk // MIX_TILES),
        in_specs=[
            pl.BlockSpec((1, rows, D_MODEL), lambda b, i: (b, i, 0)),
            pl.BlockSpec((1, D_MODEL), const2),
            pl.BlockSpec((D_MODEL, NN_COLS), const2),
            pl.BlockSpec((NT_ROWS, D_MODEL), const2),
            pl.BlockSpec((F_ROWS, 1), const2),
            pl.BlockSpec((1, A_WIDTH), const2),
            pl.BlockSpec((PAIRS, CHUNK, 2 * CHUNK), lambda b, i: (0, 0, 0)),
            pl.BlockSpec((CHUNK, A_WIDTH), const2),
            pl.BlockSpec((1, A_WIDTH), const2),
            pl.BlockSpec((M_WIDTH, 1), const2),
            pl.BlockSpec((1, MEM_TOKENS, D_MODEL), lambda b, i: (b, 0, 0)),
            pl.BlockSpec((1, D_MODEL), const2),
            pl.BlockSpec((D_MODEL, M_WIDTH), const2),
            pl.BlockSpec((M_WIDTH, D_MODEL), const2),
        ],
        out_specs=[
            pl.BlockSpec((1, B_HEADS, MIX_TILES, LANES, tm), lambda b, i: (b, 0, i, 0, 0)),
            pl.BlockSpec((1, B_HEADS, rows, LANES), lambda b, i: (b, 0, i, 0)),
            pl.BlockSpec((1, B_HEADS, MIX_TILES, VT_ROWS, tm), lambda b, i: (b, 0, i, 0, 0)),
            pl.BlockSpec((1, rows, A_WIDTH), lambda b, i: (b, i, 0)),
            pl.BlockSpec((1, rows, M_WIDTH), lambda b, i: (b, i, 0)),
        ],
        out_shape=[jax.ShapeDtypeStruct((batch, B_HEADS, nblk, LANES, tm), _BF16),
                   jax.ShapeDtypeStruct((batch, B_HEADS, seq, LANES), _BF16),
                   jax.ShapeDtypeStruct((batch, B_HEADS, nblk, VT_ROWS, tm), _BF16),
                   jax.ShapeDtypeStruct((batch, seq, A_WIDTH), _BF16),
                   jax.ShapeDtypeStruct((batch, seq, M_WIDTH), _BF16)],
        scratch_shapes=[pltpu.VMEM((F_ROWS, 1), _F32),
                        pltpu.VMEM((M_HEADS, MEM_TOKENS, M_WIDTH), _BF16),
                        pltpu.VMEM((M_HEADS, M_WIDTH, MEM_TOKENS), _BF16)],
        compiler_params=pltpu.CompilerParams(dimension_semantics=("parallel", "arbitrary"),
                                             vmem_limit_bytes=VMEM_LIMIT),
        name="mix_in",
    )(x, g_pre, w_nn, w_t, b_f_col, g_sgu, ws_cat, bs_exp, g_out_a, g_out_m_col, mem, g_mem, w_mk, w_mv_t)


def _fox_out_kernel(qt_ref, ka_ref, vt_ref, ya_ref, ym_ref, x_ref, wout_ref, gob_ref, gpost_ref, x1_ref):
    t = T_ATT
    causal = (lax.broadcasted_iota(jnp.int32, (t, t), 0) <= lax.broadcasted_iota(jnp.int32, (t, t), 1))

    w = t // Q_SPLIT
    chains = [(hd, qh) for hd in range(B_HEADS) for qh in range(Q_SPLIT)]

    def attend(local, j, carry, diag):
        start = j * t

        def qk(c):
            hd, qh = chains[c]
            nk = (qh + 1) * w if diag else t
            return _dot(ka_ref[0, hd, pl.ds(start, nk), :], qt_ref[0, hd, local, :, qh * w:(qh + 1) * w])

        scores = {c: qk(c) for c in range(QK_AHEAD)}
        out = []
        for c, (hd, qh) in enumerate(chains):
            m, acc = carry[c]
            s = scores.pop(c)
            nk = s.shape[0]
            if diag:
                s = jnp.where(causal[0:nk, qh * w:(qh + 1) * w], s, NEG_INF)
            m_new = jnp.maximum(m, jnp.max(s, axis=0, keepdims=True))
            p = jnp.exp2(s - m_new).astype(_BF16)
            if c + QK_AHEAD < len(chains):
                scores[c + QK_AHEAD] = qk(c + QK_AHEAD)
            out.append((m_new, jnp.exp2(m - m_new) * acc + _dot(vt_ref[0, hd, j, :, 0:nk], p)))
        return tuple(out)

    def query_tile(local, n_off):
        rows = slice(local * t, (local + 1) * t)
        z_am = (_dot(ya_ref[0, rows, :], wout_ref[0:A_WIDTH, :])
                + _dot(ym_ref[0, rows, :], wout_ref[A_WIDTH + B_WIDTH:, :]))
        state = tuple((jnp.full((1, w), NEG_INF, _F32), jnp.zeros((VT_ROWS, w), _F32)) for _ in chains)
        for j in range(n_off):
            state = attend(local, j, state, False)
        state = attend(local, n_off, state, True)
        heads = []
        for hd in range(B_HEADS):
            acc = jnp.concatenate([state[hd * Q_SPLIT + qh][1] for qh in range(Q_SPLIT)], axis=1)
            heads.append(acc[0:HEAD_DIM] * (1.0 / acc[HEAD_DIM:HEAD_DIM + 1]))
        yb_t = jnp.concatenate(heads, axis=0)
        yb_t = yb_t * lax.rsqrt(jnp.mean(yb_t * yb_t, axis=0, keepdims=True) + EPS) * gob_ref[...]
        yb_n = yb_t.T.astype(_BF16)
        z = z_am + _dot(yb_n, wout_ref[A_WIDTH:A_WIDTH + B_WIDTH, :])
        x1_ref[0, rows, :] = x_ref[0, rows, :] + _rms(z, gpost_ref[...])

    def step(first_tile):
        for local in range(FOX_TILES):
            query_tile(local, first_tile + local)

    for s in range(ka_ref.shape[2] // (t * FOX_TILES)):
        pl.when(pl.program_id(1) == s)(functools.partial(step, s * FOX_TILES))


def _fox_out(qt, ka, vt, ya, ym, x, w_out, g_out_b_col, g_post):
    batch, seq, _ = x.shape
    t = T_ATT
    nblk = seq // t
    rows = FOX_TILES * t
    const2 = lambda b, i: (0, 0)
    return pl.pallas_call(
        _fox_out_kernel,
        grid=(batch, nblk // FOX_TILES),
        in_specs=[
            pl.BlockSpec((1, B_HEADS, FOX_TILES, LANES, t), lambda b, i: (b, 0, i, 0, 0)),
            pl.BlockSpec((1, B_HEADS, seq, LANES), lambda b, i: (b, 0, 0, 0)),
            pl.BlockSpec((1, B_HEADS, nblk, VT_ROWS, t), lambda b, i: (b, 0, 0, 0, 0)),
            pl.BlockSpec((1, rows, A_WIDTH), lambda b, i: (b, i, 0)),
            pl.BlockSpec((1, rows, M_WIDTH), lambda b, i: (b, i, 0)),
            pl.BlockSpec((1, rows, D_MODEL), lambda b, i: (b, i, 0)),
            pl.BlockSpec((D_MODEL, D_MODEL), const2),
            pl.BlockSpec((B_WIDTH, 1), const2),
            pl.BlockSpec((1, D_MODEL), const2),
        ],
        out_specs=pl.BlockSpec((1, rows, D_MODEL), lambda b, i: (b, i, 0)),
        out_shape=jax.ShapeDtypeStruct(x.shape, x.dtype),
        compiler_params=pltpu.CompilerParams(dimension_semantics=("parallel", "parallel"),
                                             vmem_limit_bytes=VMEM_LIMIT),
        name="fox_out",
    )(qt, ka, vt, ya, ym, x, w_out, g_out_b_col, g_post)


def _ffn_kernel(x_ref, gpre_ref, wg_ref, wu_ref, wd_ref, gpost_ref, o_ref):
    step = D_FF // FF_SPLIT
    rows = x_ref.shape[0] // FF_ROW_SPLIT

    def pre_norm(r):
        x = x_ref[r * rows:(r + 1) * rows, :]
        return x, _rms(x, gpre_ref[...]).astype(_BF16)

    def finish(r, x, ff):
        o_ref[r * rows:(r + 1) * rows, :] = x + _rms(ff, gpost_ref[...])

    cur = pre_norm(0)
    done = None
    for r in range(FF_ROW_SPLIT):
        x, h = cur
        ff = None
        for c in range(FF_SPLIT):
            sl = slice(c * step, (c + 1) * step)
            g = _dot(h, wg_ref[:, sl])
            a = (g * (1.0 / (1.0 + jnp.exp(-g))) * _dot(h, wu_ref[:, sl])).astype(_BF16)
            part = _dot(a, wd_ref[sl, :])
            ff = part if ff is None else ff + part
            if c == FF_OVERLAP_AT:
                if done is not None:
                    finish(*done)
                if r + 1 < FF_ROW_SPLIT:
                    cur = pre_norm(r + 1)
        done = (r, x, ff)
    finish(*done)


def _ffn(x, g_pre, w_gate, w_up, w_down, g_post):
    tokens = x.shape[0]
    tm = TM_FFN
    const = lambda i: (0, 0)
    resident = functools.partial(pl.BlockSpec, index_map=const, pipeline_mode=pl.Buffered(1))
    return pl.pallas_call(
        _ffn_kernel,
        grid=(tokens // tm,),
        in_specs=[
            pl.BlockSpec((tm, D_MODEL), lambda i: (i, 0)),
            pl.BlockSpec((1, D_MODEL), const),
            resident((D_MODEL, D_FF)),
            resident((D_MODEL, D_FF)),
            resident((D_FF, D_MODEL)),
            pl.BlockSpec((1, D_MODEL), const),
        ],
        out_specs=pl.BlockSpec((tm, D_MODEL), lambda i: (i, 0)),
        out_shape=jax.ShapeDtypeStruct(x.shape, x.dtype),
        compiler_params=pltpu.CompilerParams(dimension_semantics=("parallel",),
                                             vmem_limit_bytes=VMEM_LIMIT),
        name="ffn",
    )(x, g_pre, w_gate, w_up, w_down, g_post)


def kernel(x, mem, g_pre_mix, w_in, b_f, g_sgu, w_s, b_s, g_out_a, g_out_b, g_out_m, g_mem, w_mem_kv, w_out,
           g_post_mix, g_pre_ffn, w_gate, w_up, w_down, g_post_ffn):
    batch, seq, d = x.shape
    depth = w_in.shape[0]
    row = lambda a: a.reshape(1, -1)
    q_lo = 2 * A_WIDTH
    f_lo = q_lo + 3 * B_WIDTH
    for l in range(depth):
        w = w_in[l]
        w_nn = jnp.concatenate([w[:, :q_lo], w[:, q_lo + B_WIDTH:q_lo + 2 * B_WIDTH]],
                               axis=1).astype(_BF16)
        w_f = jnp.pad(jnp.repeat(w[:, f_lo:f_lo + B_HEADS], SUBLANES, axis=1),
                      ((0, 0), (0, F_ROWS - SUBLANES * B_HEADS)))
        w_t = jnp.concatenate([w[:, q_lo:q_lo + B_WIDTH], w[:, q_lo + 2 * B_WIDTH:f_lo], w_f, w[:, f_lo + B_HEADS:]],
                              axis=1).T.astype(_BF16)
        b_f_col = jnp.pad(jnp.repeat(b_f[l], SUBLANES), (0, F_ROWS - SUBLANES * B_HEADS)).reshape(F_ROWS, 1)
        ws_cat = w_s[l].reshape(PAIRS, 2, CHUNK, CHUNK).transpose(0, 2, 1, 3).reshape(PAIRS, CHUNK, 2 * CHUNK)
        bs_exp = jnp.repeat(b_s[l].T, HEAD_DIM, axis=1)

        qt, ka, vt, ya, ym = _mix_in(x, row(g_pre_mix[l]), w_nn, w_t, b_f_col, row(g_sgu[l]), ws_cat, bs_exp,
                                     row(g_out_a[l]), g_out_m[l].reshape(-1, 1), mem, row(g_mem[l]),
                                     w_mem_kv[l][:, :M_WIDTH].astype(_BF16), w_mem_kv[l][:, M_WIDTH:].T.astype(_BF16))
        x = _fox_out(qt, ka, vt, ya, ym, x, w_out[l].astype(_BF16), g_out_b[l].reshape(-1, 1),
                     row(g_post_mix[l]))
        x = _ffn(x.reshape(batch * seq, d), row(g_pre_ffn[l]), w_gate[l].astype(_BF16), w_up[l].astype(_BF16),
                 w_down[l].astype(_BF16), row(g_post_ffn[l])).reshape(batch, seq, d)
    return x
```

```python
import functools

import jax
import jax.numpy as jnp
from jax import lax
from jax.experimental import pallas as pl
from jax.experimental.pallas import tpu as pltpu

D_MODEL = 1024
HEAD_DIM = 64
A_GROUPS = 6
B_HEADS = 6
M_HEADS = 4
A_WIDTH = A_GROUPS * HEAD_DIM
B_WIDTH = B_HEADS * HEAD_DIM
M_WIDTH = M_HEADS * HEAD_DIM
CHUNK = 128
MEM_TOKENS = 256
D_FF = 2816
EPS = 1e-6
NEG_INF = -1e30

LANES = 128
SUBLANES = 8
PAIRS = B_HEADS // 2
SCALE = HEAD_DIM ** -0.5
LOG2E = 1.4426950408889634

K_OFF = 2 * A_WIDTH
NN_COLS = K_OFF + B_WIDTH
V_ROW = B_WIDTH
F_ROW = 2 * B_WIDTH
F_ROWS = 64
QM_ROW = F_ROW + F_ROWS
NT_ROWS = QM_ROW + M_WIDTH
VT_ROWS = HEAD_DIM + 16

T_ATT = 512
MIX_TILES = 4
FOX_TILES = 2
Q_SPLIT = 2
QK_AHEAD = 4
TM_FFN = 1024
FF_ROW_SPLIT = 2
FF_SPLIT = 11
FF_OVERLAP_AT = 1

VMEM_LIMIT = 56 * 1024 * 1024

_F32 = jnp.float32
_BF16 = jnp.bfloat16


def _dot(a, b):
    return jnp.dot(a, b, preferred_element_type=_F32)


def _dot_nt(a, b):
    return lax.dot_general(a, b, (((1,), (1,)), ((), ())), preferred_element_type=_F32)


def _rms(x, g):
    return x * lax.rsqrt(jnp.mean(x * x, axis=-1, keepdims=True) + EPS) * g


def _gelu_tanh(x):
    return 0.5 * x * (1.0 + jnp.tanh(0.7978845608028654 * (x + 0.044715 * (x * x * x))))


def _log_sigmoid(x):
    return -(jnp.maximum(-x, 0.0) + jnp.log1p(jnp.exp(-jnp.abs(x))))


def _split3(x):
    hi = x.astype(_BF16).astype(_F32)
    r = x - hi
    mid = r.astype(_BF16).astype(_F32)
    lo = (r - mid).astype(_BF16).astype(_F32)
    return hi, mid, lo


def _mix_in_kernel(x_ref, gpre_ref, wnn_ref, wt_ref, bf_ref, gsgu_ref, ws_ref, bs_ref, goa_ref, gom_ref,
                   mem_ref, gmem_ref, wmk_ref, wmvt_ref, qt_ref, ka_ref, vt_ref, ya_ref, ym_ref,
                   carry_ref, km_ref, vmt_ref):
    tm = T_ATT

    mn = _rms(mem_ref[0], gmem_ref[...]).astype(_BF16)
    km = _dot(mn, wmk_ref[...])
    vm_t = _dot_nt(wmvt_ref[...], mn)
    head_k = lax.broadcasted_iota(jnp.int32, (MEM_TOKENS, M_WIDTH), 1) // HEAD_DIM
    head_v = lax.broadcasted_iota(jnp.int32, (M_WIDTH, MEM_TOKENS), 0) // HEAD_DIM
    for hh in range(M_HEADS):
        km_ref[hh] = jnp.where(head_k == hh, km, 0.0).astype(_BF16)
        vmt_ref[hh] = jnp.where(head_v == hh, vm_t, 0.0).astype(_BF16)

    @pl.when(pl.program_id(1) == 0)
    def _():
        carry_ref[...] = jnp.zeros_like(carry_ref)

    carry = {"c": carry_ref[...]}

    def tile_stages(local):
        rows = slice(local * tm, (local + 1) * tm)
        v = {}

        def st_h():
            v["h"] = _rms(x_ref[0, rows, :], gpre_ref[...]).astype(_BF16)

        def st_zu():
            v["t_zu"] = _dot(v["h"], wnn_ref[:, 0:A_WIDTH])

        def st_zv():
            v["t_zv"] = _dot(v["h"], wnn_ref[:, A_WIDTH:2 * A_WIDTH])

        def st_k():
            v["k"] = _dot(v["h"], wnn_ref[:, K_OFF:K_OFF + B_WIDTH])

        def st_qt():
            v["q_t"] = _dot_nt(wt_ref[0:B_WIDTH, :], v["h"]) * (SCALE * LOG2E)

        def st_vt():
            v["v_t"] = _dot_nt(wt_ref[V_ROW:V_ROW + B_WIDTH, :], v["h"])

        def st_f():
            v["t_f"] = _dot_nt(wt_ref[F_ROW:F_ROW + F_ROWS, :], v["h"])

        def st_qm():
            v["qm_t"] = (_dot_nt(wt_ref[QM_ROW:NT_ROWS, :], v["h"]) * (SCALE * LOG2E)).astype(_BF16)

        def st_u():
            v["u"] = _gelu_tanh(v["t_zu"])

        def st_vn():
            v["vn"] = _rms(_gelu_tanh(v["t_zv"]), gsgu_ref[...])

        def st_vt_store():
            ones_v = jnp.where(lax.broadcasted_iota(jnp.int32, (VT_ROWS - HEAD_DIM, tm), 0) == 0, 1.0, 0.0)
            for hd in range(B_HEADS):
                vt_ref[0, hd, local] = jnp.concatenate([v["v_t"][hd * HEAD_DIM:(hd + 1) * HEAD_DIM], ones_v],
                                                       axis=0).astype(_BF16)

        rowf = lax.broadcasted_iota(jnp.int32, (F_ROWS, tm), 0)

        def st_gate_parts():
            logf = _log_sigmoid(v["t_f"] + bf_ref[...])
            logf = jnp.where(rowf < SUBLANES * B_HEADS, logf, 0.0)
            v["parts"] = jnp.concatenate(_split3(logf), axis=0).astype(_BF16)

        def st_gate_dot():
            triu = (lax.broadcasted_iota(jnp.int32, (tm, tm), 0)
                    <= lax.broadcasted_iota(jnp.int32, (tm, tm), 1)).astype(_BF16)
            v["cs"] = _dot(v["parts"], triu)

        def st_gate_bias():
            cs = v["cs"]
            c = cs[0:F_ROWS] + cs[F_ROWS:2 * F_ROWS] + cs[2 * F_ROWS:] + carry["c"]
            carry["c"] = c[:, tm - 1:tm]
            c_hi, c_mid, c_lo = _split3(c * LOG2E)
            j8 = rowf % SUBLANES
            v["cq"] = jnp.where(j8 == 0, c_hi, jnp.where(j8 == 1, c_mid, jnp.where(j8 == 2, c_lo,
                                                                                     jnp.where(j8 < 6, 1.0, 0.0))))
            ck = jnp.where(j8 < 3, 1.0, jnp.where(j8 == 3, -c_hi, jnp.where(j8 == 4, -c_mid,
                                                                               jnp.where(j8 == 5, -c_lo, 0.0))))
            v["ck_t"] = jnp.concatenate([ck, jnp.zeros((LANES - F_ROWS, tm), _F32)], axis=0).T

        def st_sgu():
            lane = lax.broadcasted_iota(jnp.int32, (CHUNK, LANES), 1)
            row_s = lax.broadcasted_iota(jnp.int32, (CHUNK, 2 * CHUNK), 0)
            col_s = lax.broadcasted_iota(jnp.int32, (CHUNK, 2 * CHUNK), 1) % CHUNK
            z_pairs = []
            for p in range(PAIRS):
                w_pair = jnp.where(col_s <= row_s, ws_ref[p], 0.0).astype(_BF16)
                z_chunks = []
                for c_i in range(tm // CHUNK):
                    v_pair = v["vn"][c_i * CHUNK:(c_i + 1) * CHUNK, p * LANES:(p + 1) * LANES]
                    rhs = jnp.concatenate([jnp.where(lane < HEAD_DIM, v_pair, 0.0),
                                           jnp.where(lane >= HEAD_DIM, v_pair, 0.0)], axis=0).astype(_BF16)
                    z_chunks.append(_dot(w_pair, rhs) + bs_ref[:, p * LANES:(p + 1) * LANES])
                z_pairs.append(jnp.concatenate(z_chunks, axis=0))
            v["z"] = jnp.concatenate(z_pairs, axis=1)

        def st_ya():
            ya_ref[0, rows, :] = _rms(v["u"] * v["z"], goa_ref[...]).astype(_BF16)

        def st_q_tables():
            zeros_q = jnp.zeros((LANES - HEAD_DIM - SUBLANES, tm), _F32)
            for hd in range(B_HEADS):
                q_h = v["q_t"][hd * HEAD_DIM:(hd + 1) * HEAD_DIM]
                cq_h = v["cq"][SUBLANES * hd:SUBLANES * (hd + 1)]
                q_aug = [q_h, cq_h, zeros_q] if hd % 2 == 0 else [cq_h, zeros_q, q_h]
                qt_ref[0, hd, local] = jnp.concatenate(q_aug, axis=0).astype(_BF16)

        def st_k_tables(heads):
            lane_t = lax.broadcasted_iota(jnp.int32, (tm, LANES), 1)
            for hd in heads:
                k_pair = v["k"][:, (hd // 2) * LANES:(hd // 2 + 1) * LANES]
                if hd % 2 == 0:
                    k_aug = jnp.where(lane_t < HEAD_DIM, k_pair, pltpu.roll(v["ck_t"], HEAD_DIM - SUBLANES * hd, 1))
                else:
                    k_aug = jnp.where(lane_t >= HEAD_DIM, k_pair, pltpu.roll(v["ck_t"], LANES - SUBLANES * hd, 1))
                ka_ref[0, hd, rows, :] = k_aug.astype(_BF16)

        def st_mem_qk(hh):
            v["s_m", hh] = _dot(km_ref[hh], v["qm_t"])

        def st_mem_softmax(hh):
            s = v.pop(("s_m", hh))
            e = jnp.exp2(s - jnp.max(s, axis=0, keepdims=True))
            v["p_m", hh] = (e * (1.0 / jnp.sum(e, axis=0, keepdims=True))).astype(_BF16)

        def st_mem_pv(hh):
            part = _dot(vmt_ref[hh], v.pop(("p_m", hh)))
            v["o_t"] = part if hh == 0 else v["o_t"] + part

        def st_ym():
            o_t = v["o_t"]
            ym_t = o_t * lax.rsqrt(jnp.mean(o_t * o_t, axis=0, keepdims=True) + EPS) * gom_ref[...]
            ym_ref[0, rows, :] = ym_t.T.astype(_BF16)

        P = functools.partial
        return (st_h, st_zu, st_zv, st_qm, st_u, st_vt, st_f, st_vn, st_qt, st_vt_store,
                P(st_mem_qk, 0), P(st_mem_qk, 1), st_gate_parts, st_sgu, st_gate_dot, P(st_mem_softmax, 0),
                st_k, P(st_mem_qk, 2), st_ya, P(st_mem_pv, 0), P(st_mem_softmax, 1), st_gate_bias,
                P(st_mem_qk, 3), P(st_mem_pv, 1), P(st_mem_softmax, 2), st_q_tables, P(st_mem_pv, 2),
                P(st_mem_softmax, 3), P(st_k_tables, (0, 1, 2)), P(st_mem_pv, 3), P(st_k_tables, (3, 4, 5)),
                st_ym)

    for local in range(MIX_TILES):
        for stage in tile_stages(local):
            stage()
    carry_ref[...] = carry["c"]


def _mix_in(x, g_pre, w_nn, w_t, b_f_col, g_sgu, ws_cat, bs_exp, g_out_a, g_out_m_col, mem, g_mem, w_mk, w_mv_t):
    batch, seq, _ = x.shape
    tm = T_ATT
    nblk = seq // tm
    rows = MIX_TILES * tm
    const2 = lambda b, i: (0, 0)
    return pl.pallas_call(
        _mix_in_kernel,
        grid=(batch, nblk // MIX_TILES),
        in_specs=[
            pl.BlockSpec((1, rows, D_MODEL), lambda b, i: (b, i, 0)),
            pl.BlockSpec((1, D_MODEL), const2),
            pl.BlockSpec((D_MODEL, NN_COLS), const2),
            pl.BlockSpec((NT_ROWS, D_MODEL), const2),
            pl.BlockSpec((F_ROWS, 1), const2),
            pl.BlockSpec((1, A_WIDTH), const2),
            pl.BlockSpec((PAIRS, CHUNK, 2 * CHUNK), lambda b, i: (0, 0, 0)),
            pl.BlockSpec((CHUNK, A_WIDTH), const2),
            pl.BlockSpec((1, A_WIDTH), const2),
            pl.BlockSpec((M_WIDTH, 1), const2),
            pl.BlockSpec((1, MEM_TOKENS, D_MODEL), lambda b, i: (b, 0, 0)),
            pl.BlockSpec((1, D_MODEL), const2),
            pl.BlockSpec((D_MODEL, M_WIDTH), const2),
            pl.BlockSpec((M_WIDTH, D_MODEL), const2),
        ],
        out_specs=[
            pl.BlockSpec((1, B_HEADS, MIX_TILES, LANES, tm), lambda b, i: (b, 0, i, 0, 0)),
            pl.BlockSpec((1, B_HEADS, rows, LANES), lambda b, i: (b, 0, i, 0)),
            pl.BlockSpec((1, B_HEADS, MIX_TILES, VT_ROWS, tm), lambda b, i: (b, 0, i, 0, 0)),
            pl.BlockSpec((1, rows, A_WIDTH), lambda b, i: (b, i, 0)),
            pl.BlockSpec((1, rows, M_WIDTH), lambda b, i: (b, i, 0)),
        ],
        out_shape=[jax.ShapeDtypeStruct((batch, B_HEADS, nblk, LANES, tm), _BF16),
                   jax.ShapeDtypeStruct((batch, B_HEADS, seq, LANES), _BF16),
                   jax.ShapeDtypeStruct((batch, B_HEADS, nblk, VT_ROWS, tm), _BF16),
                   jax.ShapeDtypeStruct((batch, seq, A_WIDTH), _BF16),
                   jax.ShapeDtypeStruct((batch, seq, M_WIDTH), _BF16)],
        scratch_shapes=[pltpu.VMEM((F_ROWS, 1), _F32),
                        pltpu.VMEM((M_HEADS, MEM_TOKENS, M_WIDTH), _BF16),
                        pltpu.VMEM((M_HEADS, M_WIDTH, MEM_TOKENS), _BF16)],
        compiler_params=pltpu.CompilerParams(dimension_semantics=("parallel", "arbitrary"),
                                             vmem_limit_bytes=VMEM_LIMIT),
        name="mix_in",
    )(x, g_pre, w_nn, w_t, b_f_col, g_sgu, ws_cat, bs_exp, g_out_a, g_out_m_col, mem, g_mem, w_mk, w_mv_t)


def _fox_out_kernel(qt_ref, ka_ref, vt_ref, ya_ref, ym_ref, x_ref, wout_ref, gob_ref, gpost_ref,
                    wg_ref, wu_ref, wd_ref, x1_ref, wg16_ref, wu16_ref, wd16_ref):
    t = T_ATT
    wg16_ref[...] = wg_ref[...].astype(_BF16)
    wu16_ref[...] = wu_ref[...].astype(_BF16)
    wd16_ref[...] = wd_ref[...].astype(_BF16)
    causal = (lax.broadcasted_iota(jnp.int32, (t, t), 0) <= lax.broadcasted_iota(jnp.int32, (t, t), 1))

    w = t // Q_SPLIT
    chains = [(hd, qh) for hd in range(B_HEADS) for qh in range(Q_SPLIT)]

    def attend(local, j, carry, diag):
        start = j * t

        def qk(c):
            hd, qh = chains[c]
            nk = (qh + 1) * w if diag else t
            return _dot(ka_ref[0, hd, pl.ds(start, nk), :], qt_ref[0, hd, local, :, qh * w:(qh + 1) * w])

        scores = {c: qk(c) for c in range(QK_AHEAD)}
        out = []
        for c, (hd, qh) in enumerate(chains):
            m, acc = carry[c]
            s = scores.pop(c)
            nk = s.shape[0]
            if diag:
                s = jnp.where(causal[0:nk, qh * w:(qh + 1) * w], s, NEG_INF)
            m_new = jnp.maximum(m, jnp.max(s, axis=0, keepdims=True))
            p = jnp.exp2(s - m_new).astype(_BF16)
            if c + QK_AHEAD < len(chains):
                scores[c + QK_AHEAD] = qk(c + QK_AHEAD)
            out.append((m_new, jnp.exp2(m - m_new) * acc + _dot(vt_ref[0, hd, j, :, 0:nk], p)))
        return tuple(out)

    def query_tile(local, n_off):
        rows = slice(local * t, (local + 1) * t)
        z_am = (_dot(ya_ref[0, rows, :], wout_ref[0:A_WIDTH, :])
                + _dot(ym_ref[0, rows, :], wout_ref[A_WIDTH + B_WIDTH:, :]))
        state = tuple((jnp.full((1, w), NEG_INF, _F32), jnp.zeros((VT_ROWS, w), _F32)) for _ in chains)
        for j in range(n_off):
            state = attend(local, j, state, False)
        state = attend(local, n_off, state, True)
        heads = []
        for hd in range(B_HEADS):
            acc = jnp.concatenate([state[hd * Q_SPLIT + qh][1] for qh in range(Q_SPLIT)], axis=1)
            heads.append(acc[0:HEAD_DIM] * (1.0 / acc[HEAD_DIM:HEAD_DIM + 1]))
        yb_t = jnp.concatenate(heads, axis=0)
        yb_t = yb_t * lax.rsqrt(jnp.mean(yb_t * yb_t, axis=0, keepdims=True) + EPS) * gob_ref[...]
        yb_n = yb_t.T.astype(_BF16)
        z = z_am + _dot(yb_n, wout_ref[A_WIDTH:A_WIDTH + B_WIDTH, :])
        x1_ref[0, rows, :] = x_ref[0, rows, :] + _rms(z, gpost_ref[...])

    def step(first_tile):
        for local in range(FOX_TILES):
            query_tile(local, first_tile + local)

    for s in range(ka_ref.shape[2] // (t * FOX_TILES)):
        pl.when(pl.program_id(1) == s)(functools.partial(step, s * FOX_TILES))


def _fox_out(qt, ka, vt, ya, ym, x, w_out, g_out_b_col, g_post, w_gate, w_up, w_down):
    batch, seq, _ = x.shape
    t = T_ATT
    nblk = seq // t
    rows = FOX_TILES * t
    steps = nblk // FOX_TILES
    slab = D_MODEL // (batch * steps)
    slab_d = D_FF // batch
    assert slab * batch * steps == D_MODEL and slab % 16 == 0 and slab_d * batch == D_FF and slab_d % 16 == 0
    const2 = lambda b, i: (0, 0)
    slab_spec = pl.BlockSpec((slab, D_FF), lambda b, i: (b * steps + i, 0))
    slab_d_spec = pl.BlockSpec((slab_d, D_MODEL), lambda b, i: (b, 0))
    w16 = jax.ShapeDtypeStruct((D_MODEL, D_FF), _BF16)
    return pl.pallas_call(
        _fox_out_kernel,
        grid=(batch, nblk // FOX_TILES),
        in_specs=[
            pl.BlockSpec((1, B_HEADS, FOX_TILES, LANES, t), lambda b, i: (b, 0, i, 0, 0)),
            pl.BlockSpec((1, B_HEADS, seq, LANES), lambda b, i: (b, 0, 0, 0)),
            pl.BlockSpec((1, B_HEADS, nblk, VT_ROWS, t), lambda b, i: (b, 0, 0, 0, 0)),
            pl.BlockSpec((1, rows, A_WIDTH), lambda b, i: (b, i, 0)),
            pl.BlockSpec((1, rows, M_WIDTH), lambda b, i: (b, i, 0)),
            pl.BlockSpec((1, rows, D_MODEL), lambda b, i: (b, i, 0)),
            pl.BlockSpec((D_MODEL, D_MODEL), const2),
            pl.BlockSpec((B_WIDTH, 1), const2),
            pl.BlockSpec((1, D_MODEL), const2),
            slab_spec, slab_spec, slab_d_spec,
        ],
        out_specs=[pl.BlockSpec((1, rows, D_MODEL), lambda b, i: (b, i, 0)), slab_spec, slab_spec, slab_d_spec],
        out_shape=[jax.ShapeDtypeStruct(x.shape, x.dtype), w16, w16,
                   jax.ShapeDtypeStruct((D_FF, D_MODEL), _BF16)],
        compiler_params=pltpu.CompilerParams(dimension_semantics=("parallel", "arbitrary"),
                                             vmem_limit_bytes=VMEM_LIMIT),
        name="fox_out",
    )(qt, ka, vt, ya, ym, x, w_out, g_out_b_col, g_post, w_gate, w_up, w_down)


def _ffn_kernel(x_ref, gpre_ref, wg_ref, wu_ref, wd_ref, gpost_ref, o_ref):
    step = D_FF // FF_SPLIT
    rows = x_ref.shape[0] // FF_ROW_SPLIT

    def pre_norm(r):
        x = x_ref[r * rows:(r + 1) * rows, :]
        return x, _rms(x, gpre_ref[...]).astype(_BF16)

    def finish(r, x, ff):
        o_ref[r * rows:(r + 1) * rows, :] = x + _rms(ff, gpost_ref[...])

    cur = pre_norm(0)
    done = None
    for r in range(FF_ROW_SPLIT):
        x, h = cur
        ff = None
        for c in range(FF_SPLIT):
            sl = slice(c * step, (c + 1) * step)
            g = _dot(h, wg_ref[:, sl])
            a = (g * (1.0 / (1.0 + jnp.exp(-g))) * _dot(h, wu_ref[:, sl])).astype(_BF16)
            part = _dot(a, wd_ref[sl, :])
            ff = part if ff is None else ff + part
            if c == FF_OVERLAP_AT:
                if done is not None:
                    finish(*done)
                if r + 1 < FF_ROW_SPLIT:
                    cur = pre_norm(r + 1)
        done = (r, x, ff)
    finish(*done)


def _ffn(x, g_pre, w_gate, w_up, w_down, g_post):
    tokens = x.shape[0]
    tm = TM_FFN
    const = lambda i: (0, 0)
    resident = functools.partial(pl.BlockSpec, index_map=const, pipeline_mode=pl.Buffered(1))
    return pl.pallas_call(
        _ffn_kernel,
        grid=(tokens // tm,),
        in_specs=[
            pl.BlockSpec((tm, D_MODEL), lambda i: (i, 0)),
            pl.BlockSpec((1, D_MODEL), const),
            resident((D_MODEL, D_FF)),
            resident((D_MODEL, D_FF)),
            resident((D_FF, D_MODEL)),
            pl.BlockSpec((1, D_MODEL), const),
        ],
        out_specs=pl.BlockSpec((tm, D_MODEL), lambda i: (i, 0)),
        out_shape=jax.ShapeDtypeStruct(x.shape, x.dtype),
        compiler_params=pltpu.CompilerParams(dimension_semantics=("parallel",),
                                             vmem_limit_bytes=VMEM_LIMIT),
        name="ffn",
    )(x, g_pre, w_gate, w_up, w_down, g_post)


def kernel(x, mem, g_pre_mix, w_in, b_f, g_sgu, w_s, b_s, g_out_a, g_out_b, g_out_m, g_mem, w_mem_kv, w_out,
           g_post_mix, g_pre_ffn, w_gate, w_up, w_down, g_post_ffn):
    batch, seq, d = x.shape
    depth = w_in.shape[0]
    row = lambda a: a.reshape(1, -1)
    q_lo = 2 * A_WIDTH
    f_lo = q_lo + 3 * B_WIDTH
    for l in range(depth):
        w = w_in[l]
        w_nn = jnp.concatenate([w[:, :q_lo], w[:, q_lo + B_WIDTH:q_lo + 2 * B_WIDTH]],
                               axis=1).astype(_BF16)
        w_f = jnp.pad(jnp.repeat(w[:, f_lo:f_lo + B_HEADS], SUBLANES, axis=1),
                      ((0, 0), (0, F_ROWS - SUBLANES * B_HEADS)))
        w_t = jnp.concatenate([w[:, q_lo:q_lo + B_WIDTH], w[:, q_lo + 2 * B_WIDTH:f_lo], w_f, w[:, f_lo + B_HEADS:]],
                              axis=1).T.astype(_BF16)
        b_f_col = jnp.pad(jnp.repeat(b_f[l], SUBLANES), (0, F_ROWS - SUBLANES * B_HEADS)).reshape(F_ROWS, 1)
        ws_cat = w_s[l].reshape(PAIRS, 2, CHUNK, CHUNK).transpose(0, 2, 1, 3).reshape(PAIRS, CHUNK, 2 * CHUNK)
        bs_exp = jnp.repeat(b_s[l].T, HEAD_DIM, axis=1)

        qt, ka, vt, ya, ym = _mix_in(x, row(g_pre_mix[l]), w_nn, w_t, b_f_col, row(g_sgu[l]), ws_cat, bs_exp,
                                     row(g_out_a[l]), g_out_m[l].reshape(-1, 1), mem, row(g_mem[l]),
                                     w_mem_kv[l][:, :M_WIDTH].astype(_BF16), w_mem_kv[l][:, M_WIDTH:].T.astype(_BF16))
        x, wg16, wu16, wd16 = _fox_out(qt, ka, vt, ya, ym, x, w_out[l].astype(_BF16), g_out_b[l].reshape(-1, 1),
                                       row(g_post_mix[l]), w_gate[l], w_up[l], w_down[l])
        x = _ffn(x.reshape(batch * seq, d), row(g_pre_ffn[l]), wg16, wu16, wd16,
                 row(g_post_ffn[l])).reshape(batch, seq, d)
    return x
```

```python
import functools

import jax
import jax.numpy as jnp
from jax import lax
from jax.experimental import pallas as pl
from jax.experimental.pallas import tpu as pltpu

D_MODEL = 1024
HEAD_DIM = 64
A_GROUPS = 6
B_HEADS = 6
M_HEADS = 4
A_WIDTH = A_GROUPS * HEAD_DIM
B_WIDTH = B_HEADS * HEAD_DIM
M_WIDTH = M_HEADS * HEAD_DIM
CHUNK = 128
MEM_TOKENS = 256
D_FF = 2816
EPS = 1e-6
NEG_INF = -1e30

LANES = 128
SUBLANES = 8
PAIRS = B_HEADS // 2
SCALE = HEAD_DIM ** -0.5
LOG2E = 1.4426950408889634

K_OFF = 2 * A_WIDTH
NN_COLS = K_OFF + B_WIDTH
V_ROW = B_WIDTH
F_ROW = 2 * B_WIDTH
F_ROWS = 64
QM_ROW = F_ROW + F_ROWS
NT_ROWS = QM_ROW + M_WIDTH
VT_ROWS = HEAD_DIM + 16

T_ATT = 512
MIX_TILES = 4
FOX_TILES = 2
Q_SPLIT = 2
QK_AHEAD = 4
TM_FFN = 1024
FF_ROW_SPLIT = 2
FF_SPLIT = 11
FF_OVERLAP_AT = 1

VMEM_LIMIT = 56 * 1024 * 1024

_F32 = jnp.float32
_BF16 = jnp.bfloat16


def _dot(a, b):
    return jnp.dot(a, b, preferred_element_type=_F32)


def _dot_nt(a, b):
    return lax.dot_general(a, b, (((1,), (1,)), ((), ())), preferred_element_type=_F32)


def _rms(x, g):
    return x * lax.rsqrt(jnp.mean(x * x, axis=-1, keepdims=True) + EPS) * g


def _gelu_tanh(x):
    return 0.5 * x * (1.0 + jnp.tanh(0.7978845608028654 * (x + 0.044715 * (x * x * x))))


def _log_sigmoid(x):
    return -(jnp.maximum(-x, 0.0) + jnp.log1p(jnp.exp(-jnp.abs(x))))


def _split3(x):
    hi = x.astype(_BF16).astype(_F32)
    r = x - hi
    mid = r.astype(_BF16).astype(_F32)
    lo = (r - mid).astype(_BF16).astype(_F32)
    return hi, mid, lo


def _mix_in_kernel(x_ref, gpre_ref, wnn_ref, wt_ref, bf_ref, gsgu_ref, ws_ref, bs_ref, goa_ref, gom_ref,
                   mem_ref, gmem_ref, wmk_ref, wmvt_ref, qt_ref, ka_ref, vt_ref, ya_ref, ym_ref,
                   carry_ref, km_ref, vmt_ref):
    tm = T_ATT

    mn = _rms(mem_ref[0], gmem_ref[...]).astype(_BF16)
    km = _dot(mn, wmk_ref[...])
    vm_t = _dot_nt(wmvt_ref[...], mn)
    head_k = lax.broadcasted_iota(jnp.int32, (MEM_TOKENS, M_WIDTH), 1) // HEAD_DIM
    head_v = lax.broadcasted_iota(jnp.int32, (M_WIDTH, MEM_TOKENS), 0) // HEAD_DIM
    for hh in range(M_HEADS):
        km_ref[hh] = jnp.where(head_k == hh, km, 0.0).astype(_BF16)
        vmt_ref[hh] = jnp.where(head_v == hh, vm_t, 0.0).astype(_BF16)

    @pl.when(pl.program_id(1) == 0)
    def _():
        carry_ref[...] = jnp.zeros_like(carry_ref)

    carry = {"c": carry_ref[...]}

    def tile_stages(local):
        rows = slice(local * tm, (local + 1) * tm)
        v = {}

        def st_h():
            v["h"] = _rms(x_ref[0, rows, :], gpre_ref[...]).astype(_BF16)

        def st_zu():
            v["t_zu"] = _dot(v["h"], wnn_ref[:, 0:A_WIDTH])

        def st_zv():
            v["t_zv"] = _dot(v["h"], wnn_ref[:, A_WIDTH:2 * A_WIDTH])

        def st_k():
            v["k"] = _dot(v["h"], wnn_ref[:, K_OFF:K_OFF + B_WIDTH])

        def st_qt():
            v["q_t"] = _dot_nt(wt_ref[0:B_WIDTH, :], v["h"]) * (SCALE * LOG2E)

        def st_vt():
            v["v_t"] = _dot_nt(wt_ref[V_ROW:V_ROW + B_WIDTH, :], v["h"])

        def st_f():
            v["t_f"] = _dot_nt(wt_ref[F_ROW:F_ROW + F_ROWS, :], v["h"])

        def st_qm():
            v["qm_t"] = (_dot_nt(wt_ref[QM_ROW:NT_ROWS, :], v["h"]) * (SCALE * LOG2E)).astype(_BF16)

        def st_u():
            v["u"] = _gelu_tanh(v["t_zu"])

        def st_vn():
            v["vn"] = _rms(_gelu_tanh(v["t_zv"]), gsgu_ref[...])

        def st_vt_store():
            ones_v = jnp.where(lax.broadcasted_iota(jnp.int32, (VT_ROWS - HEAD_DIM, tm), 0) == 0, 1.0, 0.0)
            for hd in range(B_HEADS):
                vt_ref[0, hd, local] = jnp.concatenate([v["v_t"][hd * HEAD_DIM:(hd + 1) * HEAD_DIM], ones_v],
                                                       axis=0).astype(_BF16)

        rowf = lax.broadcasted_iota(jnp.int32, (F_ROWS, tm), 0)

        def st_gate_parts():
            logf = _log_sigmoid(v["t_f"] + bf_ref[...])
            logf = jnp.where(rowf < SUBLANES * B_HEADS, logf, 0.0)
            v["parts"] = jnp.concatenate(_split3(logf), axis=0).astype(_BF16)

        def st_gate_dot():
            triu = (lax.broadcasted_iota(jnp.int32, (tm, tm), 0)
                    <= lax.broadcasted_iota(jnp.int32, (tm, tm), 1)).astype(_BF16)
            v["cs"] = _dot(v["parts"], triu)

        def st_gate_bias():
            cs = v["cs"]
            c = cs[0:F_ROWS] + cs[F_ROWS:2 * F_ROWS] + cs[2 * F_ROWS:] + carry["c"]
            carry["c"] = c[:, tm - 1:tm]
            c_hi, c_mid, c_lo = _split3(c * LOG2E)
            j8 = rowf % SUBLANES
            v["cq"] = jnp.where(j8 == 0, c_hi, jnp.where(j8 == 1, c_mid, jnp.where(j8 == 2, c_lo,
                                                                                     jnp.where(j8 < 6, 1.0, 0.0))))
            ck = jnp.where(j8 < 3, 1.0, jnp.where(j8 == 3, -c_hi, jnp.where(j8 == 4, -c_mid,
                                                                               jnp.where(j8 == 5, -c_lo, 0.0))))
            v["ck_t"] = jnp.concatenate([ck, jnp.zeros((LANES - F_ROWS, tm), _F32)], axis=0).T

        def st_sgu():
            lane = lax.broadcasted_iota(jnp.int32, (CHUNK, LANES), 1)
            row_s = lax.broadcasted_iota(jnp.int32, (CHUNK, 2 * CHUNK), 0)
            col_s = lax.broadcasted_iota(jnp.int32, (CHUNK, 2 * CHUNK), 1) % CHUNK
            z_pairs = []
            for p in range(PAIRS):
                w_pair = jnp.where(col_s <= row_s, ws_ref[p], 0.0).astype(_BF16)
                z_chunks = []
                for c_i in range(tm // CHUNK):
                    v_pair = v["vn"][c_i * CHUNK:(c_i + 1) * CHUNK, p * LANES:(p + 1) * LANES]
                    rhs = jnp.concatenate([jnp.where(lane < HEAD_DIM, v_pair, 0.0),
                                           jnp.where(lane >= HEAD_DIM, v_pair, 0.0)], axis=0).astype(_BF16)
                    z_chunks.append(_dot(w_pair, rhs) + bs_ref[:, p * LANES:(p + 1) * LANES])
                z_pairs.append(jnp.concatenate(z_chunks, axis=0))
            v["z"] = jnp.concatenate(z_pairs, axis=1)

        def st_ya():
            ya_ref[0, rows, :] = _rms(v["u"] * v["z"], goa_ref[...]).astype(_BF16)

        def st_q_tables():
            zeros_q = jnp.zeros((LANES - HEAD_DIM - SUBLANES, tm), _F32)
            for hd in range(B_HEADS):
                q_h = v["q_t"][hd * HEAD_DIM:(hd + 1) * HEAD_DIM]
                cq_h = v["cq"][SUBLANES * hd:SUBLANES * (hd + 1)]
                q_aug = [q_h, cq_h, zeros_q] if hd % 2 == 0 else [cq_h, zeros_q, q_h]
                qt_ref[0, hd, local] = jnp.concatenate(q_aug, axis=0).astype(_BF16)

        def st_k_tables(heads):
            lane_t = lax.broadcasted_iota(jnp.int32, (tm, LANES), 1)
            for hd in heads:
                k_pair = v["k"][:, (hd // 2) * LANES:(hd // 2 + 1) * LANES]
                if hd % 2 == 0:
                    k_aug = jnp.where(lane_t < HEAD_DIM, k_pair, pltpu.roll(v["ck_t"], HEAD_DIM - SUBLANES * hd, 1))
                else:
                    k_aug = jnp.where(lane_t >= HEAD_DIM, k_pair, pltpu.roll(v["ck_t"], LANES - SUBLANES * hd, 1))
                ka_ref[0, hd, rows, :] = k_aug.astype(_BF16)

        def st_mem_qk(hh):
            v["s_m", hh] = _dot(km_ref[hh], v["qm_t"])

        def st_mem_softmax(hh):
            s = v.pop(("s_m", hh))
            e = jnp.exp2(s - jnp.max(s, axis=0, keepdims=True))
            v["p_m", hh] = (e * (1.0 / jnp.sum(e, axis=0, keepdims=True))).astype(_BF16)

        def st_mem_pv(hh):
            part = _dot(vmt_ref[hh], v.pop(("p_m", hh)))
            v["o_t"] = part if hh == 0 else v["o_t"] + part

        def st_ym():
            o_t = v["o_t"]
            ym_t = o_t * lax.rsqrt(jnp.mean(o_t * o_t, axis=0, keepdims=True) + EPS) * gom_ref[...]
            ym_ref[0, rows, :] = ym_t.T.astype(_BF16)

        P = functools.partial
        return (st_h, st_zu, st_zv, st_qm, st_u, st_vt, st_f, st_vn, st_qt, st_vt_store,
                P(st_mem_qk, 0), P(st_mem_qk, 1), st_gate_parts, st_sgu, st_gate_dot, P(st_mem_softmax, 0),
                st_k, P(st_mem_qk, 2), st_ya, P(st_mem_pv, 0), P(st_mem_softmax, 1), st_gate_bias,
                P(st_mem_qk, 3), P(st_mem_pv, 1), P(st_mem_softmax, 2), st_q_tables, P(st_mem_pv, 2),
                P(st_mem_softmax, 3), P(st_k_tables, (0, 1, 2)), P(st_mem_pv, 3), P(st_k_tables, (3, 4, 5)),
                st_ym)

    for local in range(MIX_TILES):
        for stage in tile_stages(local):
            stage()
    carry_ref[...] = carry["c"]


def _mix_in(x, g_pre, w_nn, w_t, b_f_col, g_sgu, ws_cat, bs_exp, g_out_a, g_out_m_col, mem, g_mem, w_mk, w_mv_t):
    batch, seq, _ = x.shape
    tm = T_ATT
    nblk = seq // tm
    rows = MIX_TILES * tm
    const2 = lambda b, i: (0, 0)
    return pl.pallas_call(
        _mix_in_kernel,
        grid=(batch, nblk // MIX_TILES),
        in_specs=[
            pl.BlockSpec((1, rows, D_MODEL), lambda b, i: (b, i, 0)),
            pl.BlockSpec((1, D_MODEL), const2),
            pl.BlockSpec((D_MODEL, NN_COLS), const2),
            pl.BlockSpec((NT_ROWS, D_MODEL), const2),
            pl.BlockSpec((F_ROWS, 1), const2),
            pl.BlockSpec((1, A_WIDTH), const2),
            pl.BlockSpec((PAIRS, CHUNK, 2 * CHUNK), lambda b, i: (0, 0, 0)),
            pl.BlockSpec((CHUNK, A_WIDTH), const2),
            pl.BlockSpec((1, A_WIDTH), const2),
            pl.BlockSpec((M_WIDTH, 1), const2),
            pl.BlockSpec((1, MEM_TOKENS, D_MODEL), lambda b, i: (b, 0, 0)),
            pl.BlockSpec((1, D_MODEL), const2),
            pl.BlockSpec((D_MODEL, M_WIDTH), const2),
            pl.BlockSpec((M_WIDTH, D_MODEL), const2),
        ],
        out_specs=[
            pl.BlockSpec((1, B_HEADS, MIX_TILES, LANES, tm), lambda b, i: (b, 0, i, 0, 0)),
            pl.BlockSpec((1, B_HEADS, rows, LANES), lambda b, i: (b, 0, i, 0)),
            pl.BlockSpec((1, B_HEADS, MIX_TILES, VT_ROWS, tm), lambda b, i: (b, 0, i, 0, 0)),
            pl.BlockSpec((1, rows, A_WIDTH), lambda b, i: (b, i, 0)),
            pl.BlockSpec((1, rows, M_WIDTH), lambda b, i: (b, i, 0)),
        ],
        out_shape=[jax.ShapeDtypeStruct((batch, B_HEADS, nblk, LANES, tm), _BF16),
                   jax.ShapeDtypeStruct((batch, B_HEADS, seq, LANES), _BF16),
                   jax.ShapeDtypeStruct((batch, B_HEADS, nblk, VT_ROWS, tm), _BF16),
                   jax.ShapeDtypeStruct((batch, seq, A_WIDTH), _BF16),
                   jax.ShapeDtypeStruct((batch, seq, M_WIDTH), _BF16)],
        scratch_shapes=[pltpu.VMEM((F_ROWS, 1), _F32),
                        pltpu.VMEM((M_HEADS, MEM_TOKENS, M_WIDTH), _BF16),
                        pltpu.VMEM((M_HEADS, M_WIDTH, MEM_TOKENS), _BF16)],
        compiler_params=pltpu.CompilerParams(dimension_semantics=("parallel", "arbitrary"),
                                             vmem_limit_bytes=VMEM_LIMIT),
        name="mix_in",
    )(x, g_pre, w_nn, w_t, b_f_col, g_sgu, ws_cat, bs_exp, g_out_a, g_out_m_col, mem, g_mem, w_mk, w_mv_t)


def _fox_out_kernel(qt_ref, ka_ref, vt_ref, ya_ref, ym_ref, x_ref, wout_ref, gob_ref, gpost_ref,
                    wg_ref, wu_ref, wd_ref, x1_ref, wg16_ref, wu16_ref, wd16_ref):
    t = T_ATT
    wg16_ref[...] = wg_ref[0].astype(_BF16)
    wu16_ref[...] = wu_ref[0].astype(_BF16)
    wd16_ref[...] = wd_ref[0].astype(_BF16)
    causal = (lax.broadcasted_iota(jnp.int32, (t, t), 0) <= lax.broadcasted_iota(jnp.int32, (t, t), 1))

    w = t // Q_SPLIT
    chains = [(hd, qh) for hd in range(B_HEADS) for qh in range(Q_SPLIT)]

    def attend(local, j, carry, diag):
        start = j * t

        def qk(c):
            hd, qh = chains[c]
            nk = (qh + 1) * w if diag else t
            return _dot(ka_ref[0, hd, pl.ds(start, nk), :], qt_ref[0, hd, local, :, qh * w:(qh + 1) * w])

        scores = {c: qk(c) for c in range(QK_AHEAD)}
        out = []
        for c, (hd, qh) in enumerate(chains):
            m, acc = carry[c]
            s = scores.pop(c)
            nk = s.shape[0]
            if diag:
                s = jnp.where(causal[0:nk, qh * w:(qh + 1) * w], s, NEG_INF)
            m_new = jnp.maximum(m, jnp.max(s, axis=0, keepdims=True))
            p = jnp.exp2(s - m_new).astype(_BF16)
            if c + QK_AHEAD < len(chains):
                scores[c + QK_AHEAD] = qk(c + QK_AHEAD)
            out.append((m_new, jnp.exp2(m - m_new) * acc + _dot(vt_ref[0, hd, j, :, 0:nk], p)))
        return tuple(out)

    def query_tile(local, n_off):
        rows = slice(local * t, (local + 1) * t)
        z_am = (_dot(ya_ref[0, rows, :], wout_ref[0:A_WIDTH, :])
                + _dot(ym_ref[0, rows, :], wout_ref[A_WIDTH + B_WIDTH:, :]))
        state = tuple((jnp.full((1, w), NEG_INF, _F32), jnp.zeros((VT_ROWS, w), _F32)) for _ in chains)
        for j in range(n_off):
            state = attend(local, j, state, False)
        state = attend(local, n_off, state, True)
        heads = []
        for hd in range(B_HEADS):
            acc = jnp.concatenate([state[hd * Q_SPLIT + qh][1] for qh in range(Q_SPLIT)], axis=1)
            heads.append(acc[0:HEAD_DIM] * (1.0 / acc[HEAD_DIM:HEAD_DIM + 1]))
        yb_t = jnp.concatenate(heads, axis=0)
        yb_t = yb_t * lax.rsqrt(jnp.mean(yb_t * yb_t, axis=0, keepdims=True) + EPS) * gob_ref[...]
        yb_n = yb_t.T.astype(_BF16)
        z = z_am + _dot(yb_n, wout_ref[A_WIDTH:A_WIDTH + B_WIDTH, :])
        x1_ref[0, rows, :] = x_ref[0, rows, :] + _rms(z, gpost_ref[...])

    def step(first_tile):
        for local in range(FOX_TILES):
            query_tile(local, first_tile + local)

    for s in range(ka_ref.shape[2] // (t * FOX_TILES)):
        pl.when(pl.program_id(1) == s)(functools.partial(step, s * FOX_TILES))


def _fox_out(qt, ka, vt, ya, ym, x, w_out, g_out_b_col, g_post, w_gate, w_up, w_down, layer):
    batch, seq, _ = x.shape
    t = T_ATT
    nblk = seq // t
    rows = FOX_TILES * t
    steps = nblk // FOX_TILES
    slab = D_MODEL // (batch * steps)
    slab_d = D_FF // batch
    assert slab * batch * steps == D_MODEL and slab % 16 == 0 and slab_d * batch == D_FF and slab_d % 16 == 0
    const2 = lambda b, i: (0, 0)
    slab_spec = pl.BlockSpec((slab, D_FF), lambda b, i: (b * steps + i, 0))
    slab_d_spec = pl.BlockSpec((slab_d, D_MODEL), lambda b, i: (b, 0))
    slab_in = pl.BlockSpec((1, slab, D_FF), lambda b, i: (layer, b * steps + i, 0))
    slab_d_in = pl.BlockSpec((1, slab_d, D_MODEL), lambda b, i: (layer, b, 0))
    w16 = jax.ShapeDtypeStruct((D_MODEL, D_FF), _BF16)
    return pl.pallas_call(
        _fox_out_kernel,
        grid=(batch, nblk // FOX_TILES),
        in_specs=[
            pl.BlockSpec((1, B_HEADS, FOX_TILES, LANES, t), lambda b, i: (b, 0, i, 0, 0)),
            pl.BlockSpec((1, B_HEADS, seq, LANES), lambda b, i: (b, 0, 0, 0)),
            pl.BlockSpec((1, B_HEADS, nblk, VT_ROWS, t), lambda b, i: (b, 0, 0, 0, 0)),
            pl.BlockSpec((1, rows, A_WIDTH), lambda b, i: (b, i, 0)),
            pl.BlockSpec((1, rows, M_WIDTH), lambda b, i: (b, i, 0)),
            pl.BlockSpec((1, rows, D_MODEL), lambda b, i: (b, i, 0)),
            pl.BlockSpec((D_MODEL, D_MODEL), const2),
            pl.BlockSpec((B_WIDTH, 1), const2),
            pl.BlockSpec((1, D_MODEL), const2),
            slab_in, slab_in, slab_d_in,
        ],
        out_specs=[pl.BlockSpec((1, rows, D_MODEL), lambda b, i: (b, i, 0)), slab_spec, slab_spec, slab_d_spec],
        out_shape=[jax.ShapeDtypeStruct(x.shape, x.dtype), w16, w16,
                   jax.ShapeDtypeStruct((D_FF, D_MODEL), _BF16)],
        compiler_params=pltpu.CompilerParams(dimension_semantics=("parallel", "arbitrary"),
                                             vmem_limit_bytes=VMEM_LIMIT),
        name="fox_out",
    )(qt, ka, vt, ya, ym, x, w_out, g_out_b_col, g_post, w_gate, w_up, w_down)


def _ffn_kernel(x_ref, gpre_ref, wg_ref, wu_ref, wd_ref, gpost_ref, o_ref):
    step = D_FF // FF_SPLIT
    rows = x_ref.shape[0] // FF_ROW_SPLIT

    def pre_norm(r):
        x = x_ref[r * rows:(r + 1) * rows, :]
        return x, _rms(x, gpre_ref[...]).astype(_BF16)

    def finish(r, x, ff):
        o_ref[r * rows:(r + 1) * rows, :] = x + _rms(ff, gpost_ref[...])

    cur = pre_norm(0)
    done = None
    for r in range(FF_ROW_SPLIT):
        x, h = cur
        ff = None
        for c in range(FF_SPLIT):
            sl = slice(c * step, (c + 1) * step)
            g = _dot(h, wg_ref[:, sl])
            a = (g * (1.0 / (1.0 + jnp.exp(-g))) * _dot(h, wu_ref[:, sl])).astype(_BF16)
            part = _dot(a, wd_ref[sl, :])
            ff = part if ff is None else ff + part
            if c == FF_OVERLAP_AT:
                if done is not None:
                    finish(*done)
                if r + 1 < FF_ROW_SPLIT:
                    cur = pre_norm(r + 1)
        done = (r, x, ff)
    finish(*done)


def _ffn(x, g_pre, w_gate, w_up, w_down, g_post):
    tokens = x.shape[0]
    tm = TM_FFN
    const = lambda i: (0, 0)
    resident = functools.partial(pl.BlockSpec, index_map=const, pipeline_mode=pl.Buffered(1))
    return pl.pallas_call(
        _ffn_kernel,
        grid=(tokens // tm,),
        in_specs=[
            pl.BlockSpec((tm, D_MODEL), lambda i: (i, 0)),
            pl.BlockSpec((1, D_MODEL), const),
            resident((D_MODEL, D_FF)),
            resident((D_MODEL, D_FF)),
            resident((D_FF, D_MODEL)),
            pl.BlockSpec((1, D_MODEL), const),
        ],
        out_specs=pl.BlockSpec((tm, D_MODEL), lambda i: (i, 0)),
        out_shape=jax.ShapeDtypeStruct(x.shape, x.dtype),
        compiler_params=pltpu.CompilerParams(dimension_semantics=("parallel",),
                                             vmem_limit_bytes=VMEM_LIMIT),
        name="ffn",
    )(x, g_pre, w_gate, w_up, w_down, g_post)


def kernel(x, mem, g_pre_mix, w_in, b_f, g_sgu, w_s, b_s, g_out_a, g_out_b, g_out_m, g_mem, w_mem_kv, w_out,
           g_post_mix, g_pre_ffn, w_gate, w_up, w_down, g_post_ffn):
    batch, seq, d = x.shape
    depth = w_in.shape[0]
    row = lambda a: a.reshape(1, -1)
    q_lo = 2 * A_WIDTH
    f_lo = q_lo + 3 * B_WIDTH
    for l in range(depth):
        w = w_in[l]
        w_nn = jnp.concatenate([w[:, :q_lo], w[:, q_lo + B_WIDTH:q_lo + 2 * B_WIDTH]],
                               axis=1).astype(_BF16)
        w_f = jnp.pad(jnp.repeat(w[:, f_lo:f_lo + B_HEADS], SUBLANES, axis=1),
                      ((0, 0), (0, F_ROWS - SUBLANES * B_HEADS)))
        w_t = jnp.concatenate([w[:, q_lo:q_lo + B_WIDTH], w[:, q_lo + 2 * B_WIDTH:f_lo], w_f, w[:, f_lo + B_HEADS:]],
                              axis=1).T.astype(_BF16)
        b_f_col = jnp.pad(jnp.repeat(b_f[l], SUBLANES), (0, F_ROWS - SUBLANES * B_HEADS)).reshape(F_ROWS, 1)
        ws_cat = w_s[l].reshape(PAIRS, 2, CHUNK, CHUNK).transpose(0, 2, 1, 3).reshape(PAIRS, CHUNK, 2 * CHUNK)
        bs_exp = jnp.repeat(b_s[l].T, HEAD_DIM, axis=1)

        qt, ka, vt, ya, ym = _mix_in(x, row(g_pre_mix[l]), w_nn, w_t, b_f_col, row(g_sgu[l]), ws_cat, bs_exp,
                                     row(g_out_a[l]), g_out_m[l].reshape(-1, 1), mem, row(g_mem[l]),
                                     w_mem_kv[l][:, :M_WIDTH].astype(_BF16), w_mem_kv[l][:, M_WIDTH:].T.astype(_BF16))
        x, wg16, wu16, wd16 = _fox_out(qt, ka, vt, ya, ym, x, w_out[l].astype(_BF16), g_out_b[l].reshape(-1, 1),
                                       row(g_post_mix[l]), w_gate, w_up, w_down, l)
        x = _ffn(x.reshape(batch * seq, d), row(g_pre_ffn[l]), wg16, wu16, wd16,
                 row(g_post_ffn[l])).reshape(batch, seq, d)
    return x
```

```python
import functools

import jax
import jax.numpy as jnp
from jax import lax
from jax.experimental import pallas as pl
from jax.experimental.pallas import tpu as pltpu

D_MODEL = 1024
HEAD_DIM = 64
A_GROUPS = 6
B_HEADS = 6
M_HEADS = 4
A_WIDTH = A_GROUPS * HEAD_DIM
B_WIDTH = B_HEADS * HEAD_DIM
M_WIDTH = M_HEADS * HEAD_DIM
CHUNK = 128
MEM_TOKENS = 256
D_FF = 2816
EPS = 1e-6
NEG_INF = -1e30

LANES = 128
SUBLANES = 8
PAIRS = B_HEADS // 2
SCALE = HEAD_DIM ** -0.5
LOG2E = 1.4426950408889634

K_OFF = 2 * A_WIDTH
NN_COLS = K_OFF + B_WIDTH
V_ROW = B_WIDTH
F_ROW = 2 * B_WIDTH
F_ROWS = 64
QM_ROW = F_ROW + F_ROWS
NT_ROWS = QM_ROW + M_WIDTH
VT_ROWS = HEAD_DIM + 16

T_ATT = 512
MIX_TILES = 4
FOX_TILES = 2
Q_SPLIT = 2
QK_AHEAD = 4
TM_FFN = 1024
FF_ROW_SPLIT = 2
FF_SPLIT = 11

VMEM_LIMIT = 56 * 1024 * 1024

_F32 = jnp.float32
_BF16 = jnp.bfloat16


def _dot(a, b):
    return jnp.dot(a, b, preferred_element_type=_F32)


def _dot_nt(a, b):
    return lax.dot_general(a, b, (((1,), (1,)), ((), ())), preferred_element_type=_F32)


def _rms(x, g):
    return x * lax.rsqrt(jnp.mean(x * x, axis=-1, keepdims=True) + EPS) * g


def _gelu_tanh(x):
    return 0.5 * x * (1.0 + jnp.tanh(0.7978845608028654 * (x + 0.044715 * (x * x * x))))


def _log_sigmoid(x):
    return -(jnp.maximum(-x, 0.0) + jnp.log1p(jnp.exp(-jnp.abs(x))))


def _split3(x):
    hi = x.astype(_BF16).astype(_F32)
    r = x - hi
    mid = r.astype(_BF16).astype(_F32)
    lo = (r - mid).astype(_BF16).astype(_F32)
    return hi, mid, lo


def _mix_in_kernel(x_ref, gpre_ref, wnn_ref, wt_ref, bf_ref, gsgu_ref, ws_ref, bs_ref, goa_ref, gom_ref,
                   mem_ref, gmem_ref, wmk_ref, wmvt_ref, wout_ref, qt_ref, ka_ref, vt_ref, ya_ref, ym_ref,
                   wout16_ref, carry_ref, km_ref, vmt_ref):
    tm = T_ATT
    wout16_ref[...] = wout_ref[0].astype(_BF16)

    mn = _rms(mem_ref[0], gmem_ref[...]).astype(_BF16)
    km = _dot(mn, wmk_ref[...])
    vm_t = _dot_nt(wmvt_ref[...], mn)
    head_k = lax.broadcasted_iota(jnp.int32, (MEM_TOKENS, M_WIDTH), 1) // HEAD_DIM
    head_v = lax.broadcasted_iota(jnp.int32, (M_WIDTH, MEM_TOKENS), 0) // HEAD_DIM
    for hh in range(M_HEADS):
        km_ref[hh] = jnp.where(head_k == hh, km, 0.0).astype(_BF16)
        vmt_ref[hh] = jnp.where(head_v == hh, vm_t, 0.0).astype(_BF16)

    @pl.when(pl.program_id(1) == 0)
    def _():
        carry_ref[...] = jnp.zeros_like(carry_ref)

    carry = {"c": carry_ref[...]}

    def tile_stages(local):
        rows = slice(local * tm, (local + 1) * tm)
        v = {}

        def st_h():
            v["h"] = _rms(x_ref[0, rows, :], gpre_ref[...]).astype(_BF16)

        def st_zu():
            v["t_zu"] = _dot(v["h"], wnn_ref[:, 0:A_WIDTH])

        def st_zv():
            v["t_zv"] = _dot(v["h"], wnn_ref[:, A_WIDTH:2 * A_WIDTH])

        def st_k():
            v["k"] = _dot(v["h"], wnn_ref[:, K_OFF:K_OFF + B_WIDTH])

        def st_qt():
            v["q_t"] = _dot_nt(wt_ref[0:B_WIDTH, :], v["h"]) * (SCALE * LOG2E)

        def st_vt():
            v["v_t"] = _dot_nt(wt_ref[V_ROW:V_ROW + B_WIDTH, :], v["h"])

        def st_f():
            v["t_f"] = _dot_nt(wt_ref[F_ROW:F_ROW + F_ROWS, :], v["h"])

        def st_qm():
            v["qm_t"] = (_dot_nt(wt_ref[QM_ROW:NT_ROWS, :], v["h"]) * (SCALE * LOG2E)).astype(_BF16)

        def st_u():
            v["u"] = _gelu_tanh(v["t_zu"])

        def st_vn():
            v["vn"] = _rms(_gelu_tanh(v["t_zv"]), gsgu_ref[...])

        def st_vt_store():
            ones_v = jnp.where(lax.broadcasted_iota(jnp.int32, (VT_ROWS - HEAD_DIM, tm), 0) == 0, 1.0, 0.0)
            for hd in range(B_HEADS):
                vt_ref[0, hd, local] = jnp.concatenate([v["v_t"][hd * HEAD_DIM:(hd + 1) * HEAD_DIM], ones_v],
                                                       axis=0).astype(_BF16)

        rowf = lax.broadcasted_iota(jnp.int32, (F_ROWS, tm), 0)

        def st_gate_parts():
            logf = _log_sigmoid(v["t_f"] + bf_ref[...])
            logf = jnp.where(rowf < SUBLANES * B_HEADS, logf, 0.0)
            v["parts"] = jnp.concatenate(_split3(logf), axis=0).astype(_BF16)

        def st_gate_dot():
            triu = (lax.broadcasted_iota(jnp.int32, (tm, tm), 0)
                    <= lax.broadcasted_iota(jnp.int32, (tm, tm), 1)).astype(_BF16)
            v["cs"] = _dot(v["parts"], triu)

        def st_gate_bias():
            cs = v["cs"]
            c = cs[0:F_ROWS] + cs[F_ROWS:2 * F_ROWS] + cs[2 * F_ROWS:] + carry["c"]
            carry["c"] = c[:, tm - 1:tm]
            c_hi, c_mid, c_lo = _split3(c * LOG2E)
            j8 = rowf % SUBLANES
            v["cq"] = jnp.where(j8 == 0, c_hi, jnp.where(j8 == 1, c_mid, jnp.where(j8 == 2, c_lo,
                                                                                     jnp.where(j8 < 6, 1.0, 0.0))))
            ck = jnp.where(j8 < 3, 1.0, jnp.where(j8 == 3, -c_hi, jnp.where(j8 == 4, -c_mid,
                                                                               jnp.where(j8 == 5, -c_lo, 0.0))))
            v["ck_t"] = jnp.concatenate([ck, jnp.zeros((LANES - F_ROWS, tm), _F32)], axis=0).T

        def st_sgu():
            lane = lax.broadcasted_iota(jnp.int32, (CHUNK, LANES), 1)
            row_s = lax.broadcasted_iota(jnp.int32, (CHUNK, 2 * CHUNK), 0)
            col_s = lax.broadcasted_iota(jnp.int32, (CHUNK, 2 * CHUNK), 1) % CHUNK
            z_pairs = []
            for p in range(PAIRS):
                w_pair = jnp.where(col_s <= row_s, ws_ref[p], 0.0).astype(_BF16)
                z_chunks = []
                for c_i in range(tm // CHUNK):
                    v_pair = v["vn"][c_i * CHUNK:(c_i + 1) * CHUNK, p * LANES:(p + 1) * LANES]
                    rhs = jnp.concatenate([jnp.where(lane < HEAD_DIM, v_pair, 0.0),
                                           jnp.where(lane >= HEAD_DIM, v_pair, 0.0)], axis=0).astype(_BF16)
                    z_chunks.append(_dot(w_pair, rhs) + bs_ref[:, p * LANES:(p + 1) * LANES])
                z_pairs.append(jnp.concatenate(z_chunks, axis=0))
            v["z"] = jnp.concatenate(z_pairs, axis=1)

        def st_ya():
            ya_ref[0, rows, :] = _rms(v["u"] * v["z"], goa_ref[...]).astype(_BF16)

        def st_q_tables():
            zeros_q = jnp.zeros((LANES - HEAD_DIM - SUBLANES, tm), _F32)
            for hd in range(B_HEADS):
                q_h = v["q_t"][hd * HEAD_DIM:(hd + 1) * HEAD_DIM]
                cq_h = v["cq"][SUBLANES * hd:SUBLANES * (hd + 1)]
                q_aug = [q_h, cq_h, zeros_q] if hd % 2 == 0 else [cq_h, zeros_q, q_h]
                qt_ref[0, hd, local] = jnp.concatenate(q_aug, axis=0).astype(_BF16)

        def st_k_tables(heads):
            lane_t = lax.broadcasted_iota(jnp.int32, (tm, LANES), 1)
            for hd in heads:
                k_pair = v["k"][:, (hd // 2) * LANES:(hd // 2 + 1) * LANES]
                if hd % 2 == 0:
                    k_aug = jnp.where(lane_t < HEAD_DIM, k_pair, pltpu.roll(v["ck_t"], HEAD_DIM - SUBLANES * hd, 1))
                else:
                    k_aug = jnp.where(lane_t >= HEAD_DIM, k_pair, pltpu.roll(v["ck_t"], LANES - SUBLANES * hd, 1))
                ka_ref[0, hd, rows, :] = k_aug.astype(_BF16)

        def st_mem_qk(hh):
            v["s_m", hh] = _dot(km_ref[hh], v["qm_t"])

        def st_mem_softmax(hh):
            s = v.pop(("s_m", hh))
            e = jnp.exp2(s - jnp.max(s, axis=0, keepdims=True))
            v["p_m", hh] = (e * (1.0 / jnp.sum(e, axis=0, keepdims=True))).astype(_BF16)

        def st_mem_pv(hh):
            part = _dot(vmt_ref[hh], v.pop(("p_m", hh)))
            v["o_t"] = part if hh == 0 else v["o_t"] + part

        def st_ym():
            o_t = v["o_t"]
            ym_t = o_t * lax.rsqrt(jnp.mean(o_t * o_t, axis=0, keepdims=True) + EPS)
            ym_ref[0, rows, :] = (ym_t.T * gom_ref[...]).astype(_BF16)

        P = functools.partial
        return (st_h, st_zu, st_zv, st_qm, st_u, st_vt, st_f, st_vn, st_qt, st_vt_store,
                P(st_mem_qk, 0), P(st_mem_qk, 1), st_gate_parts, st_sgu, st_gate_dot, P(st_mem_softmax, 0),
                st_k, P(st_mem_qk, 2), st_ya, P(st_mem_pv, 0), P(st_mem_softmax, 1), st_gate_bias,
                P(st_mem_qk, 3), P(st_mem_pv, 1), P(st_mem_softmax, 2), st_q_tables, P(st_mem_pv, 2),
                P(st_mem_softmax, 3), P(st_k_tables, (0, 1, 2)), P(st_mem_pv, 3), P(st_k_tables, (3, 4, 5)),
                st_ym)

    for local in range(MIX_TILES):
        for stage in tile_stages(local):
            stage()
    carry_ref[...] = carry["c"]


def _mix_in(x, g_pre, w_nn, w_t, b_f_col, g_sgu, ws_cat, bs_exp, g_out_a, g_out_m, mem, g_mem, w_mk, w_mv_t,
            w_out, layer):
    batch, seq, _ = x.shape
    tm = T_ATT
    nblk = seq // tm
    rows = MIX_TILES * tm
    steps = nblk // MIX_TILES
    slab = D_MODEL // (batch * steps)
    assert slab * batch * steps == D_MODEL and slab % 16 == 0
    const2 = lambda b, i: (0, 0)
    return pl.pallas_call(
        _mix_in_kernel,
        grid=(batch, nblk // MIX_TILES),
        in_specs=[
            pl.BlockSpec((1, rows, D_MODEL), lambda b, i: (b, i, 0)),
            pl.BlockSpec((1, D_MODEL), const2),
            pl.BlockSpec((D_MODEL, NN_COLS), const2),
            pl.BlockSpec((NT_ROWS, D_MODEL), const2),
            pl.BlockSpec((F_ROWS, 1), const2),
            pl.BlockSpec((1, A_WIDTH), const2),
            pl.BlockSpec((PAIRS, CHUNK, 2 * CHUNK), lambda b, i: (0, 0, 0)),
            pl.BlockSpec((CHUNK, A_WIDTH), const2),
            pl.BlockSpec((1, A_WIDTH), const2),
            pl.BlockSpec((1, M_WIDTH), const2),
            pl.BlockSpec((1, MEM_TOKENS, D_MODEL), lambda b, i: (b, 0, 0)),
            pl.BlockSpec((1, D_MODEL), const2),
            pl.BlockSpec((D_MODEL, M_WIDTH), const2),
            pl.BlockSpec((M_WIDTH, D_MODEL), const2),
            pl.BlockSpec((1, slab, D_MODEL), lambda b, i: (layer, b * steps + i, 0)),
        ],
        out_specs=[
            pl.BlockSpec((1, B_HEADS, MIX_TILES, LANES, tm), lambda b, i: (b, 0, i, 0, 0)),
            pl.BlockSpec((1, B_HEADS, rows, LANES), lambda b, i: (b, 0, i, 0)),
            pl.BlockSpec((1, B_HEADS, MIX_TILES, VT_ROWS, tm), lambda b, i: (b, 0, i, 0, 0)),
            pl.BlockSpec((1, rows, A_WIDTH), lambda b, i: (b, i, 0)),
            pl.BlockSpec((1, rows, M_WIDTH), lambda b, i: (b, i, 0)),
            pl.BlockSpec((slab, D_MODEL), lambda b, i: (b * steps + i, 0)),
        ],
        out_shape=[jax.ShapeDtypeStruct((batch, B_HEADS, nblk, LANES, tm), _BF16),
                   jax.ShapeDtypeStruct((batch, B_HEADS, seq, LANES), _BF16),
                   jax.ShapeDtypeStruct((batch, B_HEADS, nblk, VT_ROWS, tm), _BF16),
                   jax.ShapeDtypeStruct((batch, seq, A_WIDTH), _BF16),
                   jax.ShapeDtypeStruct((batch, seq, M_WIDTH), _BF16),
                   jax.ShapeDtypeStruct((D_MODEL, D_MODEL), _BF16)],
        scratch_shapes=[pltpu.VMEM((F_ROWS, 1), _F32),
                        pltpu.VMEM((M_HEADS, MEM_TOKENS, M_WIDTH), _BF16),
                        pltpu.VMEM((M_HEADS, M_WIDTH, MEM_TOKENS), _BF16)],
        compiler_params=pltpu.CompilerParams(dimension_semantics=("parallel", "arbitrary"),
                                             vmem_limit_bytes=VMEM_LIMIT),
        name="mix_in",
    )(x, g_pre, w_nn, w_t, b_f_col, g_sgu, ws_cat, bs_exp, g_out_a, g_out_m, mem, g_mem, w_mk, w_mv_t, w_out)


def _fox_out_kernel(qt_ref, ka_ref, vt_ref, ya_ref, ym_ref, x_ref, wout_ref, gob_ref, gpost_ref,
                    wg_ref, wu_ref, wd_ref, x1_ref, wg16_ref, wu16_ref, wd16_ref):
    t = T_ATT
    wg16_ref[...] = wg_ref[0].astype(_BF16)
    wu16_ref[...] = wu_ref[0].astype(_BF16)
    wd16_ref[...] = wd_ref[0].astype(_BF16)
    causal = (lax.broadcasted_iota(jnp.int32, (t, t), 0) <= lax.broadcasted_iota(jnp.int32, (t, t), 1))

    w = t // Q_SPLIT
    chains = [(hd, qh) for hd in range(B_HEADS) for qh in range(Q_SPLIT)]

    def attend(local, j, carry, diag):
        start = j * t

        def qk(c):
            hd, qh = chains[c]
            nk = (qh + 1) * w if diag else t
            return _dot(ka_ref[0, hd, pl.ds(start, nk), :], qt_ref[0, hd, local, :, qh * w:(qh + 1) * w])

        scores = {c: qk(c) for c in range(QK_AHEAD)}
        out = []
        for c, (hd, qh) in enumerate(chains):
            m, acc = carry[c]
            s = scores.pop(c)
            nk = s.shape[0]
            if diag:
                s = jnp.where(causal[0:nk, qh * w:(qh + 1) * w], s, NEG_INF)
            m_new = jnp.maximum(m, jnp.max(s, axis=0, keepdims=True))
            p = jnp.exp2(s - m_new).astype(_BF16)
            if c + QK_AHEAD < len(chains):
                scores[c + QK_AHEAD] = qk(c + QK_AHEAD)
            out.append((m_new, jnp.exp2(m - m_new) * acc + _dot(vt_ref[0, hd, j, :, 0:nk], p)))
        return tuple(out)

    def query_tile(local, n_off):
        rows = slice(local * t, (local + 1) * t)
        z_am = (_dot(ya_ref[0, rows, :], wout_ref[0:A_WIDTH, :])
                + _dot(ym_ref[0, rows, :], wout_ref[A_WIDTH + B_WIDTH:, :]))
        state = tuple((jnp.full((1, w), NEG_INF, _F32), jnp.zeros((VT_ROWS, w), _F32)) for _ in chains)
        for j in range(n_off):
            state = attend(local, j, state, False)
        state = attend(local, n_off, state, True)
        heads = []
        for hd in range(B_HEADS):
            acc = jnp.concatenate([state[hd * Q_SPLIT + qh][1] for qh in range(Q_SPLIT)], axis=1)
            heads.append(acc[0:HEAD_DIM] * (1.0 / acc[HEAD_DIM:HEAD_DIM + 1]))
        yb_t = jnp.concatenate(heads, axis=0)
        yb_t = yb_t * lax.rsqrt(jnp.mean(yb_t * yb_t, axis=0, keepdims=True) + EPS)
        yb_n = (yb_t.T * gob_ref[...]).astype(_BF16)
        z = z_am + _dot(yb_n, wout_ref[A_WIDTH:A_WIDTH + B_WIDTH, :])
        x1_ref[0, rows, :] = x_ref[0, rows, :] + _rms(z, gpost_ref[...])

    def step(first_tile):
        for local in range(FOX_TILES):
            query_tile(local, first_tile + local)

    for s in range(ka_ref.shape[2] // (t * FOX_TILES)):
        pl.when(pl.program_id(1) == s)(functools.partial(step, s * FOX_TILES))


def _fox_out(qt, ka, vt, ya, ym, x, w_out, g_out_b, g_post, w_gate, w_up, w_down, layer):
    batch, seq, _ = x.shape
    t = T_ATT
    nblk = seq // t
    rows = FOX_TILES * t
    steps = nblk // FOX_TILES
    slab = D_MODEL // (batch * steps)
    slab_d = D_FF // batch
    assert slab * batch * steps == D_MODEL and slab % 16 == 0 and slab_d * batch == D_FF and slab_d % 16 == 0
    const2 = lambda b, i: (0, 0)
    slab_spec = pl.BlockSpec((slab, D_FF), lambda b, i: (b * steps + i, 0))
    slab_d_spec = pl.BlockSpec((slab_d, D_MODEL), lambda b, i: (b, 0))
    slab_in = pl.BlockSpec((1, slab, D_FF), lambda b, i: (layer, b * steps + i, 0))
    slab_d_in = pl.BlockSpec((1, slab_d, D_MODEL), lambda b, i: (layer, b, 0))
    w16 = jax.ShapeDtypeStruct((D_MODEL, D_FF), _BF16)
    return pl.pallas_call(
        _fox_out_kernel,
        grid=(batch, nblk // FOX_TILES),
        in_specs=[
            pl.BlockSpec((1, B_HEADS, FOX_TILES, LANES, t), lambda b, i: (b, 0, i, 0, 0)),
            pl.BlockSpec((1, B_HEADS, seq, LANES), lambda b, i: (b, 0, 0, 0)),
            pl.BlockSpec((1, B_HEADS, nblk, VT_ROWS, t), lambda b, i: (b, 0, 0, 0, 0)),
            pl.BlockSpec((1, rows, A_WIDTH), lambda b, i: (b, i, 0)),
            pl.BlockSpec((1, rows, M_WIDTH), lambda b, i: (b, i, 0)),
            pl.BlockSpec((1, rows, D_MODEL), lambda b, i: (b, i, 0)),
            pl.BlockSpec((D_MODEL, D_MODEL), const2),
            pl.BlockSpec((1, B_WIDTH), const2),
            pl.BlockSpec((1, D_MODEL), const2),
            slab_in, slab_in, slab_d_in,
        ],
        out_specs=[pl.BlockSpec((1, rows, D_MODEL), lambda b, i: (b, i, 0)), slab_spec, slab_spec, slab_d_spec],
        out_shape=[jax.ShapeDtypeStruct(x.shape, x.dtype), w16, w16,
                   jax.ShapeDtypeStruct((D_FF, D_MODEL), _BF16)],
        compiler_params=pltpu.CompilerParams(dimension_semantics=("parallel", "arbitrary"),
                                             vmem_limit_bytes=VMEM_LIMIT),
        name="fox_out",
    )(qt, ka, vt, ya, ym, x, w_out, g_out_b, g_post, w_gate, w_up, w_down)


def _ffn_kernel(x_ref, gpre_ref, wg_ref, wu_ref, wd_ref, gpost_ref, o_ref):
    step = D_FF // FF_SPLIT
    rows = x_ref.shape[0] // FF_ROW_SPLIT

    def pre_norm(r):
        x = x_ref[r * rows:(r + 1) * rows, :]
        return x, _rms(x, gpre_ref[...]).astype(_BF16)

    def finish(r, x, ff):
        o_ref[r * rows:(r + 1) * rows, :] = x + _rms(ff, gpost_ref[...])

    xs, hs = zip(*[pre_norm(r) for r in range(FF_ROW_SPLIT)])
    ffs = [None] * FF_ROW_SPLIT
    for c in range(FF_SPLIT):
        sl = slice(c * step, (c + 1) * step)
        for r in range(FF_ROW_SPLIT):
            g = _dot(hs[r], wg_ref[:, sl])
            a = (g * (1.0 / (1.0 + jnp.exp(-g))) * _dot(hs[r], wu_ref[:, sl])).astype(_BF16)
            part = _dot(a, wd_ref[sl, :])
            ffs[r] = part if ffs[r] is None else ffs[r] + part
            if c == FF_SPLIT - 1:
                finish(r, xs[r], ffs[r])


def _ffn(x, g_pre, w_gate, w_up, w_down, g_post):
    tokens = x.shape[0]
    tm = TM_FFN
    const = lambda i: (0, 0)
    resident = functools.partial(pl.BlockSpec, index_map=const, pipeline_mode=pl.Buffered(1))
    return pl.pallas_call(
        _ffn_kernel,
        grid=(tokens // tm,),
        in_specs=[
            pl.BlockSpec((tm, D_MODEL), lambda i: (i, 0)),
            pl.BlockSpec((1, D_MODEL), const),
            resident((D_MODEL, D_FF)),
            resident((D_MODEL, D_FF)),
            resident((D_FF, D_MODEL)),
            pl.BlockSpec((1, D_MODEL), const),
        ],
        out_specs=pl.BlockSpec((tm, D_MODEL), lambda i: (i, 0)),
        out_shape=jax.ShapeDtypeStruct(x.shape, x.dtype),
        compiler_params=pltpu.CompilerParams(dimension_semantics=("parallel",),
                                             vmem_limit_bytes=VMEM_LIMIT),
        name="ffn",
    )(x, g_pre, w_gate, w_up, w_down, g_post)


def kernel(x, mem, g_pre_mix, w_in, b_f, g_sgu, w_s, b_s, g_out_a, g_out_b, g_out_m, g_mem, w_mem_kv, w_out,
           g_post_mix, g_pre_ffn, w_gate, w_up, w_down, g_post_ffn):
    batch, seq, d = x.shape
    depth = w_in.shape[0]
    row = lambda a: a.reshape(1, -1)
    q_lo = 2 * A_WIDTH
    f_lo = q_lo + 3 * B_WIDTH
    for l in range(depth):
        w = w_in[l]
        w_nn = jnp.concatenate([w[:, :q_lo], w[:, q_lo + B_WIDTH:q_lo + 2 * B_WIDTH]],
                               axis=1).astype(_BF16)
        w_f = jnp.pad(jnp.repeat(w[:, f_lo:f_lo + B_HEADS], SUBLANES, axis=1),
                      ((0, 0), (0, F_ROWS - SUBLANES * B_HEADS)))
        w_t = jnp.concatenate([w[:, q_lo:q_lo + B_WIDTH], w[:, q_lo + 2 * B_WIDTH:f_lo], w_f, w[:, f_lo + B_HEADS:]],
                              axis=1).T.astype(_BF16)
        b_f_col = jnp.pad(jnp.repeat(b_f[l], SUBLANES), (0, F_ROWS - SUBLANES * B_HEADS)).reshape(F_ROWS, 1)
        ws_cat = w_s[l].reshape(PAIRS, 2, CHUNK, CHUNK).transpose(0, 2, 1, 3).reshape(PAIRS, CHUNK, 2 * CHUNK)
        bs_exp = jnp.repeat(b_s[l].T, HEAD_DIM, axis=1)

        qt, ka, vt, ya, ym, wo16 = _mix_in(
            x, row(g_pre_mix[l]), w_nn, w_t, b_f_col, row(g_sgu[l]), ws_cat, bs_exp, row(g_out_a[l]),
            row(g_out_m[l]), mem, row(g_mem[l]), w_mem_kv[l][:, :M_WIDTH].astype(_BF16),
            w_mem_kv[l][:, M_WIDTH:].T.astype(_BF16), w_out, l)
        x, wg16, wu16, wd16 = _fox_out(qt, ka, vt, ya, ym, x, wo16, row(g_out_b[l]),
                                       row(g_post_mix[l]), w_gate, w_up, w_down, l)
        x = _ffn(x.reshape(batch * seq, d), row(g_pre_ffn[l]), wg16, wu16, wd16,
                 row(g_post_ffn[l])).reshape(batch, seq, d)
    return x
```

```python
import functools

import jax
import jax.numpy as jnp
from jax import lax
from jax.experimental import pallas as pl
from jax.experimental.pallas import tpu as pltpu

D_MODEL = 1024
HEAD_DIM = 64
A_GROUPS = 6
B_HEADS = 6
M_HEADS = 4
A_WIDTH = A_GROUPS * HEAD_DIM
B_WIDTH = B_HEADS * HEAD_DIM
M_WIDTH = M_HEADS * HEAD_DIM
CHUNK = 128
MEM_TOKENS = 256
D_FF = 2816
EPS = 1e-6
NEG_INF = -1e30

LANES = 128
SUBLANES = 8
PAIRS = B_HEADS // 2
SCALE = HEAD_DIM ** -0.5
LOG2E = 1.4426950408889634

K_OFF = 2 * A_WIDTH
NN_COLS = K_OFF + B_WIDTH
V_ROW = B_WIDTH
F_ROW = 2 * B_WIDTH
F_ROWS = 64
QM_ROW = F_ROW + F_ROWS
NT_ROWS = QM_ROW + M_WIDTH
VT_ROWS = HEAD_DIM + 16
OUT_HALF = (A_WIDTH + B_WIDTH + M_WIDTH) // 2

T_ATT = 512
MIX_TILES = 4
FOX_TILES = 2
Q_SPLIT = 2
QK_AHEAD = 6
TM_FFN = 1024
FF_ROW_SPLIT = 2
FF_SPLIT = 11

VMEM_LIMIT = 56 * 1024 * 1024

_F32 = jnp.float32
_BF16 = jnp.bfloat16


def _dot(a, b):
    return jnp.dot(a, b, preferred_element_type=_F32)


def _dot_nt(a, b):
    return lax.dot_general(a, b, (((1,), (1,)), ((), ())), preferred_element_type=_F32)


def _rms(x, g):
    return x * lax.rsqrt(jnp.mean(x * x, axis=-1, keepdims=True) + EPS) * g


def _gelu_tanh(x):
    return 0.5 * x * (1.0 + jnp.tanh(0.7978845608028654 * (x + 0.044715 * (x * x * x))))


def _log_sigmoid(x):
    return -(jnp.maximum(-x, 0.0) + jnp.log1p(jnp.exp(-jnp.abs(x))))


def _split3(x):
    hi = x.astype(_BF16).astype(_F32)
    r = x - hi
    mid = r.astype(_BF16).astype(_F32)
    lo = (r - mid).astype(_BF16).astype(_F32)
    return hi, mid, lo


def _mix_in_kernel(x_ref, gpre_ref, wnn_ref, wt_ref, bf_ref, gsgu_ref, ws_ref, bs_ref, goa_ref, gom_ref,
                   mem_ref, gmem_ref, wmk_ref, wmvt_ref, wout_ref, qt_ref, ka_ref, vt_ref, ya_ref, ym_ref,
                   wout16_ref, carry_ref, km_ref, vmt_ref):
    tm = T_ATT
    wout16_ref[...] = wout_ref[0].astype(_BF16)

    mn = _rms(mem_ref[0], gmem_ref[...]).astype(_BF16)
    km = _dot(mn, wmk_ref[...])
    vm_t = _dot_nt(wmvt_ref[...], mn)
    head_k = lax.broadcasted_iota(jnp.int32, (MEM_TOKENS, M_WIDTH), 1) // HEAD_DIM
    head_v = lax.broadcasted_iota(jnp.int32, (M_WIDTH, MEM_TOKENS), 0) // HEAD_DIM
    for hh in range(M_HEADS):
        km_ref[hh] = jnp.where(head_k == hh, km, 0.0).astype(_BF16)
        vmt_ref[hh] = jnp.where(head_v == hh, vm_t, 0.0).astype(_BF16)

    @pl.when(pl.program_id(1) == 0)
    def _():
        carry_ref[...] = jnp.zeros_like(carry_ref)

    carry = {"c": carry_ref[...]}

    def tile_stages(local):
        rows = slice(local * tm, (local + 1) * tm)
        v = {}

        def st_h():
            v["h"] = _rms(x_ref[0, rows, :], gpre_ref[...]).astype(_BF16)

        def st_nn():
            t_nn = _dot(v["h"], wnn_ref[...])
            v["t_zu"], v["t_zv"], v["k"] = t_nn[:, 0:A_WIDTH], t_nn[:, A_WIDTH:K_OFF], t_nn[:, K_OFF:NN_COLS]

        def st_qt():
            v["q_t"] = _dot_nt(wt_ref[0:B_WIDTH, :], v["h"]) * (SCALE * LOG2E)

        def st_nt():
            t_t = _dot_nt(wt_ref[V_ROW:NT_ROWS, :], v["h"])
            v["v_t"], v["t_f"] = t_t[0:F_ROW - V_ROW], t_t[F_ROW - V_ROW:QM_ROW - V_ROW]
            v["qm_t"] = (t_t[QM_ROW - V_ROW:] * (SCALE * LOG2E)).astype(_BF16)

        def st_u():
            v["u"] = _gelu_tanh(v["t_zu"])

        def st_vn():
            v["vn"] = _rms(_gelu_tanh(v["t_zv"]), gsgu_ref[...])

        def st_vt_store():
            ones_v = jnp.where(lax.broadcasted_iota(jnp.int32, (VT_ROWS - HEAD_DIM, tm), 0) == 0, 1.0, 0.0)
            for hd in range(B_HEADS):
                vt_ref[0, hd, local] = jnp.concatenate([v["v_t"][hd * HEAD_DIM:(hd + 1) * HEAD_DIM], ones_v],
                                                       axis=0).astype(_BF16)

        rowf = lax.broadcasted_iota(jnp.int32, (F_ROWS, tm), 0)

        def st_gate_parts():
            logf = _log_sigmoid(v["t_f"] + bf_ref[...])
            logf = jnp.where(rowf < SUBLANES * B_HEADS, logf, 0.0)
            v["parts"] = jnp.concatenate(_split3(logf), axis=0).astype(_BF16)

        def st_gate_dot():
            triu = (lax.broadcasted_iota(jnp.int32, (tm, tm), 0)
                    <= lax.broadcasted_iota(jnp.int32, (tm, tm), 1)).astype(_BF16)
            v["cs"] = _dot(v["parts"], triu)

        def st_gate_bias():
            cs = v["cs"]
            c = cs[0:F_ROWS] + cs[F_ROWS:2 * F_ROWS] + cs[2 * F_ROWS:] + carry["c"]
            carry["c"] = c[:, tm - 1:tm]
            c_hi, c_mid, c_lo = _split3(c * LOG2E)
            j8 = rowf % SUBLANES
            v["cq"] = jnp.where(j8 == 0, c_hi, jnp.where(j8 == 1, c_mid, jnp.where(j8 == 2, c_lo,
                                                                                     jnp.where(j8 < 6, 1.0, 0.0))))
            ck = jnp.where(j8 < 3, 1.0, jnp.where(j8 == 3, -c_hi, jnp.where(j8 == 4, -c_mid,
                                                                               jnp.where(j8 == 5, -c_lo, 0.0))))
            v["ck_t"] = jnp.concatenate([ck, jnp.zeros((LANES - F_ROWS, tm), _F32)], axis=0).T

        def st_sgu():
            lane = lax.broadcasted_iota(jnp.int32, (CHUNK, LANES), 1)
            row_s = lax.broadcasted_iota(jnp.int32, (CHUNK, 2 * CHUNK), 0)
            col_s = lax.broadcasted_iota(jnp.int32, (CHUNK, 2 * CHUNK), 1) % CHUNK
            z_pairs = []
            for p in range(PAIRS):
                w_pair = jnp.where(col_s <= row_s, ws_ref[p], 0.0).astype(_BF16)
                z_chunks = []
                for c_i in range(tm // CHUNK):
                    v_pair = v["vn"][c_i * CHUNK:(c_i + 1) * CHUNK, p * LANES:(p + 1) * LANES]
                    rhs = jnp.concatenate([jnp.where(lane < HEAD_DIM, v_pair, 0.0),
                                           jnp.where(lane >= HEAD_DIM, v_pair, 0.0)], axis=0).astype(_BF16)
                    z_chunks.append(_dot(w_pair, rhs) + bs_ref[:, p * LANES:(p + 1) * LANES])
                z_pairs.append(jnp.concatenate(z_chunks, axis=0))
            v["z"] = jnp.concatenate(z_pairs, axis=1)

        def st_ya():
            ya_ref[0, rows, :] = _rms(v["u"] * v["z"], goa_ref[...]).astype(_BF16)

        def st_q_tables():
            zeros_q = jnp.zeros((LANES - HEAD_DIM - SUBLANES, tm), _F32)
            for hd in range(B_HEADS):
                q_h = v["q_t"][hd * HEAD_DIM:(hd + 1) * HEAD_DIM]
                cq_h = v["cq"][SUBLANES * hd:SUBLANES * (hd + 1)]
                q_aug = [q_h, cq_h, zeros_q] if hd % 2 == 0 else [cq_h, zeros_q, q_h]
                qt_ref[0, hd, local] = jnp.concatenate(q_aug, axis=0).astype(_BF16)

        def st_k_tables(heads):
            lane_t = lax.broadcasted_iota(jnp.int32, (tm, LANES), 1)
            for hd in heads:
                k_pair = v["k"][:, (hd // 2) * LANES:(hd // 2 + 1) * LANES]
                if hd % 2 == 0:
                    k_aug = jnp.where(lane_t < HEAD_DIM, k_pair, pltpu.roll(v["ck_t"], HEAD_DIM - SUBLANES * hd, 1))
                else:
                    k_aug = jnp.where(lane_t >= HEAD_DIM, k_pair, pltpu.roll(v["ck_t"], LANES - SUBLANES * hd, 1))
                ka_ref[0, hd, rows, :] = k_aug.astype(_BF16)

        def st_mem_qk(hh):
            v["s_m", hh] = _dot(km_ref[hh], v["qm_t"])

        def st_mem_softmax(hh):
            s = v.pop(("s_m", hh))
            e = jnp.exp2(s - jnp.max(s, axis=0, keepdims=True))
            v["p_m", hh] = (e * (1.0 / jnp.sum(e, axis=0, keepdims=True))).astype(_BF16)

        def st_mem_pv(hh):
            part = _dot(vmt_ref[hh], v.pop(("p_m", hh)))
            v["o_t"] = part if hh == 0 else v["o_t"] + part

        def st_ym():
            o_t = v["o_t"]
            ym_t = o_t * lax.rsqrt(jnp.mean(o_t * o_t, axis=0, keepdims=True) + EPS)
            ym_ref[0, rows, :] = (ym_t.T * gom_ref[...]).astype(_BF16)

        P = functools.partial
        return (st_h, st_nn, st_nt, st_u, st_vn, st_qt, st_vt_store,
                P(st_mem_qk, 0), P(st_mem_qk, 1), st_gate_parts, st_sgu, st_gate_dot, P(st_mem_softmax, 0),
                P(st_mem_qk, 2), st_ya, P(st_mem_pv, 0), P(st_mem_softmax, 1), st_gate_bias,
                P(st_mem_qk, 3), P(st_mem_pv, 1), P(st_mem_softmax, 2), st_q_tables, P(st_mem_pv, 2),
                P(st_mem_softmax, 3), P(st_k_tables, (0, 1, 2)), P(st_mem_pv, 3), P(st_k_tables, (3, 4, 5)),
                st_ym)

    for local in range(MIX_TILES):
        for stage in tile_stages(local):
            stage()
    carry_ref[...] = carry["c"]


def _mix_in(x, g_pre, w_nn, w_t, b_f_col, g_sgu, ws_cat, bs_exp, g_out_a, g_out_m, mem, g_mem, w_mk, w_mv_t,
            w_out, layer):
    batch, seq, _ = x.shape
    tm = T_ATT
    nblk = seq // tm
    rows = MIX_TILES * tm
    steps = nblk // MIX_TILES
    slab = D_MODEL // (batch * steps)
    assert slab * batch * steps == D_MODEL and slab % 16 == 0
    m_lo = OUT_HALF - A_WIDTH
    assert A_WIDTH % slab == 0 and m_lo % slab == 0 and B_WIDTH % slab == 0
    e_a, e_m, e_b = A_WIDTH // slab, OUT_HALF // slab, (OUT_HALF + B_WIDTH) // slab

    def regroup(dst):
        return jnp.where(dst < e_a, dst, jnp.where(dst < e_m, dst + B_WIDTH // slab,
                                                   jnp.where(dst < e_b, dst - m_lo // slab, dst)))

    const2 = lambda b, i: (0, 0)
    return pl.pallas_call(
        _mix_in_kernel,
        grid=(batch, nblk // MIX_TILES),
        in_specs=[
            pl.BlockSpec((1, rows, D_MODEL), lambda b, i: (b, i, 0)),
            pl.BlockSpec((1, D_MODEL), const2),
            pl.BlockSpec((D_MODEL, NN_COLS), const2),
            pl.BlockSpec((NT_ROWS, D_MODEL), const2),
            pl.BlockSpec((F_ROWS, 1), const2),
            pl.BlockSpec((1, A_WIDTH), const2),
            pl.BlockSpec((PAIRS, CHUNK, 2 * CHUNK), lambda b, i: (0, 0, 0)),
            pl.BlockSpec((CHUNK, A_WIDTH), const2),
            pl.BlockSpec((1, A_WIDTH), const2),
            pl.BlockSpec((1, M_WIDTH), const2),
            pl.BlockSpec((1, MEM_TOKENS, D_MODEL), lambda b, i: (b, 0, 0)),
            pl.BlockSpec((1, D_MODEL), const2),
            pl.BlockSpec((D_MODEL, M_WIDTH), const2),
            pl.BlockSpec((M_WIDTH, D_MODEL), const2),
            pl.BlockSpec((1, slab, D_MODEL), lambda b, i: (layer, regroup(b * steps + i), 0)),
        ],
        out_specs=[
            pl.BlockSpec((1, B_HEADS, MIX_TILES, LANES, tm), lambda b, i: (b, 0, i, 0, 0)),
            pl.BlockSpec((1, B_HEADS, rows, LANES), lambda b, i: (b, 0, i, 0)),
            pl.BlockSpec((1, B_HEADS, MIX_TILES, VT_ROWS, tm), lambda b, i: (b, 0, i, 0, 0)),
            pl.BlockSpec((1, rows, A_WIDTH), lambda b, i: (b, i, 0)),
            pl.BlockSpec((1, rows, M_WIDTH), lambda b, i: (b, i, 0)),
            pl.BlockSpec((slab, D_MODEL), lambda b, i: (b * steps + i, 0)),
        ],
        out_shape=[jax.ShapeDtypeStruct((batch, B_HEADS, nblk, LANES, tm), _BF16),
                   jax.ShapeDtypeStruct((batch, B_HEADS, seq, LANES), _BF16),
                   jax.ShapeDtypeStruct((batch, B_HEADS, nblk, VT_ROWS, tm), _BF16),
                   jax.ShapeDtypeStruct((batch, seq, A_WIDTH), _BF16),
                   jax.ShapeDtypeStruct((batch, seq, M_WIDTH), _BF16),
                   jax.ShapeDtypeStruct((D_MODEL, D_MODEL), _BF16)],
        scratch_shapes=[pltpu.VMEM((F_ROWS, 1), _F32),
                        pltpu.VMEM((M_HEADS, MEM_TOKENS, M_WIDTH), _BF16),
                        pltpu.VMEM((M_HEADS, M_WIDTH, MEM_TOKENS), _BF16)],
        compiler_params=pltpu.CompilerParams(dimension_semantics=("parallel", "arbitrary"),
                                             vmem_limit_bytes=VMEM_LIMIT),
        name="mix_in",
    )(x, g_pre, w_nn, w_t, b_f_col, g_sgu, ws_cat, bs_exp, g_out_a, g_out_m, mem, g_mem, w_mk, w_mv_t, w_out)


def _fox_out_kernel(qt_ref, ka_ref, vt_ref, ya_ref, ym_ref, x_ref, wout_ref, gob_ref, gpost_ref,
                    wg_ref, wu_ref, wd_ref, x1_ref, wg16_ref, wu16_ref, wd16_ref):
    t = T_ATT
    wg16_ref[...] = wg_ref[0].astype(_BF16)
    wu16_ref[...] = wu_ref[0].astype(_BF16)
    wd16_ref[...] = wd_ref[0].astype(_BF16)
    causal = (lax.broadcasted_iota(jnp.int32, (t, t), 0) <= lax.broadcasted_iota(jnp.int32, (t, t), 1))

    w = t // Q_SPLIT
    chains = [(hd, qh) for hd in range(B_HEADS) for qh in range(Q_SPLIT)]

    def attend(local, j, carry, diag):
        start = j * t

        def qk(c):
            hd, qh = chains[c]
            nk = (qh + 1) * w if diag else t
            return _dot(ka_ref[0, hd, pl.ds(start, nk), :], qt_ref[0, hd, local, :, qh * w:(qh + 1) * w])

        scores = {c: qk(c) for c in range(QK_AHEAD)}
        out = []
        for c, (hd, qh) in enumerate(chains):
            m, acc = carry[c]
            s = scores.pop(c)
            nk = s.shape[0]
            if diag:
                s = jnp.where(causal[0:nk, qh * w:(qh + 1) * w], s, NEG_INF)
            m_new = jnp.maximum(m, jnp.max(s, axis=0, keepdims=True))
            p = jnp.exp2(s - m_new).astype(_BF16)
            if c + QK_AHEAD < len(chains):
                scores[c + QK_AHEAD] = qk(c + QK_AHEAD)
            out.append((m_new, jnp.exp2(m - m_new) * acc + _dot(vt_ref[0, hd, j, :, 0:nk], p)))
        return tuple(out)

    def query_tile(local, n_off):
        rows = slice(local * t, (local + 1) * t)
        y_am = jnp.concatenate([ya_ref[0, rows, :], ym_ref[0, rows, 0:OUT_HALF - A_WIDTH]], axis=1)
        z_am = _dot(y_am, wout_ref[0:OUT_HALF, :])
        state = tuple((jnp.full((1, w), NEG_INF, _F32), jnp.zeros((VT_ROWS, w), _F32)) for _ in chains)
        for j in range(n_off):
            state = attend(local, j, state, False)
        state = attend(local, n_off, state, True)
        heads = []
        for hd in range(B_HEADS):
            acc = jnp.concatenate([state[hd * Q_SPLIT + qh][1] for qh in range(Q_SPLIT)], axis=1)
            heads.append(acc[0:HEAD_DIM] * (1.0 / acc[HEAD_DIM:HEAD_DIM + 1]))
        yb_t = jnp.concatenate(heads, axis=0)
        yb_t = yb_t * lax.rsqrt(jnp.mean(yb_t * yb_t, axis=0, keepdims=True) + EPS)
        yb_n = (yb_t.T * gob_ref[...]).astype(_BF16)
        y_bm = jnp.concatenate([yb_n, ym_ref[0, rows, OUT_HALF - A_WIDTH:]], axis=1)
        z = z_am + _dot(y_bm, wout_ref[OUT_HALF:, :])
        x1_ref[0, rows, :] = x_ref[0, rows, :] + _rms(z, gpost_ref[...])

    def step(first_tile):
        for local in range(FOX_TILES):
            query_tile(local, first_tile + local)

    for s in range(ka_ref.shape[2] // (t * FOX_TILES)):
        pl.when(pl.program_id(1) == s)(functools.partial(step, s * FOX_TILES))


def _fox_out(qt, ka, vt, ya, ym, x, w_out, g_out_b, g_post, w_gate, w_up, w_down, layer):
    batch, seq, _ = x.shape
    t = T_ATT
    nblk = seq // t
    rows = FOX_TILES * t
    steps = nblk // FOX_TILES
    slab = D_MODEL // (batch * steps)
    slab_d = D_FF // batch
    assert slab * batch * steps == D_MODEL and slab % 16 == 0 and slab_d * batch == D_FF and slab_d % 16 == 0
    const2 = lambda b, i: (0, 0)
    slab_spec = pl.BlockSpec((slab, D_FF), lambda b, i: (b * steps + i, 0))
    slab_d_spec = pl.BlockSpec((slab_d, D_MODEL), lambda b, i: (b, 0))
    slab_in = pl.BlockSpec((1, slab, D_FF), lambda b, i: (layer, b * steps + i, 0))
    slab_d_in = pl.BlockSpec((1, slab_d, D_MODEL), lambda b, i: (layer, b, 0))
    w16 = jax.ShapeDtypeStruct((D_MODEL, D_FF), _BF16)
    return pl.pallas_call(
        _fox_out_kernel,
        grid=(batch, nblk // FOX_TILES),
        in_specs=[
            pl.BlockSpec((1, B_HEADS, FOX_TILES, LANES, t), lambda b, i: (b, 0, i, 0, 0)),
            pl.BlockSpec((1, B_HEADS, seq, LANES), lambda b, i: (b, 0, 0, 0)),
            pl.BlockSpec((1, B_HEADS, nblk, VT_ROWS, t), lambda b, i: (b, 0, 0, 0, 0)),
            pl.BlockSpec((1, rows, A_WIDTH), lambda b, i: (b, i, 0)),
            pl.BlockSpec((1, rows, M_WIDTH), lambda b, i: (b, i, 0)),
            pl.BlockSpec((1, rows, D_MODEL), lambda b, i: (b, i, 0)),
            pl.BlockSpec((D_MODEL, D_MODEL), const2),
            pl.BlockSpec((1, B_WIDTH), const2),
            pl.BlockSpec((1, D_MODEL), const2),
            slab_in, slab_in, slab_d_in,
        ],
        out_specs=[pl.BlockSpec((1, rows, D_MODEL), lambda b, i: (b, i, 0)), slab_spec, slab_spec, slab_d_spec],
        out_shape=[jax.ShapeDtypeStruct(x.shape, x.dtype), w16, w16,
                   jax.ShapeDtypeStruct((D_FF, D_MODEL), _BF16)],
        compiler_params=pltpu.CompilerParams(dimension_semantics=("parallel", "arbitrary"),
                                             vmem_limit_bytes=VMEM_LIMIT),
        name="fox_out",
    )(qt, ka, vt, ya, ym, x, w_out, g_out_b, g_post, w_gate, w_up, w_down)


def _ffn_kernel(x_ref, gpre_ref, wg_ref, wu_ref, wd_ref, gpost_ref, o_ref):
    step = D_FF // FF_SPLIT
    rows = x_ref.shape[0] // FF_ROW_SPLIT

    def pre_norm(r):
        x = x_ref[r * rows:(r + 1) * rows, :]
        return x, _rms(x, gpre_ref[...]).astype(_BF16)

    def finish(r, x, ff):
        o_ref[r * rows:(r + 1) * rows, :] = x + _rms(ff, gpost_ref[...])

    xs, hs = zip(*[pre_norm(r) for r in range(FF_ROW_SPLIT)])
    ffs = [None] * FF_ROW_SPLIT
    for c in range(FF_SPLIT):
        sl = slice(c * step, (c + 1) * step)
        for r in range(FF_ROW_SPLIT):
            g = _dot(hs[r], wg_ref[:, sl])
            a = (g * (1.0 / (1.0 + jnp.exp(-g))) * _dot(hs[r], wu_ref[:, sl])).astype(_BF16)
            part = _dot(a, wd_ref[sl, :])
            ffs[r] = part if ffs[r] is None else ffs[r] + part
            if c == FF_SPLIT - 1:
                finish(r, xs[r], ffs[r])


def _ffn(x, g_pre, w_gate, w_up, w_down, g_post):
    tokens = x.shape[0]
    tm = TM_FFN
    const = lambda i: (0, 0)
    resident = functools.partial(pl.BlockSpec, index_map=const, pipeline_mode=pl.Buffered(1))
    return pl.pallas_call(
        _ffn_kernel,
        grid=(tokens // tm,),
        in_specs=[
            pl.BlockSpec((tm, D_MODEL), lambda i: (i, 0)),
            pl.BlockSpec((1, D_MODEL), const),
            resident((D_MODEL, D_FF)),
            resident((D_MODEL, D_FF)),
            resident((D_FF, D_MODEL)),
            pl.BlockSpec((1, D_MODEL), const),
        ],
        out_specs=pl.BlockSpec((tm, D_MODEL), lambda i: (i, 0)),
        out_shape=jax.ShapeDtypeStruct(x.shape, x.dtype),
        compiler_params=pltpu.CompilerParams(dimension_semantics=("parallel",),
                                             vmem_limit_bytes=VMEM_LIMIT),
        name="ffn",
    )(x, g_pre, w_gate, w_up, w_down, g_post)


def kernel(x, mem, g_pre_mix, w_in, b_f, g_sgu, w_s, b_s, g_out_a, g_out_b, g_out_m, g_mem, w_mem_kv, w_out,
           g_post_mix, g_pre_ffn, w_gate, w_up, w_down, g_post_ffn):
    batch, seq, d = x.shape
    depth = w_in.shape[0]
    row = lambda a: a.reshape(1, -1)
    q_lo = 2 * A_WIDTH
    f_lo = q_lo + 3 * B_WIDTH
    for l in range(depth):
        w = w_in[l]
        w_nn = jnp.concatenate([w[:, :q_lo], w[:, q_lo + B_WIDTH:q_lo + 2 * B_WIDTH]],
                               axis=1).astype(_BF16)
        w_f = jnp.pad(jnp.repeat(w[:, f_lo:f_lo + B_HEADS], SUBLANES, axis=1),
                      ((0, 0), (0, F_ROWS - SUBLANES * B_HEADS)))
        w_t = jnp.concatenate([w[:, q_lo:q_lo + B_WIDTH], w[:, q_lo + 2 * B_WIDTH:f_lo], w_f, w[:, f_lo + B_HEADS:]],
                              axis=1).T.astype(_BF16)
        b_f_col = jnp.pad(jnp.repeat(b_f[l], SUBLANES), (0, F_ROWS - SUBLANES * B_HEADS)).reshape(F_ROWS, 1)
        ws_cat = w_s[l].reshape(PAIRS, 2, CHUNK, CHUNK).transpose(0, 2, 1, 3).reshape(PAIRS, CHUNK, 2 * CHUNK)
        bs_exp = jnp.repeat(b_s[l].T, HEAD_DIM, axis=1)

        qt, ka, vt, ya, ym, wo16 = _mix_in(
            x, row(g_pre_mix[l]), w_nn, w_t, b_f_col, row(g_sgu[l]), ws_cat, bs_exp, row(g_out_a[l]),
            row(g_out_m[l]), mem, row(g_mem[l]), w_mem_kv[l][:, :M_WIDTH].astype(_BF16),
            w_mem_kv[l][:, M_WIDTH:].T.astype(_BF16), w_out, l)
        x, wg16, wu16, wd16 = _fox_out(qt, ka, vt, ya, ym, x, wo16, row(g_out_b[l]),
                                       row(g_post_mix[l]), w_gate, w_up, w_down, l)
        x = _ffn(x.reshape(batch * seq, d), row(g_pre_ffn[l]), wg16, wu16, wd16,
                 row(g_post_ffn[l])).reshape(batch, seq, d)
    return x
```

```python
import functools

import jax
import jax.numpy as jnp
from jax import lax
from jax.experimental import pallas as pl
from jax.experimental.pallas import tpu as pltpu

D_MODEL = 1024
HEAD_DIM = 64
A_GROUPS = 6
B_HEADS = 6
M_HEADS = 4
A_WIDTH = A_GROUPS * HEAD_DIM
B_WIDTH = B_HEADS * HEAD_DIM
M_WIDTH = M_HEADS * HEAD_DIM
CHUNK = 128
MEM_TOKENS = 256
D_FF = 2816
EPS = 1e-6
NEG_INF = -1e30

LANES = 128
SUBLANES = 8
BF16_ROWS = 16
PAIRS = B_HEADS // 2
SCALE = HEAD_DIM ** -0.5
LOG2E = 1.4426950408889634

K_OFF = 2 * A_WIDTH
NN_COLS = K_OFF + B_WIDTH
V_ROW = B_WIDTH
F_ROW = 2 * B_WIDTH
F_ROWS = 64
QM_ROW = F_ROW + F_ROWS
NT_ROWS = QM_ROW + M_WIDTH
VT_ROWS = HEAD_DIM + BF16_ROWS
OUT_HALF = (A_WIDTH + B_WIDTH + M_WIDTH) // 2

T_ATT = 512
MIX_TILES = 4
FOX_TILES = 2
Q_SPLIT = 2
QK_AHEAD = 6
TM_FFN = 1024
FF_ROW_SPLIT = 2
FF_SPLIT = 11

VMEM_LIMIT = 56 * 1024 * 1024

_F32 = jnp.float32
_BF16 = jnp.bfloat16


def _dot(a, b):
    return jnp.dot(a, b, preferred_element_type=_F32)


def _dot_nt(a, b):
    return lax.dot_general(a, b, (((1,), (1,)), ((), ())), preferred_element_type=_F32)


def _rms(x, g):
    return x * lax.rsqrt(jnp.mean(x * x, axis=-1, keepdims=True) + EPS) * g


def _gelu_tanh(x):
    return 0.5 * x * (1.0 + jnp.tanh(0.7978845608028654 * (x + 0.044715 * (x * x * x))))


def _log_sigmoid(x):
    return -(jnp.maximum(-x, 0.0) + jnp.log1p(jnp.exp(-jnp.abs(x))))


def _split3(x):
    hi = x.astype(_BF16).astype(_F32)
    r = x - hi
    mid = r.astype(_BF16).astype(_F32)
    lo = (r - mid).astype(_BF16).astype(_F32)
    return hi, mid, lo


def _mix_in_kernel(x_ref, gpre_ref, wnn_ref, wt_ref, bf_ref, gsgu_ref, ws_ref, bs_ref, goa_ref, gom_ref,
                   mem_ref, gmem_ref, wmk_ref, wmvt_ref, wout_ref, qt_ref, ka_ref, vt_ref, ya_ref, ym_ref,
                   wout16_ref, carry_ref, km_ref, vmt_ref):
    tm = T_ATT
    wout16_ref[...] = wout_ref[0].astype(_BF16)

    mn = _rms(mem_ref[0], gmem_ref[...]).astype(_BF16)
    km = _dot(mn, wmk_ref[...])
    vm_t = _dot_nt(wmvt_ref[...], mn)
    head_k = lax.broadcasted_iota(jnp.int32, (MEM_TOKENS, M_WIDTH), 1) // HEAD_DIM
    head_v = lax.broadcasted_iota(jnp.int32, (M_WIDTH, MEM_TOKENS), 0) // HEAD_DIM
    for hh in range(M_HEADS):
        km_ref[hh * MEM_TOKENS:(hh + 1) * MEM_TOKENS, :] = jnp.where(head_k == hh, km, 0.0).astype(_BF16)
        vmt_ref[:, hh * MEM_TOKENS:(hh + 1) * MEM_TOKENS] = jnp.where(head_v == hh, vm_t, 0.0).astype(_BF16)

    @pl.when(pl.program_id(1) == 0)
    def _():
        carry_ref[...] = jnp.zeros_like(carry_ref)

    carry = {"c": carry_ref[...]}

    def tile_stages(local):
        rows = slice(local * tm, (local + 1) * tm)
        v = {}

        def st_h():
            v["h"] = _rms(x_ref[0, rows, :], gpre_ref[...]).astype(_BF16)

        def st_nn():
            t_nn = _dot(v["h"], wnn_ref[...])
            v["t_zu"], v["t_zv"], v["k"] = t_nn[:, 0:A_WIDTH], t_nn[:, A_WIDTH:K_OFF], t_nn[:, K_OFF:NN_COLS]

        def st_qt():
            v["q_t"] = _dot_nt(wt_ref[0:B_WIDTH, :], v["h"]) * (SCALE * LOG2E)

        def st_nt():
            t_t = _dot_nt(wt_ref[V_ROW:NT_ROWS, :], v["h"])
            v["v_t"], v["t_f"] = t_t[0:F_ROW - V_ROW], t_t[F_ROW - V_ROW:QM_ROW - V_ROW]
            v["qm_t"] = (t_t[QM_ROW - V_ROW:] * (SCALE * LOG2E)).astype(_BF16)

        def st_u():
            v["u"] = _gelu_tanh(v["t_zu"])

        def st_vn():
            v["vn"] = _rms(_gelu_tanh(v["t_zv"]), gsgu_ref[...])

        def st_vt_store():
            ones_v = jnp.where(lax.broadcasted_iota(jnp.int32, (VT_ROWS - HEAD_DIM, tm), 0) == 0, 1.0, 0.0)
            for hd in range(B_HEADS):
                vt_ref[0, hd, local] = jnp.concatenate([v["v_t"][hd * HEAD_DIM:(hd + 1) * HEAD_DIM], ones_v],
                                                       axis=0).astype(_BF16)

        rowf = lax.broadcasted_iota(jnp.int32, (F_ROWS, tm), 0)

        def st_gate_parts():
            logf = _log_sigmoid(v["t_f"] + bf_ref[...])
            logf = jnp.where(rowf < SUBLANES * B_HEADS, logf, 0.0)
            v["parts"] = jnp.concatenate(_split3(logf), axis=0).astype(_BF16)

        def st_gate_dot():
            triu = (lax.broadcasted_iota(jnp.int32, (tm, tm), 0)
                    <= lax.broadcasted_iota(jnp.int32, (tm, tm), 1)).astype(_BF16)
            v["cs"] = _dot(v["parts"], triu)

        def st_gate_bias():
            cs = v["cs"]
            c = cs[0:F_ROWS] + cs[F_ROWS:2 * F_ROWS] + cs[2 * F_ROWS:] + carry["c"]
            carry["c"] = c[:, tm - 1:tm]
            c_hi, c_mid, c_lo = _split3(c * LOG2E)
            j8 = rowf % SUBLANES
            v["cq"] = jnp.where(j8 == 0, c_hi, jnp.where(j8 == 1, c_mid, jnp.where(j8 == 2, c_lo,
                                                                                     jnp.where(j8 < 6, 1.0, 0.0))))
            ck = jnp.where(j8 < 3, 1.0, jnp.where(j8 == 3, -c_hi, jnp.where(j8 == 4, -c_mid,
                                                                               jnp.where(j8 == 5, -c_lo, 0.0))))
            v["ck_t"] = jnp.concatenate([ck, jnp.zeros((LANES - F_ROWS, tm), _F32)], axis=0).T

        def st_sgu():
            lane = lax.broadcasted_iota(jnp.int32, (CHUNK, LANES), 1)
            row_s = lax.broadcasted_iota(jnp.int32, (CHUNK, 2 * CHUNK), 0)
            col_s = lax.broadcasted_iota(jnp.int32, (CHUNK, 2 * CHUNK), 1) % CHUNK
            z_pairs = []
            for p in range(PAIRS):
                w_pair = jnp.where(col_s <= row_s, ws_ref[p], 0.0).astype(_BF16)
                z_chunks = []
                for c_i in range(tm // CHUNK):
                    v_pair = v["vn"][c_i * CHUNK:(c_i + 1) * CHUNK, p * LANES:(p + 1) * LANES]
                    rhs = jnp.concatenate([jnp.where(lane < HEAD_DIM, v_pair, 0.0),
                                           jnp.where(lane >= HEAD_DIM, v_pair, 0.0)], axis=0).astype(_BF16)
                    z_chunks.append(_dot(w_pair, rhs) + bs_ref[:, p * LANES:(p + 1) * LANES])
                z_pairs.append(jnp.concatenate(z_chunks, axis=0))
            v["z"] = jnp.concatenate(z_pairs, axis=1)

        def st_ya():
            ya_ref[0, rows, :] = _rms(v["u"] * v["z"], goa_ref[...]).astype(_BF16)

        def st_q_tables():
            zeros_q = jnp.zeros((LANES - HEAD_DIM - SUBLANES, tm), _F32)
            for hd in range(B_HEADS):
                q_h = v["q_t"][hd * HEAD_DIM:(hd + 1) * HEAD_DIM]
                cq_h = v["cq"][SUBLANES * hd:SUBLANES * (hd + 1)]
                q_aug = [q_h, cq_h, zeros_q] if hd % 2 == 0 else [cq_h, zeros_q, q_h]
                qt_ref[0, hd, local] = jnp.concatenate(q_aug, axis=0).astype(_BF16)

        def st_k_tables(heads):
            lane_t = lax.broadcasted_iota(jnp.int32, (tm, LANES), 1)
            for hd in heads:
                k_pair = v["k"][:, (hd // 2) * LANES:(hd // 2 + 1) * LANES]
                if hd % 2 == 0:
                    k_aug = jnp.where(lane_t < HEAD_DIM, k_pair, pltpu.roll(v["ck_t"], HEAD_DIM - SUBLANES * hd, 1))
                else:
                    k_aug = jnp.where(lane_t >= HEAD_DIM, k_pair, pltpu.roll(v["ck_t"], LANES - SUBLANES * hd, 1))
                ka_ref[0, hd, rows, :] = k_aug.astype(_BF16)

        def st_mem_qk():
            v["s_m"] = _dot(km_ref[...], v["qm_t"])

        def st_mem_softmax(hh):
            s = v["s_m"][hh * MEM_TOKENS:(hh + 1) * MEM_TOKENS]
            e = jnp.exp2(s - jnp.max(s, axis=0, keepdims=True))
            v["p_m", hh] = (e * (1.0 / jnp.sum(e, axis=0, keepdims=True))).astype(_BF16)

        def st_mem_pv():
            p_all = jnp.concatenate([v.pop(("p_m", hh)) for hh in range(M_HEADS)], axis=0)
            v["o_t"] = _dot(vmt_ref[...], p_all)

        def st_ym():
            o_t = v["o_t"]
            ym_t = o_t * lax.rsqrt(jnp.mean(o_t * o_t, axis=0, keepdims=True) + EPS)
            ym_ref[0, rows, :] = (ym_t.T * gom_ref[...]).astype(_BF16)

        P = functools.partial
        return (st_h, st_nn, st_nt, st_u, st_vn, st_qt, st_vt_store,
                st_mem_qk, st_gate_parts, st_sgu, st_gate_dot, P(st_mem_softmax, 0),
                st_ya, P(st_mem_softmax, 1), st_gate_bias,
                P(st_mem_softmax, 2), st_q_tables,
                P(st_mem_softmax, 3), P(st_k_tables, (0, 1, 2)), st_mem_pv, P(st_k_tables, (3, 4, 5)),
                st_ym)

    for local in range(MIX_TILES):
        for stage in tile_stages(local):
            stage()
    carry_ref[...] = carry["c"]


def _mix_in(x, g_pre, w_nn, w_t, b_f_col, g_sgu, ws_cat, bs_exp, g_out_a, g_out_m, mem, g_mem, w_mk, w_mv_t,
            w_out, layer):
    batch, seq, _ = x.shape
    tm = T_ATT
    nblk = seq // tm
    rows = MIX_TILES * tm
    steps = nblk // MIX_TILES
    slab = D_MODEL // (batch * steps)
    assert slab * batch * steps == D_MODEL and slab % BF16_ROWS == 0
    m_lo = OUT_HALF - A_WIDTH
    assert A_WIDTH % slab == 0 and m_lo % slab == 0 and B_WIDTH % slab == 0
    e_a, e_m, e_b = A_WIDTH // slab, OUT_HALF // slab, (OUT_HALF + B_WIDTH) // slab

    def regroup(dst):
        return jnp.where(dst < e_a, dst, jnp.where(dst < e_m, dst + B_WIDTH // slab,
                                                   jnp.where(dst < e_b, dst - m_lo // slab, dst)))

    const2 = lambda b, i: (0, 0)
    return pl.pallas_call(
        _mix_in_kernel,
        grid=(batch, nblk // MIX_TILES),
        in_specs=[
            pl.BlockSpec((1, rows, D_MODEL), lambda b, i: (b, i, 0)),
            pl.BlockSpec((1, D_MODEL), const2),
            pl.BlockSpec((D_MODEL, NN_COLS), const2),
            pl.BlockSpec((NT_ROWS, D_MODEL), const2),
            pl.BlockSpec((F_ROWS, 1), const2),
            pl.BlockSpec((1, A_WIDTH), const2),
            pl.BlockSpec((PAIRS, CHUNK, 2 * CHUNK), lambda b, i: (0, 0, 0)),
            pl.BlockSpec((CHUNK, A_WIDTH), const2),
            pl.BlockSpec((1, A_WIDTH), const2),
            pl.BlockSpec((1, M_WIDTH), const2),
            pl.BlockSpec((1, MEM_TOKENS, D_MODEL), lambda b, i: (b, 0, 0)),
            pl.BlockSpec((1, D_MODEL), const2),
            pl.BlockSpec((D_MODEL, M_WIDTH), const2),
            pl.BlockSpec((M_WIDTH, D_MODEL), const2),
            pl.BlockSpec((1, slab, D_MODEL), lambda b, i: (layer, regroup(b * steps + i), 0)),
        ],
        out_specs=[
            pl.BlockSpec((1, B_HEADS, MIX_TILES, LANES, tm), lambda b, i: (b, 0, i, 0, 0)),
            pl.BlockSpec((1, B_HEADS, rows, LANES), lambda b, i: (b, 0, i, 0)),
            pl.BlockSpec((1, B_HEADS, MIX_TILES, VT_ROWS, tm), lambda b, i: (b, 0, i, 0, 0)),
            pl.BlockSpec((1, rows, A_WIDTH), lambda b, i: (b, i, 0)),
            pl.BlockSpec((1, rows, M_WIDTH), lambda b, i: (b, i, 0)),
            pl.BlockSpec((slab, D_MODEL), lambda b, i: (b * steps + i, 0)),
        ],
        out_shape=[jax.ShapeDtypeStruct((batch, B_HEADS, nblk, LANES, tm), _BF16),
                   jax.ShapeDtypeStruct((batch, B_HEADS, seq, LANES), _BF16),
                   jax.ShapeDtypeStruct((batch, B_HEADS, nblk, VT_ROWS, tm), _BF16),
                   jax.ShapeDtypeStruct((batch, seq, A_WIDTH), _BF16),
                   jax.ShapeDtypeStruct((batch, seq, M_WIDTH), _BF16),
                   jax.ShapeDtypeStruct((D_MODEL, D_MODEL), _BF16)],
        scratch_shapes=[pltpu.VMEM((F_ROWS, 1), _F32),
                        pltpu.VMEM((M_HEADS * MEM_TOKENS, M_WIDTH), _BF16),
                        pltpu.VMEM((M_WIDTH, M_HEADS * MEM_TOKENS), _BF16)],
        compiler_params=pltpu.CompilerParams(dimension_semantics=("parallel", "arbitrary"),
                                             vmem_limit_bytes=VMEM_LIMIT),
        name="mix_in",
    )(x, g_pre, w_nn, w_t, b_f_col, g_sgu, ws_cat, bs_exp, g_out_a, g_out_m, mem, g_mem, w_mk, w_mv_t, w_out)


def _fox_out_kernel(qt_ref, ka_ref, vt_ref, ya_ref, ym_ref, x_ref, wout_ref, gob_ref, gpost_ref,
                    wg_ref, wu_ref, wd_ref, x1_ref, wg16_ref, wu16_ref, wd16_ref):
    t = T_ATT
    wg16_ref[...] = wg_ref[0].astype(_BF16)
    wu16_ref[...] = wu_ref[0].astype(_BF16)
    wd16_ref[...] = wd_ref[0].astype(_BF16)
    causal = (lax.broadcasted_iota(jnp.int32, (t, t), 0) <= lax.broadcasted_iota(jnp.int32, (t, t), 1))

    w = t // Q_SPLIT
    chains = [(hd, qh) for hd in range(B_HEADS) for qh in range(Q_SPLIT)]

    def attend(local, j, carry, diag):
        start = j * t

        def qk(c):
            hd, qh = chains[c]
            nk = (qh + 1) * w if diag else t
            return _dot(ka_ref[0, hd, pl.ds(start, nk), :], qt_ref[0, hd, local, :, qh * w:(qh + 1) * w])

        scores = {c: qk(c) for c in range(QK_AHEAD)}
        out = []
        for c, (hd, qh) in enumerate(chains):
            m, acc = carry[c]
            s = scores.pop(c)
            nk = s.shape[0]
            if diag:
                s = jnp.where(causal[0:nk, qh * w:(qh + 1) * w], s, NEG_INF)
            m_new = jnp.maximum(m, jnp.max(s, axis=0, keepdims=True))
            p = jnp.exp2(s - m_new).astype(_BF16)
            if c + QK_AHEAD < len(chains):
                scores[c + QK_AHEAD] = qk(c + QK_AHEAD)
            out.append((m_new, jnp.exp2(m - m_new) * acc + _dot(vt_ref[0, hd, j, :, 0:nk], p)))
        return tuple(out)

    def query_tile(local, n_off):
        rows = slice(local * t, (local + 1) * t)
        y_am = jnp.concatenate([ya_ref[0, rows, :], ym_ref[0, rows, 0:OUT_HALF - A_WIDTH]], axis=1)
        z_am = _dot(y_am, wout_ref[0:OUT_HALF, :])
        state = tuple((jnp.full((1, w), NEG_INF, _F32), jnp.zeros((VT_ROWS, w), _F32)) for _ in chains)
        for j in range(n_off):
            state = attend(local, j, state, False)
        state = attend(local, n_off, state, True)
        heads = []
        for hd in range(B_HEADS):
            acc = jnp.concatenate([state[hd * Q_SPLIT + qh][1] for qh in range(Q_SPLIT)], axis=1)
            heads.append(acc[0:HEAD_DIM] * (1.0 / acc[HEAD_DIM:HEAD_DIM + 1]))
        yb_t = jnp.concatenate(heads, axis=0)
        yb_t = yb_t * lax.rsqrt(jnp.mean(yb_t * yb_t, axis=0, keepdims=True) + EPS)
        yb_n = (yb_t.T * gob_ref[...]).astype(_BF16)
        y_bm = jnp.concatenate([yb_n, ym_ref[0, rows, OUT_HALF - A_WIDTH:]], axis=1)
        z = z_am + _dot(y_bm, wout_ref[OUT_HALF:, :])
        x1_ref[0, rows, :] = x_ref[0, rows, :] + _rms(z, gpost_ref[...])

    def step(first_tile):
        for local in range(FOX_TILES):
            query_tile(local, first_tile + local)

    for s in range(ka_ref.shape[2] // (t * FOX_TILES)):
        pl.when(pl.program_id(1) == s)(functools.partial(step, s * FOX_TILES))


def _fox_out(qt, ka, vt, ya, ym, x, w_out, g_out_b, g_post, w_gate, w_up, w_down, layer):
    batch, seq, _ = x.shape
    t = T_ATT
    nblk = seq // t
    rows = FOX_TILES * t
    steps = nblk // FOX_TILES
    slab = D_MODEL // (batch * steps)
    slab_d = D_FF // batch
    assert slab * batch * steps == D_MODEL and slab % BF16_ROWS == 0
    assert slab_d * batch == D_FF and slab_d % BF16_ROWS == 0
    const2 = lambda b, i: (0, 0)
    slab_spec = pl.BlockSpec((slab, D_FF), lambda b, i: (b * steps + i, 0))
    slab_d_spec = pl.BlockSpec((slab_d, D_MODEL), lambda b, i: (b, 0))
    slab_in = pl.BlockSpec((1, slab, D_FF), lambda b, i: (layer, b * steps + i, 0))
    slab_d_in = pl.BlockSpec((1, slab_d, D_MODEL), lambda b, i: (layer, b, 0))
    w16 = jax.ShapeDtypeStruct((D_MODEL, D_FF), _BF16)
    return pl.pallas_call(
        _fox_out_kernel,
        grid=(batch, nblk // FOX_TILES),
        in_specs=[
            pl.BlockSpec((1, B_HEADS, FOX_TILES, LANES, t), lambda b, i: (b, 0, i, 0, 0)),
            pl.BlockSpec((1, B_HEADS, seq, LANES), lambda b, i: (b, 0, 0, 0)),
            pl.BlockSpec((1, B_HEADS, nblk, VT_ROWS, t), lambda b, i: (b, 0, 0, 0, 0)),
            pl.BlockSpec((1, rows, A_WIDTH), lambda b, i: (b, i, 0)),
            pl.BlockSpec((1, rows, M_WIDTH), lambda b, i: (b, i, 0)),
            pl.BlockSpec((1, rows, D_MODEL), lambda b, i: (b, i, 0)),
            pl.BlockSpec((D_MODEL, D_MODEL), const2),
            pl.BlockSpec((1, B_WIDTH), const2),
            pl.BlockSpec((1, D_MODEL), const2),
            slab_in, slab_in, slab_d_in,
        ],
        out_specs=[pl.BlockSpec((1, rows, D_MODEL), lambda b, i: (b, i, 0)), slab_spec, slab_spec, slab_d_spec],
        out_shape=[jax.ShapeDtypeStruct(x.shape, x.dtype), w16, w16,
                   jax.ShapeDtypeStruct((D_FF, D_MODEL), _BF16)],
        compiler_params=pltpu.CompilerParams(dimension_semantics=("parallel", "arbitrary"),
                                             vmem_limit_bytes=VMEM_LIMIT),
        name="fox_out",
    )(qt, ka, vt, ya, ym, x, w_out, g_out_b, g_post, w_gate, w_up, w_down)


def _ffn_kernel(x_ref, gpre_ref, wg_ref, wu_ref, wd_ref, gpost_ref, o_ref):
    step = D_FF // FF_SPLIT
    rows = x_ref.shape[0] // FF_ROW_SPLIT

    def pre_norm(r):
        x = x_ref[r * rows:(r + 1) * rows, :]
        return x, _rms(x, gpre_ref[...]).astype(_BF16)

    def finish(r, x, ff):
        o_ref[r * rows:(r + 1) * rows, :] = x + _rms(ff, gpost_ref[...])

    xs, hs = zip(*[pre_norm(r) for r in range(FF_ROW_SPLIT)])
    ffs = [None] * FF_ROW_SPLIT
    for c in range(FF_SPLIT):
        sl = slice(c * step, (c + 1) * step)
        for r in range(FF_ROW_SPLIT):
            g = _dot(hs[r], wg_ref[:, sl])
            a = (g * (1.0 / (1.0 + jnp.exp(-g))) * _dot(hs[r], wu_ref[:, sl])).astype(_BF16)
            part = _dot(a, wd_ref[sl, :])
            ffs[r] = part if ffs[r] is None else ffs[r] + part
            if c == FF_SPLIT - 1:
                finish(r, xs[r], ffs[r])


def _ffn(x, g_pre, w_gate, w_up, w_down, g_post):
    tokens = x.shape[0]
    tm = TM_FFN
    const = lambda i: (0, 0)
    resident = functools.partial(pl.BlockSpec, index_map=const, pipeline_mode=pl.Buffered(1))
    return pl.pallas_call(
        _ffn_kernel,
        grid=(tokens // tm,),
        in_specs=[
            pl.BlockSpec((tm, D_MODEL), lambda i: (i, 0)),
            pl.BlockSpec((1, D_MODEL), const),
            resident((D_MODEL, D_FF)),
            resident((D_MODEL, D_FF)),
            resident((D_FF, D_MODEL)),
            pl.BlockSpec((1, D_MODEL), const),
        ],
        out_specs=pl.BlockSpec((tm, D_MODEL), lambda i: (i, 0)),
        out_shape=jax.ShapeDtypeStruct(x.shape, x.dtype),
        compiler_params=pltpu.CompilerParams(dimension_semantics=("parallel",),
                                             vmem_limit_bytes=VMEM_LIMIT),
        name="ffn",
    )(x, g_pre, w_gate, w_up, w_down, g_post)


def kernel(x, mem, g_pre_mix, w_in, b_f, g_sgu, w_s, b_s, g_out_a, g_out_b, g_out_m, g_mem, w_mem_kv, w_out,
           g_post_mix, g_pre_ffn, w_gate, w_up, w_down, g_post_ffn):
    batch, seq, d = x.shape
    depth = w_in.shape[0]
    row = lambda a: a.reshape(1, -1)
    q_lo = 2 * A_WIDTH
    f_lo = q_lo + 3 * B_WIDTH
    for l in range(depth):
        w = w_in[l]
        w_nn = jnp.concatenate([w[:, :q_lo], w[:, q_lo + B_WIDTH:q_lo + 2 * B_WIDTH]],
                               axis=1).astype(_BF16)
        w_f = jnp.pad(jnp.repeat(w[:, f_lo:f_lo + B_HEADS], SUBLANES, axis=1),
                      ((0, 0), (0, F_ROWS - SUBLANES * B_HEADS)))
        w_t = jnp.concatenate([w[:, q_lo:q_lo + B_WIDTH], w[:, q_lo + 2 * B_WIDTH:f_lo], w_f, w[:, f_lo + B_HEADS:]],
                              axis=1).T.astype(_BF16)
        b_f_col = jnp.pad(jnp.repeat(b_f[l], SUBLANES), (0, F_ROWS - SUBLANES * B_HEADS)).reshape(F_ROWS, 1)
        ws_cat = w_s[l].reshape(PAIRS, 2, CHUNK, CHUNK).transpose(0, 2, 1, 3).reshape(PAIRS, CHUNK, 2 * CHUNK)
        bs_exp = jnp.repeat(b_s[l].T, HEAD_DIM, axis=1)

        qt, ka, vt, ya, ym, wo16 = _mix_in(
            x, row(g_pre_mix[l]), w_nn, w_t, b_f_col, row(g_sgu[l]), ws_cat, bs_exp, row(g_out_a[l]),
            row(g_out_m[l]), mem, row(g_mem[l]), w_mem_kv[l][:, :M_WIDTH].astype(_BF16),
            w_mem_kv[l][:, M_WIDTH:].T.astype(_BF16), w_out, l)
        x, wg16, wu16, wd16 = _fox_out(qt, ka, vt, ya, ym, x, wo16, row(g_out_b[l]),
                                       row(g_post_mix[l]), w_gate, w_up, w_down, l)
        x = _ffn(x.reshape(batch * seq, d), row(g_pre_ffn[l]), wg16, wu16, wd16,
                 row(g_post_ffn[l])).reshape(batch, seq, d)
    return x
```

```python
import functools

import jax
import jax.numpy as jnp
from jax import lax
from jax.experimental import pallas as pl
from jax.experimental.pallas import tpu as pltpu

D_MODEL = 1024
HEAD_DIM = 64
A_GROUPS = 6
B_HEADS = 6
M_HEADS = 4
A_WIDTH = A_GROUPS * HEAD_DIM
B_WIDTH = B_HEADS * HEAD_DIM
M_WIDTH = M_HEADS * HEAD_DIM
CHUNK = 128
MEM_TOKENS = 256
D_FF = 2816
EPS = 1e-6
NEG_INF = -1e30

LANES = 128
SUBLANES = 8
BF16_ROWS = 16
PAIRS = B_HEADS // 2
SCALE = HEAD_DIM ** -0.5
LOG2E = 1.4426950408889634

K_OFF = 2 * A_WIDTH
NN_COLS = K_OFF + B_WIDTH
V_ROW = B_WIDTH
F_ROW = 2 * B_WIDTH
F_ROWS = 64
QM_ROW = F_ROW + F_ROWS
NT_ROWS = QM_ROW + M_WIDTH
VT_ROWS = HEAD_DIM + BF16_ROWS
OUT_HALF = (A_WIDTH + B_WIDTH + M_WIDTH) // 2

T_ATT = 512
MIX_TILES = 4
FOX_TILES = 2
Q_SPLIT = 2
QK_AHEAD = 6
TM_FFN = 1024
FF_ROW_SPLIT = 2
FF_SPLIT = 11

VMEM_LIMIT = 56 * 1024 * 1024

_F32 = jnp.float32
_BF16 = jnp.bfloat16


def _dot(a, b):
    return jnp.dot(a, b, preferred_element_type=_F32)


def _dot_nt(a, b):
    return lax.dot_general(a, b, (((1,), (1,)), ((), ())), preferred_element_type=_F32)


def _rms(x, g):
    return x * lax.rsqrt(jnp.mean(x * x, axis=-1, keepdims=True) + EPS) * g


def _gelu_tanh(x):
    return 0.5 * x * (1.0 + jnp.tanh(0.7978845608028654 * (x + 0.044715 * (x * x * x))))


def _log_sigmoid(x):
    return -(jnp.maximum(-x, 0.0) + jnp.log1p(jnp.exp(-jnp.abs(x))))


def _split3(x):
    hi = x.astype(_BF16).astype(_F32)
    r = x - hi
    mid = r.astype(_BF16).astype(_F32)
    lo = (r - mid).astype(_BF16).astype(_F32)
    return hi, mid, lo


def _mix_in_kernel(x_ref, gpre_ref, wnn_ref, wt_ref, bf_ref, gsgu_ref, ws_ref, bs_ref, goa_ref, gom_ref,
                   mem_ref, gmem_ref, wmk_ref, wmvt_ref, wout_ref, qt_ref, ka_ref, vt_ref, ya_ref, ym_ref,
                   wout16_ref, carry_ref, km_ref, vmt_ref):
    tm = T_ATT
    wout16_ref[...] = wout_ref[0].astype(_BF16)

    mn = _rms(mem_ref[0], gmem_ref[...]).astype(_BF16)
    km = _dot(mn, wmk_ref[...])
    vm_t = _dot_nt(wmvt_ref[...], mn)
    head_k = lax.broadcasted_iota(jnp.int32, (MEM_TOKENS, M_WIDTH), 1) // HEAD_DIM
    head_v = lax.broadcasted_iota(jnp.int32, (M_WIDTH, MEM_TOKENS), 0) // HEAD_DIM
    for hh in range(M_HEADS):
        km_ref[hh * MEM_TOKENS:(hh + 1) * MEM_TOKENS, :] = jnp.where(head_k == hh, km, 0.0).astype(_BF16)
        vmt_ref[:, hh * MEM_TOKENS:(hh + 1) * MEM_TOKENS] = jnp.where(head_v == hh, vm_t, 0.0).astype(_BF16)

    @pl.when(pl.program_id(1) == 0)
    def _():
        carry_ref[...] = jnp.zeros_like(carry_ref)

    carry = {"c": carry_ref[...]}

    def tile_stages(local):
        rows = slice(local * tm, (local + 1) * tm)
        v = {}

        def st_h():
            v["h"] = _rms(x_ref[0, rows, :], gpre_ref[...]).astype(_BF16)

        def st_nn():
            t_nn = _dot(v["h"], wnn_ref[...])
            v["t_zu"], v["t_zv"], v["k"] = t_nn[:, 0:A_WIDTH], t_nn[:, A_WIDTH:K_OFF], t_nn[:, K_OFF:NN_COLS]

        def st_qt():
            v["q_t"] = _dot_nt(wt_ref[0:B_WIDTH, :], v["h"]) * (SCALE * LOG2E)

        def st_nt():
            t_t = _dot_nt(wt_ref[V_ROW:NT_ROWS, :], v["h"])
            v["v_t"], v["t_f"] = t_t[0:F_ROW - V_ROW], t_t[F_ROW - V_ROW:QM_ROW - V_ROW]
            v["qm_t"] = (t_t[QM_ROW - V_ROW:] * (SCALE * LOG2E)).astype(_BF16)

        def st_u():
            v["u"] = _gelu_tanh(v["t_zu"])

        def st_vn():
            v["vn"] = _rms(_gelu_tanh(v["t_zv"]), gsgu_ref[...])

        def st_vt_store():
            ones_v = jnp.where(lax.broadcasted_iota(jnp.int32, (VT_ROWS - HEAD_DIM, tm), 0) == 0, 1.0, 0.0)
            for hd in range(B_HEADS):
                vt_ref[0, hd, local] = jnp.concatenate([v["v_t"][hd * HEAD_DIM:(hd + 1) * HEAD_DIM], ones_v],
                                                       axis=0).astype(_BF16)

        rowf = lax.broadcasted_iota(jnp.int32, (F_ROWS, tm), 0)

        def st_gate_parts():
            logf = _log_sigmoid(v["t_f"] + bf_ref[...])
            logf = jnp.where(rowf < SUBLANES * B_HEADS, logf, 0.0)
            v["parts"] = jnp.concatenate(_split3(logf), axis=0).astype(_BF16)

        def st_gate_dot():
            half = tm // 2
            triu = (lax.broadcasted_iota(jnp.int32, (half, half), 0)
                    <= lax.broadcasted_iota(jnp.int32, (half, half), 1)).astype(_BF16)
            parts = v["parts"]
            local = _dot(jnp.concatenate([parts[:, 0:half], parts[:, half:]], axis=0), triu)
            first, second = local[0:3 * F_ROWS], local[3 * F_ROWS:]
            v["cs"] = jnp.concatenate([first, second + first[:, half - 1:half]], axis=1)

        def st_gate_bias():
            cs = v["cs"]
            c = cs[0:F_ROWS] + cs[F_ROWS:2 * F_ROWS] + cs[2 * F_ROWS:] + carry["c"]
            carry["c"] = c[:, tm - 1:tm]
            c_hi, c_mid, c_lo = _split3(c * LOG2E)
            j8 = rowf % SUBLANES
            v["cq"] = jnp.where(j8 == 0, c_hi, jnp.where(j8 == 1, c_mid, jnp.where(j8 == 2, c_lo,
                                                                                     jnp.where(j8 < 6, 1.0, 0.0))))
            ck = jnp.where(j8 < 3, 1.0, jnp.where(j8 == 3, -c_hi, jnp.where(j8 == 4, -c_mid,
                                                                               jnp.where(j8 == 5, -c_lo, 0.0))))
            v["ck_t"] = jnp.concatenate([ck, jnp.zeros((LANES - F_ROWS, tm), _F32)], axis=0).T

        def st_sgu():
            lane = lax.broadcasted_iota(jnp.int32, (CHUNK, LANES), 1)
            row_s = lax.broadcasted_iota(jnp.int32, (CHUNK, 2 * CHUNK), 0)
            col_s = lax.broadcasted_iota(jnp.int32, (CHUNK, 2 * CHUNK), 1) % CHUNK
            z_pairs = []
            for p in range(PAIRS):
                w_pair = jnp.where(col_s <= row_s, ws_ref[p], 0.0).astype(_BF16)
                bias = bs_ref[:, p * LANES:(p + 1) * LANES]
                z_chunks = []
                for c_i in range(0, tm // CHUNK, 2):
                    rhs = []
                    for cc in (c_i, c_i + 1):
                        v_pair = v["vn"][cc * CHUNK:(cc + 1) * CHUNK, p * LANES:(p + 1) * LANES]
                        rhs.append(jnp.concatenate([jnp.where(lane < HEAD_DIM, v_pair, 0.0),
                                                    jnp.where(lane >= HEAD_DIM, v_pair, 0.0)], axis=0))
                    zz = _dot(w_pair, jnp.concatenate(rhs, axis=1).astype(_BF16))
                    z_chunks += [zz[:, 0:LANES] + bias, zz[:, LANES:] + bias]
                z_pairs.append(jnp.concatenate(z_chunks, axis=0))
            v["z"] = jnp.concatenate(z_pairs, axis=1)

        def st_ya():
            ya_ref[0, rows, :] = _rms(v["u"] * v["z"], goa_ref[...]).astype(_BF16)

        def st_q_tables():
            zeros_q = jnp.zeros((LANES - HEAD_DIM - SUBLANES, tm), _F32)
            for hd in range(B_HEADS):
                q_h = v["q_t"][hd * HEAD_DIM:(hd + 1) * HEAD_DIM]
                cq_h = v["cq"][SUBLANES * hd:SUBLANES * (hd + 1)]
                q_aug = [q_h, cq_h, zeros_q] if hd % 2 == 0 else [cq_h, zeros_q, q_h]
                qt_ref[0, hd, local] = jnp.concatenate(q_aug, axis=0).astype(_BF16)

        def st_k_tables(heads):
            lane_t = lax.broadcasted_iota(jnp.int32, (tm, LANES), 1)
            for hd in heads:
                k_pair = v["k"][:, (hd // 2) * LANES:(hd // 2 + 1) * LANES]
                if hd % 2 == 0:
                    k_aug = jnp.where(lane_t < HEAD_DIM, k_pair, pltpu.roll(v["ck_t"], HEAD_DIM - SUBLANES * hd, 1))
                else:
                    k_aug = jnp.where(lane_t >= HEAD_DIM, k_pair, pltpu.roll(v["ck_t"], LANES - SUBLANES * hd, 1))
                ka_ref[0, hd, rows, :] = k_aug.astype(_BF16)

        def st_mem_qk():
            v["s_m"] = _dot(km_ref[...], v["qm_t"])

        def st_mem_softmax(hh):
            s = v["s_m"][hh * MEM_TOKENS:(hh + 1) * MEM_TOKENS]
            e = jnp.exp2(s - jnp.max(s, axis=0, keepdims=True))
            v["p_m", hh] = (e * (1.0 / jnp.sum(e, axis=0, keepdims=True))).astype(_BF16)

        def st_mem_pv():
            p_all = jnp.concatenate([v.pop(("p_m", hh)) for hh in range(M_HEADS)], axis=0)
            v["o_t"] = _dot(vmt_ref[...], p_all)

        def st_ym():
            o_t = v["o_t"]
            ym_t = o_t * lax.rsqrt(jnp.mean(o_t * o_t, axis=0, keepdims=True) + EPS)
            ym_ref[0, rows, :] = (ym_t.T * gom_ref[...]).astype(_BF16)

        P = functools.partial
        return (st_h, st_nn, st_nt, st_u, st_vn, st_qt, st_vt_store,
                st_mem_qk, st_gate_parts, st_sgu, st_gate_dot, P(st_mem_softmax, 0),
                st_ya, P(st_mem_softmax, 1), st_gate_bias,
                P(st_mem_softmax, 2), st_q_tables,
                P(st_mem_softmax, 3), P(st_k_tables, (0, 1, 2)), st_mem_pv, P(st_k_tables, (3, 4, 5)),
                st_ym)

    for local in range(MIX_TILES):
        for stage in tile_stages(local):
            stage()
    carry_ref[...] = carry["c"]


def _mix_in(x, g_pre, w_nn, w_t, b_f_col, g_sgu, ws_cat, bs_exp, g_out_a, g_out_m, mem, g_mem, w_mk, w_mv_t,
            w_out, layer):
    batch, seq, _ = x.shape
    tm = T_ATT
    nblk = seq // tm
    rows = MIX_TILES * tm
    steps = nblk // MIX_TILES
    slab = D_MODEL // (batch * steps)
    assert slab * batch * steps == D_MODEL and slab % BF16_ROWS == 0
    m_lo = OUT_HALF - A_WIDTH
    assert A_WIDTH % slab == 0 and m_lo % slab == 0 and B_WIDTH % slab == 0
    e_a, e_m, e_b = A_WIDTH // slab, OUT_HALF // slab, (OUT_HALF + B_WIDTH) // slab

    def regroup(dst):
        return jnp.where(dst < e_a, dst, jnp.where(dst < e_m, dst + B_WIDTH // slab,
                                                   jnp.where(dst < e_b, dst - m_lo // slab, dst)))

    const2 = lambda b, i: (0, 0)
    return pl.pallas_call(
        _mix_in_kernel,
        grid=(batch, nblk // MIX_TILES),
        in_specs=[
            pl.BlockSpec((1, rows, D_MODEL), lambda b, i: (b, i, 0)),
            pl.BlockSpec((1, D_MODEL), const2),
            pl.BlockSpec((D_MODEL, NN_COLS), const2),
            pl.BlockSpec((NT_ROWS, D_MODEL), const2),
            pl.BlockSpec((F_ROWS, 1), const2),
            pl.BlockSpec((1, A_WIDTH), const2),
            pl.BlockSpec((PAIRS, CHUNK, 2 * CHUNK), lambda b, i: (0, 0, 0)),
            pl.BlockSpec((CHUNK, A_WIDTH), const2),
            pl.BlockSpec((1, A_WIDTH), const2),
            pl.BlockSpec((1, M_WIDTH), const2),
            pl.BlockSpec((1, MEM_TOKENS, D_MODEL), lambda b, i: (b, 0, 0)),
            pl.BlockSpec((1, D_MODEL), const2),
            pl.BlockSpec((D_MODEL, M_WIDTH), const2),
            pl.BlockSpec((M_WIDTH, D_MODEL), const2),
            pl.BlockSpec((1, slab, D_MODEL), lambda b, i: (layer, regroup(b * steps + i), 0)),
        ],
        out_specs=[
            pl.BlockSpec((1, B_HEADS, MIX_TILES, LANES, tm), lambda b, i: (b, 0, i, 0, 0)),
            pl.BlockSpec((1, B_HEADS, rows, LANES), lambda b, i: (b, 0, i, 0)),
            pl.BlockSpec((1, B_HEADS, MIX_TILES, VT_ROWS, tm), lambda b, i: (b, 0, i, 0, 0)),
            pl.BlockSpec((1, rows, A_WIDTH), lambda b, i: (b, i, 0)),
            pl.BlockSpec((1, rows, M_WIDTH), lambda b, i: (b, i, 0)),
            pl.BlockSpec((slab, D_MODEL), lambda b, i: (b * steps + i, 0)),
        ],
        out_shape=[jax.ShapeDtypeStruct((batch, B_HEADS, nblk, LANES, tm), _BF16),
                   jax.ShapeDtypeStruct((batch, B_HEADS, seq, LANES), _BF16),
                   jax.ShapeDtypeStruct((batch, B_HEADS, nblk, VT_ROWS, tm), _BF16),
                   jax.ShapeDtypeStruct((batch, seq, A_WIDTH), _BF16),
                   jax.ShapeDtypeStruct((batch, seq, M_WIDTH), _BF16),
                   jax.ShapeDtypeStruct((D_MODEL, D_MODEL), _BF16)],
        scratch_shapes=[pltpu.VMEM((F_ROWS, 1), _F32),
                        pltpu.VMEM((M_HEADS * MEM_TOKENS, M_WIDTH), _BF16),
                        pltpu.VMEM((M_WIDTH, M_HEADS * MEM_TOKENS), _BF16)],
        compiler_params=pltpu.CompilerParams(dimension_semantics=("parallel", "arbitrary"),
                                             vmem_limit_bytes=VMEM_LIMIT),
        name="mix_in",
    )(x, g_pre, w_nn, w_t, b_f_col, g_sgu, ws_cat, bs_exp, g_out_a, g_out_m, mem, g_mem, w_mk, w_mv_t, w_out)


def _fox_out_kernel(qt_ref, ka_ref, vt_ref, ya_ref, ym_ref, x_ref, wout_ref, gob_ref, gpost_ref,
                    wg_ref, wu_ref, wd_ref, x1_ref, wg16_ref, wu16_ref, wd16_ref):
    t = T_ATT
    wg16_ref[...] = wg_ref[0].astype(_BF16)
    wu16_ref[...] = wu_ref[0].astype(_BF16)
    wd16_ref[...] = wd_ref[0].astype(_BF16)
    causal = (lax.broadcasted_iota(jnp.int32, (t, t), 0) <= lax.broadcasted_iota(jnp.int32, (t, t), 1))

    w = t // Q_SPLIT
    chains = [(hd, qh) for hd in range(B_HEADS) for qh in range(Q_SPLIT)]

    def attend(local, j, carry, diag):
        start = j * t

        def qk(c):
            hd, qh = chains[c]
            nk = (qh + 1) * w if diag else t
            return _dot(ka_ref[0, hd, pl.ds(start, nk), :], qt_ref[0, hd, local, :, qh * w:(qh + 1) * w])

        scores = {c: qk(c) for c in range(QK_AHEAD)}
        out = []
        for c, (hd, qh) in enumerate(chains):
            s = scores.pop(c)
            nk = s.shape[0]
            if diag:
                s = jnp.where(causal[0:nk, qh * w:(qh + 1) * w], s, NEG_INF)
            m_new = jnp.max(s, axis=0, keepdims=True)
            if carry is not None:
                m_new = jnp.maximum(carry[c][0], m_new)
            p = jnp.exp2(s - m_new).astype(_BF16)
            if c + QK_AHEAD < len(chains):
                scores[c + QK_AHEAD] = qk(c + QK_AHEAD)
            acc = _dot(vt_ref[0, hd, j, :, 0:nk], p)
            if carry is not None:
                acc = jnp.exp2(carry[c][0] - m_new) * carry[c][1] + acc
            out.append((m_new, acc))
        return tuple(out)

    def query_tile(local, n_off):
        rows = slice(local * t, (local + 1) * t)
        y_am = jnp.concatenate([ya_ref[0, rows, :], ym_ref[0, rows, 0:OUT_HALF - A_WIDTH]], axis=1)
        z_am = _dot(y_am, wout_ref[0:OUT_HALF, :])
        state = None
        for j in range(n_off):
            state = attend(local, j, state, False)
        state = attend(local, n_off, state, True)
        heads = []
        for hd in range(B_HEADS):
            acc = jnp.concatenate([state[hd * Q_SPLIT + qh][1] for qh in range(Q_SPLIT)], axis=1)
            heads.append(acc[0:HEAD_DIM] * (1.0 / acc[HEAD_DIM:HEAD_DIM + 1]))
        yb_t = jnp.concatenate(heads, axis=0)
        yb_t = yb_t * lax.rsqrt(jnp.mean(yb_t * yb_t, axis=0, keepdims=True) + EPS)
        yb_n = (yb_t.T * gob_ref[...]).astype(_BF16)
        y_bm = jnp.concatenate([yb_n, ym_ref[0, rows, OUT_HALF - A_WIDTH:]], axis=1)
        z = z_am + _dot(y_bm, wout_ref[OUT_HALF:, :])
        x1_ref[0, rows, :] = x_ref[0, rows, :] + _rms(z, gpost_ref[...])

    def step(first_tile):
        for local in range(FOX_TILES):
            query_tile(local, first_tile + local)

    for s in range(ka_ref.shape[2] // (t * FOX_TILES)):
        pl.when(pl.program_id(1) == s)(functools.partial(step, s * FOX_TILES))


def _fox_out(qt, ka, vt, ya, ym, x, w_out, g_out_b, g_post, w_gate, w_up, w_down, layer):
    batch, seq, _ = x.shape
    t = T_ATT
    nblk = seq // t
    rows = FOX_TILES * t
    steps = nblk // FOX_TILES
    slab = D_MODEL // (batch * steps)
    slab_d = D_FF // batch
    assert slab * batch * steps == D_MODEL and slab % BF16_ROWS == 0
    assert slab_d * batch == D_FF and slab_d % BF16_ROWS == 0
    const2 = lambda b, i: (0, 0)
    slab_spec = pl.BlockSpec((slab, D_FF), lambda b, i: (b * steps + i, 0))
    slab_d_spec = pl.BlockSpec((slab_d, D_MODEL), lambda b, i: (b, 0))
    slab_in = pl.BlockSpec((1, slab, D_FF), lambda b, i: (layer, b * steps + i, 0))
    slab_d_in = pl.BlockSpec((1, slab_d, D_MODEL), lambda b, i: (layer, b, 0))
    w16 = jax.ShapeDtypeStruct((D_MODEL, D_FF), _BF16)
    return pl.pallas_call(
        _fox_out_kernel,
        grid=(batch, nblk // FOX_TILES),
        in_specs=[
            pl.BlockSpec((1, B_HEADS, FOX_TILES, LANES, t), lambda b, i: (b, 0, i, 0, 0)),
            pl.BlockSpec((1, B_HEADS, seq, LANES), lambda b, i: (b, 0, 0, 0)),
            pl.BlockSpec((1, B_HEADS, nblk, VT_ROWS, t), lambda b, i: (b, 0, 0, 0, 0)),
            pl.BlockSpec((1, rows, A_WIDTH), lambda b, i: (b, i, 0)),
            pl.BlockSpec((1, rows, M_WIDTH), lambda b, i: (b, i, 0)),
            pl.BlockSpec((1, rows, D_MODEL), lambda b, i: (b, i, 0)),
            pl.BlockSpec((D_MODEL, D_MODEL), const2),
            pl.BlockSpec((1, B_WIDTH), const2),
            pl.BlockSpec((1, D_MODEL), const2),
            slab_in, slab_in, slab_d_in,
        ],
        out_specs=[pl.BlockSpec((1, rows, D_MODEL), lambda b, i: (b, i, 0)), slab_spec, slab_spec, slab_d_spec],
        out_shape=[jax.ShapeDtypeStruct(x.shape, x.dtype), w16, w16,
                   jax.ShapeDtypeStruct((D_FF, D_MODEL), _BF16)],
        compiler_params=pltpu.CompilerParams(dimension_semantics=("parallel", "arbitrary"),
                                             vmem_limit_bytes=VMEM_LIMIT),
        name="fox_out",
    )(qt, ka, vt, ya, ym, x, w_out, g_out_b, g_post, w_gate, w_up, w_down)


def _ffn_kernel(x_ref, gpre_ref, wg_ref, wu_ref, wd_ref, gpost_ref, o_ref):
    step = D_FF // FF_SPLIT
    rows = x_ref.shape[0] // FF_ROW_SPLIT

    def pre_norm(r):
        x = x_ref[r * rows:(r + 1) * rows, :]
        return x, _rms(x, gpre_ref[...]).astype(_BF16)

    def finish(r, x, ff):
        o_ref[r * rows:(r + 1) * rows, :] = x + _rms(ff, gpost_ref[...])

    xs, hs = zip(*[pre_norm(r) for r in range(FF_ROW_SPLIT)])
    ffs = [None] * FF_ROW_SPLIT
    for c in range(FF_SPLIT):
        sl = slice(c * step, (c + 1) * step)
        for r in range(FF_ROW_SPLIT):
            g = _dot(hs[r], wg_ref[:, sl])
            a = (g * (1.0 / (1.0 + jnp.exp(-g))) * _dot(hs[r], wu_ref[:, sl])).astype(_BF16)
            part = _dot(a, wd_ref[sl, :])
            ffs[r] = part if ffs[r] is None else ffs[r] + part
            if c == FF_SPLIT - 1:
                finish(r, xs[r], ffs[r])


def _ffn(x, g_pre, w_gate, w_up, w_down, g_post):
    tokens = x.shape[0]
    tm = TM_FFN
    const = lambda i: (0, 0)
    resident = functools.partial(pl.BlockSpec, index_map=const, pipeline_mode=pl.Buffered(1))
    return pl.pallas_call(
        _ffn_kernel,
        grid=(tokens // tm,),
        in_specs=[
            pl.BlockSpec((tm, D_MODEL), lambda i: (i, 0)),
            pl.BlockSpec((1, D_MODEL), const),
            resident((D_MODEL, D_FF)),
            resident((D_MODEL, D_FF)),
            resident((D_FF, D_MODEL)),
            pl.BlockSpec((1, D_MODEL), const),
        ],
        out_specs=pl.BlockSpec((tm, D_MODEL), lambda i: (i, 0)),
        out_shape=jax.ShapeDtypeStruct(x.shape, x.dtype),
        compiler_params=pltpu.CompilerParams(dimension_semantics=("parallel",),
                                             vmem_limit_bytes=VMEM_LIMIT),
        name="ffn",
    )(x, g_pre, w_gate, w_up, w_down, g_post)


def kernel(x, mem, g_pre_mix, w_in, b_f, g_sgu, w_s, b_s, g_out_a, g_out_b, g_out_m, g_mem, w_mem_kv, w_out,
           g_post_mix, g_pre_ffn, w_gate, w_up, w_down, g_post_ffn):
    batch, seq, d = x.shape
    depth = w_in.shape[0]
    row = lambda a: a.reshape(1, -1)
    q_lo = 2 * A_WIDTH
    f_lo = q_lo + 3 * B_WIDTH
    for l in range(depth):
        w = w_in[l]
        w_nn = jnp.concatenate([w[:, :q_lo], w[:, q_lo + B_WIDTH:q_lo + 2 * B_WIDTH]],
                               axis=1).astype(_BF16)
        w_f = jnp.pad(jnp.repeat(w[:, f_lo:f_lo + B_HEADS], SUBLANES, axis=1),
                      ((0, 0), (0, F_ROWS - SUBLANES * B_HEADS)))
        w_t = jnp.concatenate([w[:, q_lo:q_lo + B_WIDTH], w[:, q_lo + 2 * B_WIDTH:f_lo], w_f, w[:, f_lo + B_HEADS:]],
                              axis=1).T.astype(_BF16)
        b_f_col = jnp.pad(jnp.repeat(b_f[l], SUBLANES), (0, F_ROWS - SUBLANES * B_HEADS)).reshape(F_ROWS, 1)
        ws_cat = w_s[l].reshape(PAIRS, 2, CHUNK, CHUNK).transpose(0, 2, 1, 3).reshape(PAIRS, CHUNK, 2 * CHUNK)
        bs_exp = jnp.repeat(b_s[l].T, HEAD_DIM, axis=1)

        qt, ka, vt, ya, ym, wo16 = _mix_in(
            x, row(g_pre_mix[l]), w_nn, w_t, b_f_col, row(g_sgu[l]), ws_cat, bs_exp, row(g_out_a[l]),
            row(g_out_m[l]), mem, row(g_mem[l]), w_mem_kv[l][:, :M_WIDTH].astype(_BF16),
            w_mem_kv[l][:, M_WIDTH:].T.astype(_BF16), w_out, l)
        x, wg16, wu16, wd16 = _fox_out(qt, ka, vt, ya, ym, x, wo16, row(g_out_b[l]),
                                       row(g_post_mix[l]), w_gate, w_up, w_down, l)
        x = _ffn(x.reshape(batch * seq, d), row(g_pre_ffn[l]), wg16, wu16, wd16,
                 row(g_post_ffn[l])).reshape(batch, seq, d)
    return x
```

```python
import functools

import jax
import jax.numpy as jnp
from jax import lax
from jax.experimental import pallas as pl
from jax.experimental.pallas import tpu as pltpu

D_MODEL = 1024
HEAD_DIM = 64
A_GROUPS = 6
B_HEADS = 6
M_HEADS = 4
A_WIDTH = A_GROUPS * HEAD_DIM
B_WIDTH = B_HEADS * HEAD_DIM
M_WIDTH = M_HEADS * HEAD_DIM
CHUNK = 128
MEM_TOKENS = 256
D_FF = 2816
EPS = 1e-6
NEG_INF = -1e30

LANES = 128
SUBLANES = 8
BF16_ROWS = 16
PAIRS = B_HEADS // 2
SCALE = HEAD_DIM ** -0.5
LOG2E = 1.4426950408889634

K_OFF = 2 * A_WIDTH
NN_COLS = K_OFF + B_WIDTH
V_ROW = B_WIDTH
F_ROW = 2 * B_WIDTH
F_ROWS = 64
QM_ROW = F_ROW + F_ROWS
NT_ROWS = QM_ROW + M_WIDTH
VT_ROWS = HEAD_DIM + BF16_ROWS
OUT_HALF = (A_WIDTH + B_WIDTH + M_WIDTH) // 2

T_ATT = 512
MIX_TILES = 4
FOX_TILES = 2
Q_SPLIT = 2
QK_AHEAD = 6
TM_FFN = 1024
FF_ROW_SPLIT = 2
FF_SPLIT = 11

VMEM_LIMIT = 56 * 1024 * 1024

_F32 = jnp.float32
_BF16 = jnp.bfloat16


def _dot(a, b):
    return jnp.dot(a, b, preferred_element_type=_F32)


def _dot_nt(a, b):
    return lax.dot_general(a, b, (((1,), (1,)), ((), ())), preferred_element_type=_F32)


def _rms(x, g):
    return x * lax.rsqrt(jnp.mean(x * x, axis=-1, keepdims=True) + EPS) * g


def _gelu_tanh(x):
    return 0.5 * x * (1.0 + jnp.tanh(0.7978845608028654 * (x + 0.044715 * (x * x * x))))


def _log_sigmoid(x):
    return -(jnp.maximum(-x, 0.0) + jnp.log1p(jnp.exp(-jnp.abs(x))))


def _split3(x):
    hi = x.astype(_BF16).astype(_F32)
    r = x - hi
    mid = r.astype(_BF16).astype(_F32)
    lo = (r - mid).astype(_BF16).astype(_F32)
    return hi, mid, lo


def _mix_in_kernel(x_ref, gpre_ref, wnn_ref, wt_ref, bf_ref, gsgu_ref, ws_ref, bs_ref, goa_ref, gom_ref,
                   mem_ref, gmem_ref, wmk_ref, wmvt_ref, wout_ref, qt_ref, ka_ref, vt_ref, ya_ref, ym_ref,
                   wout16_ref, carry_ref, km_ref, vmt_ref):
    tm = T_ATT
    wout16_ref[...] = wout_ref[0].astype(_BF16)

    mn = _rms(mem_ref[0], gmem_ref[...]).astype(_BF16)
    km = _dot(mn, wmk_ref[...])
    vm_t = _dot_nt(wmvt_ref[...], mn)
    head_k = lax.broadcasted_iota(jnp.int32, (MEM_TOKENS, M_WIDTH), 1) // HEAD_DIM
    head_v = lax.broadcasted_iota(jnp.int32, (M_WIDTH, MEM_TOKENS), 0) // HEAD_DIM
    for hh in range(M_HEADS):
        km_ref[hh * MEM_TOKENS:(hh + 1) * MEM_TOKENS, :] = jnp.where(head_k == hh, km, 0.0).astype(_BF16)
        vmt_ref[:, hh * MEM_TOKENS:(hh + 1) * MEM_TOKENS] = jnp.where(head_v == hh, vm_t, 0.0).astype(_BF16)

    @pl.when(pl.program_id(1) == 0)
    def _():
        carry_ref[...] = jnp.zeros_like(carry_ref)

    carry = {"c": carry_ref[...]}

    def tile_stages(local):
        rows = slice(local * tm, (local + 1) * tm)
        v = {}

        def st_h():
            v["h"] = _rms(x_ref[0, rows, :], gpre_ref[...]).astype(_BF16)

        def st_nn():
            t_nn = _dot(v["h"], wnn_ref[...])
            v["t_zu"], v["t_zv"], v["k"] = t_nn[:, 0:A_WIDTH], t_nn[:, A_WIDTH:K_OFF], t_nn[:, K_OFF:NN_COLS]

        def st_qt():
            v["q_t"] = _dot_nt(wt_ref[0:B_WIDTH, :], v["h"]) * (SCALE * LOG2E)

        def st_nt():
            t_t = _dot_nt(wt_ref[V_ROW:NT_ROWS, :], v["h"])
            v["v_t"], v["t_f"] = t_t[0:F_ROW - V_ROW], t_t[F_ROW - V_ROW:QM_ROW - V_ROW]
            v["qm_t"] = (t_t[QM_ROW - V_ROW:] * (SCALE * LOG2E)).astype(_BF16)

        def st_u():
            v["u"] = _gelu_tanh(v["t_zu"])

        def st_vn():
            v["vn"] = _rms(_gelu_tanh(v["t_zv"]), gsgu_ref[...])

        def st_vt_store():
            ones_v = jnp.where(lax.broadcasted_iota(jnp.int32, (VT_ROWS - HEAD_DIM, tm), 0) == 0, 1.0, 0.0)
            for hd in range(B_HEADS):
                vt_ref[0, hd, local] = jnp.concatenate([v["v_t"][hd * HEAD_DIM:(hd + 1) * HEAD_DIM], ones_v],
                                                       axis=0).astype(_BF16)

        rowf = lax.broadcasted_iota(jnp.int32, (F_ROWS, tm), 0)

        def st_gate_parts():
            logf = _log_sigmoid(v["t_f"] + bf_ref[...])
            logf = jnp.where(rowf < SUBLANES * B_HEADS, logf, 0.0)
            v["parts"] = jnp.concatenate(_split3(logf), axis=0).astype(_BF16)

        def st_gate_dot():
            half = tm // 2
            triu = (lax.broadcasted_iota(jnp.int32, (half, half), 0)
                    <= lax.broadcasted_iota(jnp.int32, (half, half), 1)).astype(_BF16)
            parts = v["parts"]
            local = _dot(jnp.concatenate([parts[:, 0:half], parts[:, half:]], axis=0), triu)
            first, second = local[0:3 * F_ROWS], local[3 * F_ROWS:]
            v["cs"] = jnp.concatenate([first, second + first[:, half - 1:half]], axis=1)

        def st_gate_bias():
            cs = v["cs"]
            c = cs[0:F_ROWS] + cs[F_ROWS:2 * F_ROWS] + cs[2 * F_ROWS:] + carry["c"]
            carry["c"] = c[:, tm - 1:tm]
            c_hi, c_mid, c_lo = _split3(c * LOG2E)
            j8 = rowf % SUBLANES
            v["cq"] = jnp.where(j8 == 0, c_hi, jnp.where(j8 == 1, c_mid, jnp.where(j8 == 2, c_lo,
                                                                                     jnp.where(j8 < 6, 1.0, 0.0))))
            ck = jnp.where(j8 < 3, 1.0, jnp.where(j8 == 3, -c_hi, jnp.where(j8 == 4, -c_mid,
                                                                               jnp.where(j8 == 5, -c_lo, 0.0))))
            v["ck_t"] = jnp.concatenate([ck, jnp.zeros((LANES - F_ROWS, tm), _F32)], axis=0).T

        def st_sgu():
            lane = lax.broadcasted_iota(jnp.int32, (CHUNK, LANES), 1)
            row_s = lax.broadcasted_iota(jnp.int32, (CHUNK, 2 * CHUNK), 0)
            col_s = lax.broadcasted_iota(jnp.int32, (CHUNK, 2 * CHUNK), 1) % CHUNK
            z_pairs = []
            for p in range(PAIRS):
                w_pair = jnp.where(col_s <= row_s, ws_ref[p], 0.0).astype(_BF16)
                bias = bs_ref[:, p * LANES:(p + 1) * LANES]
                z_chunks = []
                for c_i in range(0, tm // CHUNK, 2):
                    rhs = []
                    for cc in (c_i, c_i + 1):
                        v_pair = v["vn"][cc * CHUNK:(cc + 1) * CHUNK, p * LANES:(p + 1) * LANES]
                        rhs.append(jnp.concatenate([jnp.where(lane < HEAD_DIM, v_pair, 0.0),
                                                    jnp.where(lane >= HEAD_DIM, v_pair, 0.0)], axis=0))
                    zz = _dot(w_pair, jnp.concatenate(rhs, axis=1).astype(_BF16))
                    z_chunks += [zz[:, 0:LANES] + bias, zz[:, LANES:] + bias]
                z_pairs.append(jnp.concatenate(z_chunks, axis=0))
            v["z"] = jnp.concatenate(z_pairs, axis=1)

        def st_ya():
            ya_ref[0, rows, :] = _rms(v["u"] * v["z"], goa_ref[...]).astype(_BF16)

        def st_q_tables():
            zeros_q = jnp.zeros((LANES - HEAD_DIM - SUBLANES, tm), _F32)
            for hd in range(B_HEADS):
                q_h = v["q_t"][hd * HEAD_DIM:(hd + 1) * HEAD_DIM]
                cq_h = v["cq"][SUBLANES * hd:SUBLANES * (hd + 1)]
                q_aug = [q_h, cq_h, zeros_q] if hd % 2 == 0 else [cq_h, zeros_q, q_h]
                qt_ref[0, hd, local] = jnp.concatenate(q_aug, axis=0).astype(_BF16)

        def st_k_tables(heads):
            lane_t = lax.broadcasted_iota(jnp.int32, (tm, LANES), 1)
            for hd in heads:
                k_pair = v["k"][:, (hd // 2) * LANES:(hd // 2 + 1) * LANES]
                if hd % 2 == 0:
                    k_aug = jnp.where(lane_t < HEAD_DIM, k_pair, pltpu.roll(v["ck_t"], HEAD_DIM - SUBLANES * hd, 1))
                else:
                    k_aug = jnp.where(lane_t >= HEAD_DIM, k_pair, pltpu.roll(v["ck_t"], LANES - SUBLANES * hd, 1))
                ka_ref[0, hd, rows, :] = k_aug.astype(_BF16)

        def st_mem_qk():
            v["s_m"] = _dot(km_ref[...], v["qm_t"])

        def st_mem_softmax(hh):
            s = v["s_m"][hh * MEM_TOKENS:(hh + 1) * MEM_TOKENS]
            e = jnp.exp2(s - jnp.max(s, axis=0, keepdims=True))
            v["p_m", hh] = (e * (1.0 / jnp.sum(e, axis=0, keepdims=True))).astype(_BF16)

        def st_mem_pv():
            p_all = jnp.concatenate([v.pop(("p_m", hh)) for hh in range(M_HEADS)], axis=0)
            v["o_t"] = _dot(vmt_ref[...], p_all)

        def st_ym():
            o_t = v["o_t"]
            ym_t = o_t * lax.rsqrt(jnp.mean(o_t * o_t, axis=0, keepdims=True) + EPS)
            ym_ref[0, rows, :] = (ym_t.T * gom_ref[...]).astype(_BF16)

        P = functools.partial
        return (st_h, st_nn, st_nt, st_u, st_vn, st_qt, st_vt_store,
                st_mem_qk, st_gate_parts, st_sgu, st_gate_dot, P(st_mem_softmax, 0),
                st_ya, P(st_mem_softmax, 1), st_gate_bias,
                P(st_mem_softmax, 2), st_q_tables,
                P(st_mem_softmax, 3), P(st_k_tables, (0, 1, 2)), st_mem_pv, P(st_k_tables, (3, 4, 5)),
                st_ym)

    for local in range(MIX_TILES):
        for stage in tile_stages(local):
            stage()
    carry_ref[...] = carry["c"]


def _mix_in(x, g_pre, w_nn, w_t, b_f_col, g_sgu, ws_cat, bs_exp, g_out_a, g_out_m, mem, g_mem, w_mk, w_mv_t,
            w_out, layer):
    batch, seq, _ = x.shape
    tm = T_ATT
    nblk = seq // tm
    rows = MIX_TILES * tm
    steps = nblk // MIX_TILES
    slab = D_MODEL // (batch * steps)
    assert slab * batch * steps == D_MODEL and slab % BF16_ROWS == 0
    m_lo = OUT_HALF - A_WIDTH
    assert A_WIDTH % slab == 0 and m_lo % slab == 0 and B_WIDTH % slab == 0
    e_a, e_m, e_b = A_WIDTH // slab, OUT_HALF // slab, (OUT_HALF + B_WIDTH) // slab

    def regroup(dst):
        return jnp.where(dst < e_a, dst, jnp.where(dst < e_m, dst + B_WIDTH // slab,
                                                   jnp.where(dst < e_b, dst - m_lo // slab, dst)))

    const2 = lambda b, i: (0, 0)
    return pl.pallas_call(
        _mix_in_kernel,
        grid=(batch, nblk // MIX_TILES),
        in_specs=[
            pl.BlockSpec((1, rows, D_MODEL), lambda b, i: (b, i, 0)),
            pl.BlockSpec((1, D_MODEL), const2),
            pl.BlockSpec((D_MODEL, NN_COLS), const2),
            pl.BlockSpec((NT_ROWS, D_MODEL), const2),
            pl.BlockSpec((F_ROWS, 1), const2),
            pl.BlockSpec((1, A_WIDTH), const2),
            pl.BlockSpec((PAIRS, CHUNK, 2 * CHUNK), lambda b, i: (0, 0, 0)),
            pl.BlockSpec((CHUNK, A_WIDTH), const2),
            pl.BlockSpec((1, A_WIDTH), const2),
            pl.BlockSpec((1, M_WIDTH), const2),
            pl.BlockSpec((1, MEM_TOKENS, D_MODEL), lambda b, i: (b, 0, 0)),
            pl.BlockSpec((1, D_MODEL), const2),
            pl.BlockSpec((D_MODEL, M_WIDTH), const2),
            pl.BlockSpec((M_WIDTH, D_MODEL), const2),
            pl.BlockSpec((1, slab, D_MODEL), lambda b, i: (layer, regroup(b * steps + i), 0)),
        ],
        out_specs=[
            pl.BlockSpec((1, B_HEADS, MIX_TILES, LANES, tm), lambda b, i: (b, 0, i, 0, 0)),
            pl.BlockSpec((1, B_HEADS, rows, LANES), lambda b, i: (b, 0, i, 0)),
            pl.BlockSpec((1, B_HEADS, MIX_TILES, VT_ROWS, tm), lambda b, i: (b, 0, i, 0, 0)),
            pl.BlockSpec((1, rows, A_WIDTH), lambda b, i: (b, i, 0)),
            pl.BlockSpec((1, rows, M_WIDTH), lambda b, i: (b, i, 0)),
            pl.BlockSpec((slab, D_MODEL), lambda b, i: (b * steps + i, 0)),
        ],
        out_shape=[jax.ShapeDtypeStruct((batch, B_HEADS, nblk, LANES, tm), _BF16),
                   jax.ShapeDtypeStruct((batch, B_HEADS, seq, LANES), _BF16),
                   jax.ShapeDtypeStruct((batch, B_HEADS, nblk, VT_ROWS, tm), _BF16),
                   jax.ShapeDtypeStruct((batch, seq, A_WIDTH), _BF16),
                   jax.ShapeDtypeStruct((batch, seq, M_WIDTH), _BF16),
                   jax.ShapeDtypeStruct((D_MODEL, D_MODEL), _BF16)],
        scratch_shapes=[pltpu.VMEM((F_ROWS, 1), _F32),
                        pltpu.VMEM((M_HEADS * MEM_TOKENS, M_WIDTH), _BF16),
                        pltpu.VMEM((M_WIDTH, M_HEADS * MEM_TOKENS), _BF16)],
        compiler_params=pltpu.CompilerParams(dimension_semantics=("parallel", "arbitrary"),
                                             vmem_limit_bytes=VMEM_LIMIT),
        name="mix_in",
    )(x, g_pre, w_nn, w_t, b_f_col, g_sgu, ws_cat, bs_exp, g_out_a, g_out_m, mem, g_mem, w_mk, w_mv_t, w_out)


def _fox_out_kernel(qt_ref, ka_ref, vt_ref, ya_ref, ym_ref, x_ref, wout_ref, gob_ref, gpost_ref,
                    wg_ref, wu_ref, wd_ref, x1_ref, wg16_ref, wu16_ref, wd16_ref):
    t = T_ATT
    wg16_ref[...] = wg_ref[0].astype(_BF16)
    wu16_ref[...] = wu_ref[0].astype(_BF16)
    wd16_ref[...] = wd_ref[0].astype(_BF16)
    causal = (lax.broadcasted_iota(jnp.int32, (t, t), 0) <= lax.broadcasted_iota(jnp.int32, (t, t), 1))

    w = t // Q_SPLIT
    chains = [(hd, qh) for hd in range(B_HEADS) for qh in range(Q_SPLIT)]

    def out_proj_am(local):
        rows = slice(local * t, (local + 1) * t)
        y_am = jnp.concatenate([ya_ref[0, rows, :], ym_ref[0, rows, 0:OUT_HALF - A_WIDTH]], axis=1)
        return _dot(y_am, wout_ref[0:OUT_HALF, :])

    def step(first_tile):
        items = [(local, j, j == first_tile + local, c) for local in range(FOX_TILES)
                 for j in range(first_tile + local + 1) for c in range(len(chains))]

        def qk(item):
            local, j, diag, c = item
            hd, qh = chains[c]
            nk = (qh + 1) * w if diag else t
            return _dot(ka_ref[0, hd, pl.ds(j * t, nk), :], qt_ref[0, hd, local, :, qh * w:(qh + 1) * w])

        scores = {i: qk(items[i]) for i in range(QK_AHEAD)}
        z_am = out_proj_am(0)
        state = [None] * len(chains)
        for i, (local, j, diag, c) in enumerate(items):
            hd, qh = chains[c]
            s = scores.pop(i)
            nk = s.shape[0]
            if diag:
                s = jnp.where(causal[0:nk, qh * w:(qh + 1) * w], s, NEG_INF)
            m_new = jnp.max(s, axis=0, keepdims=True)
            if state[c] is not None:
                m_new = jnp.maximum(state[c][0], m_new)
            p = jnp.exp2(s - m_new).astype(_BF16)
            if i + QK_AHEAD < len(items):
                scores[i + QK_AHEAD] = qk(items[i + QK_AHEAD])
            acc = _dot(vt_ref[0, hd, j, :, 0:nk], p)
            if state[c] is not None:
                acc = jnp.exp2(state[c][0] - m_new) * state[c][1] + acc
            state[c] = (m_new, acc)
            if diag and c == len(chains) - 1:
                z_next = out_proj_am(local + 1) if local + 1 < FOX_TILES else None
                finish(local, state, z_am)
                z_am, state = z_next, [None] * len(chains)

    def finish(local, state, z_am):
        rows = slice(local * t, (local + 1) * t)
        heads = []
        for hd in range(B_HEADS):
            acc = jnp.concatenate([state[hd * Q_SPLIT + qh][1] for qh in range(Q_SPLIT)], axis=1)
            heads.append(acc[0:HEAD_DIM] * (1.0 / acc[HEAD_DIM:HEAD_DIM + 1]))
        yb_t = jnp.concatenate(heads, axis=0)
        yb_t = yb_t * lax.rsqrt(jnp.mean(yb_t * yb_t, axis=0, keepdims=True) + EPS)
        yb_n = (yb_t.T * gob_ref[...]).astype(_BF16)
        y_bm = jnp.concatenate([yb_n, ym_ref[0, rows, OUT_HALF - A_WIDTH:]], axis=1)
        z = z_am + _dot(y_bm, wout_ref[OUT_HALF:, :])
        x1_ref[0, rows, :] = x_ref[0, rows, :] + _rms(z, gpost_ref[...])

    for s in range(ka_ref.shape[2] // (t * FOX_TILES)):
        pl.when(pl.program_id(1) == s)(functools.partial(step, s * FOX_TILES))


def _fox_out(qt, ka, vt, ya, ym, x, w_out, g_out_b, g_post, w_gate, w_up, w_down, layer):
    batch, seq, _ = x.shape
    t = T_ATT
    nblk = seq // t
    rows = FOX_TILES * t
    steps = nblk // FOX_TILES
    slab = D_MODEL // (batch * steps)
    slab_d = D_FF // batch
    assert slab * batch * steps == D_MODEL and slab % BF16_ROWS == 0
    assert slab_d * batch == D_FF and slab_d % BF16_ROWS == 0
    const2 = lambda b, i: (0, 0)
    slab_spec = pl.BlockSpec((slab, D_FF), lambda b, i: (b * steps + i, 0))
    slab_d_spec = pl.BlockSpec((slab_d, D_MODEL), lambda b, i: (b, 0))
    slab_in = pl.BlockSpec((1, slab, D_FF), lambda b, i: (layer, b * steps + i, 0))
    slab_d_in = pl.BlockSpec((1, slab_d, D_MODEL), lambda b, i: (layer, b, 0))
    w16 = jax.ShapeDtypeStruct((D_MODEL, D_FF), _BF16)
    return pl.pallas_call(
        _fox_out_kernel,
        grid=(batch, nblk // FOX_TILES),
        in_specs=[
            pl.BlockSpec((1, B_HEADS, FOX_TILES, LANES, t), lambda b, i: (b, 0, i, 0, 0)),
            pl.BlockSpec((1, B_HEADS, seq, LANES), lambda b, i: (b, 0, 0, 0)),
            pl.BlockSpec((1, B_HEADS, nblk, VT_ROWS, t), lambda b, i: (b, 0, 0, 0, 0)),
            pl.BlockSpec((1, rows, A_WIDTH), lambda b, i: (b, i, 0)),
            pl.BlockSpec((1, rows, M_WIDTH), lambda b, i: (b, i, 0)),
            pl.BlockSpec((1, rows, D_MODEL), lambda b, i: (b, i, 0)),
            pl.BlockSpec((D_MODEL, D_MODEL), const2),
            pl.BlockSpec((1, B_WIDTH), const2),
            pl.BlockSpec((1, D_MODEL), const2),
            slab_in, slab_in, slab_d_in,
        ],
        out_specs=[pl.BlockSpec((1, rows, D_MODEL), lambda b, i: (b, i, 0)), slab_spec, slab_spec, slab_d_spec],
        out_shape=[jax.ShapeDtypeStruct(x.shape, x.dtype), w16, w16,
                   jax.ShapeDtypeStruct((D_FF, D_MODEL), _BF16)],
        compiler_params=pltpu.CompilerParams(dimension_semantics=("parallel", "arbitrary"),
                                             vmem_limit_bytes=VMEM_LIMIT),
        name="fox_out",
    )(qt, ka, vt, ya, ym, x, w_out, g_out_b, g_post, w_gate, w_up, w_down)


def _ffn_kernel(x_ref, gpre_ref, wg_ref, wu_ref, wd_ref, gpost_ref, o_ref):
    step = D_FF // FF_SPLIT
    rows = x_ref.shape[0] // FF_ROW_SPLIT

    def pre_norm(r):
        x = x_ref[r * rows:(r + 1) * rows, :]
        return x, _rms(x, gpre_ref[...]).astype(_BF16)

    def finish(r, x, ff):
        o_ref[r * rows:(r + 1) * rows, :] = x + _rms(ff, gpost_ref[...])

    xs, hs = zip(*[pre_norm(r) for r in range(FF_ROW_SPLIT)])
    ffs = [None] * FF_ROW_SPLIT
    for c in range(FF_SPLIT):
        sl = slice(c * step, (c + 1) * step)
        for r in range(FF_ROW_SPLIT):
            g = _dot(hs[r], wg_ref[:, sl])
            a = (g * (1.0 / (1.0 + jnp.exp(-g))) * _dot(hs[r], wu_ref[:, sl])).astype(_BF16)
            part = _dot(a, wd_ref[sl, :])
            ffs[r] = part if ffs[r] is None else ffs[r] + part
            if c == FF_SPLIT - 1:
                finish(r, xs[r], ffs[r])


def _ffn(x, g_pre, w_gate, w_up, w_down, g_post):
    tokens = x.shape[0]
    tm = TM_FFN
    const = lambda i: (0, 0)
    resident = functools.partial(pl.BlockSpec, index_map=const, pipeline_mode=pl.Buffered(1))
    return pl.pallas_call(
        _ffn_kernel,
        grid=(tokens // tm,),
        in_specs=[
            pl.BlockSpec((tm, D_MODEL), lambda i: (i, 0)),
            pl.BlockSpec((1, D_MODEL), const),
            resident((D_MODEL, D_FF)),
            resident((D_MODEL, D_FF)),
            resident((D_FF, D_MODEL)),
            pl.BlockSpec((1, D_MODEL), const),
        ],
        out_specs=pl.BlockSpec((tm, D_MODEL), lambda i: (i, 0)),
        out_shape=jax.ShapeDtypeStruct(x.shape, x.dtype),
        compiler_params=pltpu.CompilerParams(dimension_semantics=("parallel",),
                                             vmem_limit_bytes=VMEM_LIMIT),
        name="ffn",
    )(x, g_pre, w_gate, w_up, w_down, g_post)


def kernel(x, mem, g_pre_mix, w_in, b_f, g_sgu, w_s, b_s, g_out_a, g_out_b, g_out_m, g_mem, w_mem_kv, w_out,
           g_post_mix, g_pre_ffn, w_gate, w_up, w_down, g_post_ffn):
    batch, seq, d = x.shape
    depth = w_in.shape[0]
    row = lambda a: a.reshape(1, -1)
    q_lo = 2 * A_WIDTH
    f_lo = q_lo + 3 * B_WIDTH
    for l in range(depth):
        w = w_in[l]
        w_nn = jnp.concatenate([w[:, :q_lo], w[:, q_lo + B_WIDTH:q_lo + 2 * B_WIDTH]],
                               axis=1).astype(_BF16)
        w_f = jnp.pad(jnp.repeat(w[:, f_lo:f_lo + B_HEADS], SUBLANES, axis=1),
                      ((0, 0), (0, F_ROWS - SUBLANES * B_HEADS)))
        w_t = jnp.concatenate([w[:, q_lo:q_lo + B_WIDTH], w[:, q_lo + 2 * B_WIDTH:f_lo], w_f, w[:, f_lo + B_HEADS:]],
                              axis=1).T.astype(_BF16)
        b_f_col = jnp.pad(jnp.repeat(b_f[l], SUBLANES), (0, F_ROWS - SUBLANES * B_HEADS)).reshape(F_ROWS, 1)
        ws_cat = w_s[l].reshape(PAIRS, 2, CHUNK, CHUNK).transpose(0, 2, 1, 3).reshape(PAIRS, CHUNK, 2 * CHUNK)
        bs_exp = jnp.repeat(b_s[l].T, HEAD_DIM, axis=1)

        qt, ka, vt, ya, ym, wo16 = _mix_in(
            x, row(g_pre_mix[l]), w_nn, w_t, b_f_col, row(g_sgu[l]), ws_cat, bs_exp, row(g_out_a[l]),
            row(g_out_m[l]), mem, row(g_mem[l]), w_mem_kv[l][:, :M_WIDTH].astype(_BF16),
            w_mem_kv[l][:, M_WIDTH:].T.astype(_BF16), w_out, l)
        x, wg16, wu16, wd16 = _fox_out(qt, ka, vt, ya, ym, x, wo16, row(g_out_b[l]),
                                       row(g_post_mix[l]), w_gate, w_up, w_down, l)
        x = _ffn(x.reshape(batch * seq, d), row(g_pre_ffn[l]), wg16, wu16, wd16,
                 row(g_post_ffn[l])).reshape(batch, seq, d)
    return x
```

```python
import functools

import jax
import jax.numpy as jnp
from jax import lax
from jax.experimental import pallas as pl
from jax.experimental.pallas import tpu as pltpu

D_MODEL = 1024
HEAD_DIM = 64
A_GROUPS = 6
B_HEADS = 6
M_HEADS = 4
A_WIDTH = A_GROUPS * HEAD_DIM
B_WIDTH = B_HEADS * HEAD_DIM
M_WIDTH = M_HEADS * HEAD_DIM
CHUNK = 128
MEM_TOKENS = 256
D_FF = 2816
EPS = 1e-6
NEG_INF = -1e30

LANES = 128
SUBLANES = 8
BF16_ROWS = 16
PAIRS = B_HEADS // 2
SCALE = HEAD_DIM ** -0.5
LOG2E = 1.4426950408889634

K_OFF = 2 * A_WIDTH
NN_COLS = K_OFF + B_WIDTH
V_ROW = B_WIDTH
F_ROW = 2 * B_WIDTH
F_ROWS = 64
QM_ROW = F_ROW + F_ROWS
NT_ROWS = QM_ROW + M_WIDTH
VT_ROWS = HEAD_DIM + BF16_ROWS
OUT_HALF = (A_WIDTH + B_WIDTH + M_WIDTH) // 2

T_ATT = 512
MIX_TILES = 4
FOX_TILES = 2
Q_SPLIT = 2
QK_AHEAD = 9
TM_FFN = 1024
FF_ROW_SPLIT = 2
FF_SPLIT = 11

VMEM_LIMIT = 56 * 1024 * 1024

_F32 = jnp.float32
_BF16 = jnp.bfloat16


def _dot(a, b):
    return jnp.dot(a, b, preferred_element_type=_F32)


def _dot_nt(a, b):
    return lax.dot_general(a, b, (((1,), (1,)), ((), ())), preferred_element_type=_F32)


def _rms(x, g):
    return x * lax.rsqrt(jnp.mean(x * x, axis=-1, keepdims=True) + EPS) * g


def _gelu_tanh(x):
    return 0.5 * x * (1.0 + jnp.tanh(0.7978845608028654 * (x + 0.044715 * (x * x * x))))


def _log_sigmoid(x):
    return -(jnp.maximum(-x, 0.0) + jnp.log1p(jnp.exp(-jnp.abs(x))))


def _split3(x):
    hi = x.astype(_BF16).astype(_F32)
    r = x - hi
    mid = r.astype(_BF16).astype(_F32)
    lo = (r - mid).astype(_BF16).astype(_F32)
    return hi, mid, lo


def _mix_in_kernel(x_ref, gpre_ref, wnn_ref, wt_ref, bf_ref, gsgu_ref, ws_ref, bs_ref, goa_ref, gom_ref,
                   mem_ref, gmem_ref, wmk_ref, wmvt_ref, wout_ref, qt_ref, ka_ref, vt_ref, ya_ref, ym_ref,
                   wout16_ref, carry_ref, km_ref, vmt_ref):
    tm = T_ATT
    wout16_ref[...] = wout_ref[0].astype(_BF16)

    mn = _rms(mem_ref[0], gmem_ref[...]).astype(_BF16)
    km = _dot(mn, wmk_ref[...])
    vm_t = _dot_nt(wmvt_ref[...], mn)
    head_k = lax.broadcasted_iota(jnp.int32, (MEM_TOKENS, M_WIDTH), 1) // HEAD_DIM
    head_v = lax.broadcasted_iota(jnp.int32, (M_WIDTH, MEM_TOKENS), 0) // HEAD_DIM
    for hh in range(M_HEADS):
        km_ref[hh * MEM_TOKENS:(hh + 1) * MEM_TOKENS, :] = jnp.where(head_k == hh, km, 0.0).astype(_BF16)
        vmt_ref[:, hh * MEM_TOKENS:(hh + 1) * MEM_TOKENS] = jnp.where(head_v == hh, vm_t, 0.0).astype(_BF16)

    @pl.when(pl.program_id(1) == 0)
    def _():
        carry_ref[...] = jnp.zeros_like(carry_ref)

    carry = {"c": carry_ref[...]}

    def tile_stages(local):
        rows = slice(local * tm, (local + 1) * tm)
        v = {}

        def st_h():
            v["h"] = _rms(x_ref[0, rows, :], gpre_ref[...]).astype(_BF16)

        def st_nn():
            t_nn = _dot(v["h"], wnn_ref[...])
            v["t_zu"], v["t_zv"], v["k"] = t_nn[:, 0:A_WIDTH], t_nn[:, A_WIDTH:K_OFF], t_nn[:, K_OFF:NN_COLS]

        def st_qt():
            v["q_t"] = _dot_nt(wt_ref[0:B_WIDTH, :], v["h"]) * (SCALE * LOG2E)

        def st_nt():
            t_t = _dot_nt(wt_ref[V_ROW:NT_ROWS, :], v["h"])
            v["v_t"], v["t_f"] = t_t[0:F_ROW - V_ROW], t_t[F_ROW - V_ROW:QM_ROW - V_ROW]
            v["qm_t"] = (t_t[QM_ROW - V_ROW:] * (SCALE * LOG2E)).astype(_BF16)

        def st_u():
            v["u"] = _gelu_tanh(v["t_zu"])

        def st_vn():
            v["vn"] = _rms(_gelu_tanh(v["t_zv"]), gsgu_ref[...])

        def st_vt_store():
            ones_v = jnp.where(lax.broadcasted_iota(jnp.int32, (VT_ROWS - HEAD_DIM, tm), 0) == 0, 1.0, 0.0)
            for hd in range(B_HEADS):
                vt_ref[0, hd, local] = jnp.concatenate([v["v_t"][hd * HEAD_DIM:(hd + 1) * HEAD_DIM], ones_v],
                                                       axis=0).astype(_BF16)

        rowf = lax.broadcasted_iota(jnp.int32, (F_ROWS, tm), 0)

        def st_gate_parts():
            logf = _log_sigmoid(v["t_f"] + bf_ref[...])
            logf = jnp.where(rowf < SUBLANES * B_HEADS, logf, 0.0)
            v["parts"] = jnp.concatenate(_split3(logf), axis=0).astype(_BF16)

        def st_gate_dot():
            half = tm // 2
            triu = (lax.broadcasted_iota(jnp.int32, (half, half), 0)
                    <= lax.broadcasted_iota(jnp.int32, (half, half), 1)).astype(_BF16)
            parts = v["parts"]
            local = _dot(jnp.concatenate([parts[:, 0:half], parts[:, half:]], axis=0), triu)
            first, second = local[0:3 * F_ROWS], local[3 * F_ROWS:]
            v["cs"] = jnp.concatenate([first, second + first[:, half - 1:half]], axis=1)

        def st_gate_bias():
            cs = v["cs"]
            c = cs[0:F_ROWS] + cs[F_ROWS:2 * F_ROWS] + cs[2 * F_ROWS:] + carry["c"]
            carry["c"] = c[:, tm - 1:tm]
            c_hi, c_mid, c_lo = _split3(c * LOG2E)
            j8 = rowf % SUBLANES
            v["cq"] = jnp.where(j8 == 0, c_hi, jnp.where(j8 == 1, c_mid, jnp.where(j8 == 2, c_lo,
                                                                                     jnp.where(j8 < 6, 1.0, 0.0))))
            ck = jnp.where(j8 < 3, 1.0, jnp.where(j8 == 3, -c_hi, jnp.where(j8 == 4, -c_mid,
                                                                               jnp.where(j8 == 5, -c_lo, 0.0))))
            v["ck_t"] = jnp.concatenate([ck, jnp.zeros((LANES - F_ROWS, tm), _F32)], axis=0).T

        def st_sgu():
            lane = lax.broadcasted_iota(jnp.int32, (CHUNK, LANES), 1)
            row_s = lax.broadcasted_iota(jnp.int32, (CHUNK, 2 * CHUNK), 0)
            col_s = lax.broadcasted_iota(jnp.int32, (CHUNK, 2 * CHUNK), 1) % CHUNK
            z_pairs = []
            for p in range(PAIRS):
                w_pair = jnp.where(col_s <= row_s, ws_ref[p], 0.0).astype(_BF16)
                bias = bs_ref[:, p * LANES:(p + 1) * LANES]
                z_chunks = []
                for c_i in range(0, tm // CHUNK, 2):
                    rhs = []
                    for cc in (c_i, c_i + 1):
                        v_pair = v["vn"][cc * CHUNK:(cc + 1) * CHUNK, p * LANES:(p + 1) * LANES]
                        rhs.append(jnp.concatenate([jnp.where(lane < HEAD_DIM, v_pair, 0.0),
                                                    jnp.where(lane >= HEAD_DIM, v_pair, 0.0)], axis=0))
                    zz = _dot(w_pair, jnp.concatenate(rhs, axis=1).astype(_BF16))
                    z_chunks += [zz[:, 0:LANES] + bias, zz[:, LANES:] + bias]
                z_pairs.append(jnp.concatenate(z_chunks, axis=0))
            v["z"] = jnp.concatenate(z_pairs, axis=1)

        def st_ya():
            ya_ref[0, rows, :] = _rms(v["u"] * v["z"], goa_ref[...]).astype(_BF16)

        def st_q_tables():
            zeros_q = jnp.zeros((LANES - HEAD_DIM - SUBLANES, tm), _F32)
            for hd in range(B_HEADS):
                q_h = v["q_t"][hd * HEAD_DIM:(hd + 1) * HEAD_DIM]
                cq_h = v["cq"][SUBLANES * hd:SUBLANES * (hd + 1)]
                q_aug = [q_h, cq_h, zeros_q] if hd % 2 == 0 else [cq_h, zeros_q, q_h]
                qt_ref[0, hd, local] = jnp.concatenate(q_aug, axis=0).astype(_BF16)

        def st_k_tables(heads):
            lane_t = lax.broadcasted_iota(jnp.int32, (tm, LANES), 1)
            for hd in heads:
                k_pair = v["k"][:, (hd // 2) * LANES:(hd // 2 + 1) * LANES]
                if hd % 2 == 0:
                    k_aug = jnp.where(lane_t < HEAD_DIM, k_pair, pltpu.roll(v["ck_t"], HEAD_DIM - SUBLANES * hd, 1))
                else:
                    k_aug = jnp.where(lane_t >= HEAD_DIM, k_pair, pltpu.roll(v["ck_t"], LANES - SUBLANES * hd, 1))
                ka_ref[0, hd, rows, :] = k_aug.astype(_BF16)

        def st_mem_qk():
            v["s_m"] = _dot(km_ref[...], v["qm_t"])

        def st_mem_softmax(hh):
            s = v["s_m"][hh * MEM_TOKENS:(hh + 1) * MEM_TOKENS]
            e = jnp.exp2(s - jnp.max(s, axis=0, keepdims=True))
            v["p_m", hh] = (e * (1.0 / jnp.sum(e, axis=0, keepdims=True))).astype(_BF16)

        def st_mem_pv():
            p_all = jnp.concatenate([v.pop(("p_m", hh)) for hh in range(M_HEADS)], axis=0)
            v["o_t"] = _dot(vmt_ref[...], p_all)

        def st_ym():
            o_t = v["o_t"]
            ym_t = o_t * lax.rsqrt(jnp.mean(o_t * o_t, axis=0, keepdims=True) + EPS)
            ym_ref[0, rows, :] = (ym_t.T * gom_ref[...]).astype(_BF16)

        P = functools.partial
        return (st_h, st_nn, st_nt, st_u, st_vn, st_qt, st_vt_store,
                st_mem_qk, st_gate_parts, st_sgu, st_gate_dot, P(st_mem_softmax, 0),
                st_ya, P(st_mem_softmax, 1), st_gate_bias,
                P(st_mem_softmax, 2), st_q_tables,
                P(st_mem_softmax, 3), P(st_k_tables, (0, 1, 2)), st_mem_pv, P(st_k_tables, (3, 4, 5)),
                st_ym)

    for local in range(MIX_TILES):
        for stage in tile_stages(local):
            stage()
    carry_ref[...] = carry["c"]


def _mix_in(x, g_pre, w_nn, w_t, b_f_col, g_sgu, ws_cat, bs_exp, g_out_a, g_out_m, mem, g_mem, w_mk, w_mv_t,
            w_out, layer):
    batch, seq, _ = x.shape
    tm = T_ATT
    nblk = seq // tm
    rows = MIX_TILES * tm
    steps = nblk // MIX_TILES
    slab = D_MODEL // (batch * steps)
    assert slab * batch * steps == D_MODEL and slab % BF16_ROWS == 0
    m_lo = OUT_HALF - A_WIDTH
    assert A_WIDTH % slab == 0 and m_lo % slab == 0 and B_WIDTH % slab == 0
    e_a, e_m, e_b = A_WIDTH // slab, OUT_HALF // slab, (OUT_HALF + B_WIDTH) // slab

    def regroup(dst):
        return jnp.where(dst < e_a, dst, jnp.where(dst < e_m, dst + B_WIDTH // slab,
                                                   jnp.where(dst < e_b, dst - m_lo // slab, dst)))

    const2 = lambda b, i: (0, 0)
    return pl.pallas_call(
        _mix_in_kernel,
        grid=(batch, nblk // MIX_TILES),
        in_specs=[
            pl.BlockSpec((1, rows, D_MODEL), lambda b, i: (b, i, 0)),
            pl.BlockSpec((1, D_MODEL), const2),
            pl.BlockSpec((D_MODEL, NN_COLS), const2),
            pl.BlockSpec((NT_ROWS, D_MODEL), const2),
            pl.BlockSpec((F_ROWS, 1), const2),
            pl.BlockSpec((1, A_WIDTH), const2),
            pl.BlockSpec((PAIRS, CHUNK, 2 * CHUNK), lambda b, i: (0, 0, 0)),
            pl.BlockSpec((CHUNK, A_WIDTH), const2),
            pl.BlockSpec((1, A_WIDTH), const2),
            pl.BlockSpec((1, M_WIDTH), const2),
            pl.BlockSpec((1, MEM_TOKENS, D_MODEL), lambda b, i: (b, 0, 0)),
            pl.BlockSpec((1, D_MODEL), const2),
            pl.BlockSpec((D_MODEL, M_WIDTH), const2),
            pl.BlockSpec((M_WIDTH, D_MODEL), const2),
            pl.BlockSpec((1, slab, D_MODEL), lambda b, i: (layer, regroup(b * steps + i), 0)),
        ],
        out_specs=[
            pl.BlockSpec((1, B_HEADS, MIX_TILES, LANES, tm), lambda b, i: (b, 0, i, 0, 0)),
            pl.BlockSpec((1, B_HEADS, rows, LANES), lambda b, i: (b, 0, i, 0)),
            pl.BlockSpec((1, B_HEADS, MIX_TILES, VT_ROWS, tm), lambda b, i: (b, 0, i, 0, 0)),
            pl.BlockSpec((1, rows, A_WIDTH), lambda b, i: (b, i, 0)),
            pl.BlockSpec((1, rows, M_WIDTH), lambda b, i: (b, i, 0)),
            pl.BlockSpec((slab, D_MODEL), lambda b, i: (b * steps + i, 0)),
        ],
        out_shape=[jax.ShapeDtypeStruct((batch, B_HEADS, nblk, LANES, tm), _BF16),
                   jax.ShapeDtypeStruct((batch, B_HEADS, seq, LANES), _BF16),
                   jax.ShapeDtypeStruct((batch, B_HEADS, nblk, VT_ROWS, tm), _BF16),
                   jax.ShapeDtypeStruct((batch, seq, A_WIDTH), _BF16),
                   jax.ShapeDtypeStruct((batch, seq, M_WIDTH), _BF16),
                   jax.ShapeDtypeStruct((D_MODEL, D_MODEL), _BF16)],
        scratch_shapes=[pltpu.VMEM((F_ROWS, 1), _F32),
                        pltpu.VMEM((M_HEADS * MEM_TOKENS, M_WIDTH), _BF16),
                        pltpu.VMEM((M_WIDTH, M_HEADS * MEM_TOKENS), _BF16)],
        compiler_params=pltpu.CompilerParams(dimension_semantics=("parallel", "arbitrary"),
                                             vmem_limit_bytes=VMEM_LIMIT),
        name="mix_in",
    )(x, g_pre, w_nn, w_t, b_f_col, g_sgu, ws_cat, bs_exp, g_out_a, g_out_m, mem, g_mem, w_mk, w_mv_t, w_out)


def _fox_out_kernel(qt_ref, ka_ref, vt_ref, ya_ref, ym_ref, x_ref, wout_ref, gob_ref, gpost_ref,
                    wg_ref, wu_ref, wd_ref, x1_ref, wg16_ref, wu16_ref, wd16_ref):
    t = T_ATT
    wg16_ref[...] = wg_ref[0].astype(_BF16)
    wu16_ref[...] = wu_ref[0].astype(_BF16)
    wd16_ref[...] = wd_ref[0].astype(_BF16)
    causal = (lax.broadcasted_iota(jnp.int32, (t, t), 0) <= lax.broadcasted_iota(jnp.int32, (t, t), 1))

    w = t // Q_SPLIT
    chains = [(hd, qh) for hd in range(B_HEADS) for qh in range(Q_SPLIT)]

    def out_proj_am(local):
        rows = slice(local * t, (local + 1) * t)
        y_am = jnp.concatenate([ya_ref[0, rows, :], ym_ref[0, rows, 0:OUT_HALF - A_WIDTH]], axis=1)
        return _dot(y_am, wout_ref[0:OUT_HALF, :])

    def step(first_tile):
        items = [(local, j, j == first_tile + local, c) for local in range(FOX_TILES)
                 for j in range(first_tile + local + 1) for c in range(len(chains))]

        def qk(item):
            local, j, diag, c = item
            hd, qh = chains[c]
            nk = (qh + 1) * w if diag else t
            return _dot(ka_ref[0, hd, pl.ds(j * t, nk), :], qt_ref[0, hd, local, :, qh * w:(qh + 1) * w])

        scores = {i: qk(items[i]) for i in range(QK_AHEAD)}
        z_am = out_proj_am(0)
        state = [None] * len(chains)
        for i, (local, j, diag, c) in enumerate(items):
            hd, qh = chains[c]
            s = scores.pop(i)
            nk = s.shape[0]
            if diag:
                s = jnp.where(causal[0:nk, qh * w:(qh + 1) * w], s, NEG_INF)
            m_new = jnp.max(s, axis=0, keepdims=True)
            if state[c] is not None:
                m_new = jnp.maximum(state[c][0], m_new)
            p = jnp.exp2(s - m_new).astype(_BF16)
            if i + QK_AHEAD < len(items):
                scores[i + QK_AHEAD] = qk(items[i + QK_AHEAD])
            acc = _dot(vt_ref[0, hd, j, :, 0:nk], p)
            if state[c] is not None:
                acc = jnp.exp2(state[c][0] - m_new) * state[c][1] + acc
            state[c] = (m_new, acc)
            if diag and c == len(chains) - 1:
                z_next = out_proj_am(local + 1) if local + 1 < FOX_TILES else None
                finish(local, state, z_am)
                z_am, state = z_next, [None] * len(chains)

    def finish(local, state, z_am):
        rows = slice(local * t, (local + 1) * t)
        heads = []
        for hd in range(B_HEADS):
            acc = jnp.concatenate([state[hd * Q_SPLIT + qh][1] for qh in range(Q_SPLIT)], axis=1)
            heads.append(acc[0:HEAD_DIM] * (1.0 / acc[HEAD_DIM:HEAD_DIM + 1]))
        yb_t = jnp.concatenate(heads, axis=0)
        yb_t = yb_t * lax.rsqrt(jnp.mean(yb_t * yb_t, axis=0, keepdims=True) + EPS)
        yb_n = (yb_t.T * gob_ref[...]).astype(_BF16)
        y_bm = jnp.concatenate([yb_n, ym_ref[0, rows, OUT_HALF - A_WIDTH:]], axis=1)
        z = z_am + _dot(y_bm, wout_ref[OUT_HALF:, :])
        x1_ref[0, rows, :] = x_ref[0, rows, :] + _rms(z, gpost_ref[...])

    for s in range(ka_ref.shape[2] // (t * FOX_TILES)):
        pl.when(pl.program_id(1) == s)(functools.partial(step, s * FOX_TILES))


def _fox_out(qt, ka, vt, ya, ym, x, w_out, g_out_b, g_post, w_gate, w_up, w_down, layer):
    batch, seq, _ = x.shape
    t = T_ATT
    nblk = seq // t
    rows = FOX_TILES * t
    steps = nblk // FOX_TILES
    slab = D_MODEL // (batch * steps)
    slab_d = D_FF // batch
    assert slab * batch * steps == D_MODEL and slab % BF16_ROWS == 0
    assert slab_d * batch == D_FF and slab_d % BF16_ROWS == 0
    const2 = lambda b, i: (0, 0)
    slab_spec = pl.BlockSpec((slab, D_FF), lambda b, i: (b * steps + i, 0))
    slab_d_spec = pl.BlockSpec((slab_d, D_MODEL), lambda b, i: (b, 0))
    slab_in = pl.BlockSpec((1, slab, D_FF), lambda b, i: (layer, b * steps + i, 0))
    slab_d_in = pl.BlockSpec((1, slab_d, D_MODEL), lambda b, i: (layer, b, 0))
    w16 = jax.ShapeDtypeStruct((D_MODEL, D_FF), _BF16)
    return pl.pallas_call(
        _fox_out_kernel,
        grid=(batch, nblk // FOX_TILES),
        in_specs=[
            pl.BlockSpec((1, B_HEADS, FOX_TILES, LANES, t), lambda b, i: (b, 0, i, 0, 0)),
            pl.BlockSpec((1, B_HEADS, seq, LANES), lambda b, i: (b, 0, 0, 0)),
            pl.BlockSpec((1, B_HEADS, nblk, VT_ROWS, t), lambda b, i: (b, 0, 0, 0, 0)),
            pl.BlockSpec((1, rows, A_WIDTH), lambda b, i: (b, i, 0)),
            pl.BlockSpec((1, rows, M_WIDTH), lambda b, i: (b, i, 0)),
            pl.BlockSpec((1, rows, D_MODEL), lambda b, i: (b, i, 0)),
            pl.BlockSpec((D_MODEL, D_MODEL), const2),
            pl.BlockSpec((1, B_WIDTH), const2),
            pl.BlockSpec((1, D_MODEL), const2),
            slab_in, slab_in, slab_d_in,
        ],
        out_specs=[pl.BlockSpec((1, rows, D_MODEL), lambda b, i: (b, i, 0)), slab_spec, slab_spec, slab_d_spec],
        out_shape=[jax.ShapeDtypeStruct(x.shape, x.dtype), w16, w16,
                   jax.ShapeDtypeStruct((D_FF, D_MODEL), _BF16)],
        compiler_params=pltpu.CompilerParams(dimension_semantics=("parallel", "arbitrary"),
                                             vmem_limit_bytes=VMEM_LIMIT),
        name="fox_out",
    )(qt, ka, vt, ya, ym, x, w_out, g_out_b, g_post, w_gate, w_up, w_down)


def _ffn_kernel(x_ref, gpre_ref, wg_ref, wu_ref, wd_ref, gpost_ref, o_ref):
    step = D_FF // FF_SPLIT
    rows = x_ref.shape[0] // FF_ROW_SPLIT

    def pre_norm(r):
        x = x_ref[r * rows:(r + 1) * rows, :]
        return x, _rms(x, gpre_ref[...]).astype(_BF16)

    def finish(r, x, ff):
        o_ref[r * rows:(r + 1) * rows, :] = x + _rms(ff, gpost_ref[...])

    xs, hs = zip(*[pre_norm(r) for r in range(FF_ROW_SPLIT)])
    ffs = [None] * FF_ROW_SPLIT
    for c in range(FF_SPLIT):
        sl = slice(c * step, (c + 1) * step)
        for r in range(FF_ROW_SPLIT):
            g = _dot(hs[r], wg_ref[:, sl])
            a = (g * (1.0 / (1.0 + jnp.exp(-g))) * _dot(hs[r], wu_ref[:, sl])).astype(_BF16)
            part = _dot(a, wd_ref[sl, :])
            ffs[r] = part if ffs[r] is None else ffs[r] + part
            if c == FF_SPLIT - 1:
                finish(r, xs[r], ffs[r])


def _ffn(x, g_pre, w_gate, w_up, w_down, g_post):
    tokens = x.shape[0]
    tm = TM_FFN
    const = lambda i: (0, 0)
    resident = functools.partial(pl.BlockSpec, index_map=const, pipeline_mode=pl.Buffered(1))
    return pl.pallas_call(
        _ffn_kernel,
        grid=(tokens // tm,),
        in_specs=[
            pl.BlockSpec((tm, D_MODEL), lambda i: (i, 0)),
            pl.BlockSpec((1, D_MODEL), const),
            resident((D_MODEL, D_FF)),
            resident((D_MODEL, D_FF)),
            resident((D_FF, D_MODEL)),
            pl.BlockSpec((1, D_MODEL), const),
        ],
        out_specs=pl.BlockSpec((tm, D_MODEL), lambda i: (i, 0)),
        out_shape=jax.ShapeDtypeStruct(x.shape, x.dtype),
        compiler_params=pltpu.CompilerParams(dimension_semantics=("parallel",),
                                             vmem_limit_bytes=VMEM_LIMIT),
        name="ffn",
    )(x, g_pre, w_gate, w_up, w_down, g_post)


def kernel(x, mem, g_pre_mix, w_in, b_f, g_sgu, w_s, b_s, g_out_a, g_out_b, g_out_m, g_mem, w_mem_kv, w_out,
           g_post_mix, g_pre_ffn, w_gate, w_up, w_down, g_post_ffn):
    batch, seq, d = x.shape
    depth = w_in.shape[0]
    row = lambda a: a.reshape(1, -1)
    q_lo = 2 * A_WIDTH
    f_lo = q_lo + 3 * B_WIDTH
    for l in range(depth):
        w = w_in[l]
        w_nn = jnp.concatenate([w[:, :q_lo], w[:, q_lo + B_WIDTH:q_lo + 2 * B_WIDTH]],
                               axis=1).astype(_BF16)
        w_f = jnp.pad(jnp.repeat(w[:, f_lo:f_lo + B_HEADS], SUBLANES, axis=1),
                      ((0, 0), (0, F_ROWS - SUBLANES * B_HEADS)))
        w_t = jnp.concatenate([w[:, q_lo:q_lo + B_WIDTH], w[:, q_lo + 2 * B_WIDTH:f_lo], w_f, w[:, f_lo + B_HEADS:]],
                              axis=1).T.astype(_BF16)
        b_f_col = jnp.pad(jnp.repeat(b_f[l], SUBLANES), (0, F_ROWS - SUBLANES * B_HEADS)).reshape(F_ROWS, 1)
        ws_cat = w_s[l].reshape(PAIRS, 2, CHUNK, CHUNK).transpose(0, 2, 1, 3).reshape(PAIRS, CHUNK, 2 * CHUNK)
        bs_exp = jnp.repeat(b_s[l].T, HEAD_DIM, axis=1)

        qt, ka, vt, ya, ym, wo16 = _mix_in(
            x, row(g_pre_mix[l]), w_nn, w_t, b_f_col, row(g_sgu[l]), ws_cat, bs_exp, row(g_out_a[l]),
            row(g_out_m[l]), mem, row(g_mem[l]), w_mem_kv[l][:, :M_WIDTH].astype(_BF16),
            w_mem_kv[l][:, M_WIDTH:].T.astype(_BF16), w_out, l)
        x, wg16, wu16, wd16 = _fox_out(qt, ka, vt, ya, ym, x, wo16, row(g_out_b[l]),
                                       row(g_post_mix[l]), w_gate, w_up, w_down, l)
        x = _ffn(x.reshape(batch * seq, d), row(g_pre_ffn[l]), wg16, wu16, wd16,
                 row(g_post_ffn[l])).reshape(batch, seq, d)
    return x
```
